```python
import math
import jax, jax.numpy as jnp
from jax import lax
import numpy as np

D_MODEL = 1024
BATCH = 8
SEQ = 2048
DEPTH = 4
DEC_BATCH = 128
DEC_SEQ = 4
PAST_LEN = 16384
PAGE_SIZE = 128

D_A = D_MODEL // 4
N_HEADS_A = 4
HEAD_DIM_A = D_A // N_HEADS_A
CHUNK = 128
D_B = D_MODEL // 4
N_HEADS_B = 4
HEAD_DIM_B = D_B // N_HEADS_B
CONV_W = 4
LRU_C = 8.0
D_C = D_MODEL // 4
GROUP_C = 16
N_GROUPS_C = D_C // GROUP_C
P_STATE = 64
DT_MIN = 1e-3
DT_MAX = 1e-1
D_D = D_MODEL // 4
POOL_WINDOWS = (2, 4, 8, 16)
N_GROUPS_D = len(POOL_WINDOWS)
GROUP_D = D_D // N_GROUPS_D
POOL_BUF = max(POOL_WINDOWS) - 1
N_BRANCH = 4
D_BRANCH = D_MODEL // 4
SPLIT_POINTS = (D_A, 2 * D_A, 2 * D_A + D_B, 2 * D_A + D_B + D_C, 2 * D_A + D_B + D_C + D_D)
IN_COLS = 2 * D_A + D_B + D_C + D_D + N_BRANCH * D_MODEL
D_FF = 2816
N_EXPERTS = 8
TOP_K = 2
D_FF_EXPERT = 3584
N_DENSE = (DEPTH + 1) // 2
N_MOE = DEPTH // 2
EPS = 1e-6

kernel_name = 'hybrid_gated_branch_decoder_step'


def rmsnorm(x, g):
    xf = x.astype(jnp.float32)
    y = xf * lax.rsqrt(jnp.mean(xf * xf, axis=-1, keepdims=True) + EPS)
    return (y * g.astype(jnp.float32)).astype(x.dtype)


def layernorm(x, g, b):
    xf = x.astype(jnp.float32)
    mu = jnp.mean(xf, axis=-1, keepdims=True)
    var = jnp.mean(jnp.square(xf - mu), axis=-1, keepdims=True)
    y = (xf - mu) * lax.rsqrt(var + EPS)
    return (y * g.astype(jnp.float32) + b.astype(jnp.float32)).astype(x.dtype)


def _combine(left, right):
    a_l, b_l = left
    a_r, b_r = right
    return a_l * a_r, a_r * b_l + b_r


def linear_scan(a, b, h0):
    b = b.at[:, 0].add(a[:, 0] * h0)
    _, h = lax.associative_scan(_combine, (a, b), axis=1)
    return h


def chunk_spatial_gating(u, v, w_s, b_s):
    bsz, T, _ = v.shape
    L = min(T, CHUNK)
    n_chunks = -(-T // L)
    pad = n_chunks * L - T
    vp = jnp.pad(v, ((0, 0), (0, pad), (0, 0))).reshape(bsz, n_chunks, L, N_HEADS_A, HEAD_DIM_A)
    mask = jnp.tril(jnp.ones((L, L), dtype=bool))
    w = jnp.where(mask[None], w_s[:, :L, :L], 0.0).astype(v.dtype)
    mixed = jnp.einsum('hts,bcshd->bcthd', w, vp) + b_s[:, :L].T[None, None, :, :, None]
    mixed = mixed.reshape(bsz, n_chunks * L, D_A)[:, :T]
    return u * mixed


def causal_depthwise_conv(x, buf, w, b):
    T = x.shape[1]
    xp = jnp.concatenate([buf.astype(x.dtype), x], axis=1)
    out = b + xp[:, 0:T] * w[0]
    for k in range(1, CONV_W):
        out = out + xp[:, k:k + T] * w[k]
    return out, xp[:, T:]


def rg_lru(x, h0, w_a, b_a, w_x, b_x, lam, start_pos):
    bsz, T, _ = x.shape
    xh = x.reshape(bsz, T, N_HEADS_B, HEAD_DIM_B)
    r = jax.nn.sigmoid((jnp.einsum('bthi,hij->bthj', xh, w_a).reshape(bsz, T, D_B) + b_a).astype(jnp.float32))
    i = jax.nn.sigmoid((jnp.einsum('bthi,hij->bthj', xh, w_x).reshape(bsz, T, D_B) + b_x).astype(jnp.float32))
    log_a = LRU_C * r * jax.nn.log_sigmoid(lam.astype(jnp.float32))
    a = jnp.exp(log_a)
    pos = start_pos + jnp.arange(T)
    mult = jnp.where((pos == 0)[None, :, None], 1.0, jnp.sqrt(-jnp.expm1(2.0 * log_a)))
    h = linear_scan(a, mult * i * x.astype(jnp.float32), h0.astype(jnp.float32))
    return h.astype(x.dtype), h[:, -1]


def s5_layer(u, s0_re, s0_im, a_re, a_im, log_dt, b_re, b_im, c_re, c_im, d_skip, w_glu, b_glu):
    bsz, T, _ = u.shape
    uf = u.astype(jnp.float32).reshape(bsz, T, N_GROUPS_C, GROUP_C)
    lam = lax.complex(a_re.astype(jnp.float32), a_im.astype(jnp.float32))
    dt = jnp.exp(log_dt.astype(jnp.float32))[:, None]
    lam_bar = jnp.exp(lam * dt)
    b_mat = lax.complex(b_re.astype(jnp.float32), b_im.astype(jnp.float32))
    b_bar = ((lam_bar - 1.0) / lam)[:, :, None] * b_mat
    c_mat = lax.complex(c_re.astype(jnp.float32), c_im.astype(jnp.float32))
    bu = jnp.einsum('gpc,btgc->btgp', b_bar, uf.astype(jnp.complex64))
    s0 = lax.complex(s0_re.astype(jnp.float32), s0_im.astype(jnp.float32))
    s = linear_scan(jnp.broadcast_to(lam_bar, bu.shape), bu, s0)
    y = jnp.einsum('gcp,btgp->btgc', c_mat, s).real + d_skip.astype(jnp.float32).reshape(N_GROUPS_C, GROUP_C) * uf
    y = jax.nn.gelu(y.reshape(bsz, T, D_C))
    y = y * jax.nn.sigmoid(y @ w_glu.astype(jnp.float32) + b_glu.astype(jnp.float32))
    s_last = s[:, -1]
    return y.astype(u.dtype), s_last.real, s_last.imag


def multiscale_pool(p, buf, w_pool, scale, start_pos):
    bsz, T, _ = p.shape
    xp = jnp.concatenate([buf.astype(p.dtype), p], axis=1).astype(jnp.float32)
    cs = jnp.concatenate([jnp.zeros_like(xp[:, :1]), jnp.cumsum(xp, axis=1)], axis=1)
    end = cs[:, POOL_BUF + 1:]
    tok = xp[:, POOL_BUF:]
    pos = start_pos + jnp.arange(T)
    outs = []
    for g, w in enumerate(POOL_WINDOWS):
        lo, hi = g * GROUP_D, (g + 1) * GROUP_D
        start = cs[:, POOL_BUF + 1 - w:POOL_BUF + 1 - w + T, lo:hi]
        count = jnp.minimum(w, pos + 1).astype(jnp.float32)[None, :, None]
        diff = (end[..., lo:hi] - start) / count - tok[..., lo:hi]
        outs.append(jnp.einsum('btc,cd->btd', diff, w_pool[g].astype(jnp.float32)))
    out = jnp.concatenate(outs, axis=-1) * scale.astype(jnp.float32)
    return out.astype(p.dtype), xp[:, T:].astype(p.dtype)


def token_mixer(xn, conv_buf, lru_h, s5_re, s5_im, pool_buf, start_pos, lp):
    (w_in, b_in, gmlp_ln_g, gmlp_ln_b, gmlp_w_s, gmlp_b_s, conv_w, conv_b,
     lru_w_a, lru_b_a, lru_w_x, lru_b_x, lru_lam,
     s5_a_re, s5_a_im, s5_log_dt, s5_b_re, s5_b_im, s5_c_re, s5_c_im, s5_d, s5_w_glu, s5_b_glu,
     pool_w, pool_scale, w_branch, w_out) = lp
    bsz, T, _ = xn.shape
    proj = xn @ w_in + b_in
    u, v, xb, xc, xd, gate_logits = jnp.split(proj, SPLIT_POINTS, axis=-1)
    u = jax.nn.gelu(u)
    v = layernorm(jax.nn.gelu(v), gmlp_ln_g, gmlp_ln_b)
    y_a = chunk_spatial_gating(u, v, gmlp_w_s, gmlp_b_s)
    xb_conv, new_conv = causal_depthwise_conv(xb, conv_buf, conv_w, conv_b)
    y_b, new_h = rg_lru(xb_conv, lru_h, lru_w_a, lru_b_a, lru_w_x, lru_b_x, lru_lam, start_pos)
    y_c, new_re, new_im = s5_layer(xc, s5_re, s5_im, s5_a_re, s5_a_im, s5_log_dt, s5_b_re, s5_b_im,
                                   s5_c_re, s5_c_im, s5_d, s5_w_glu, s5_b_glu)
    y_d, new_pool = multiscale_pool(xd, pool_buf, pool_w, pool_scale, start_pos)
    gates = jax.nn.sigmoid(gate_logits.reshape(bsz, T, N_BRANCH, D_MODEL))
    merged = jnp.zeros_like(xn)
    for n, y_n in enumerate((y_a, y_b, y_c, y_d)):
        merged = merged + gates[:, :, n] * (y_n @ w_branch[n])
    return merged @ w_out, (new_conv, new_h, new_re, new_im, new_pool, v)


def swiglu(x, w1, w3, w2):
    return (jax.nn.silu(x @ w1) * (x @ w3)) @ w2


def moe_swiglu(x, router_w, w1, w3, w2):
    bsz, T, D = x.shape
    xt = x.reshape(bsz * T, D)
    logits = (xt @ router_w).astype(jnp.float32)
    top_val, top_idx = lax.top_k(logits, TOP_K)
    gate = jax.nn.softmax(top_val, axis=-1)
    combine = jnp.einsum('tk,tke->te', gate, jax.nn.one_hot(top_idx, N_EXPERTS, dtype=jnp.float32))
    out = jnp.zeros((bsz * T, D), jnp.float32)
    for e in range(N_EXPERTS):
        out = out + combine[:, e:e + 1] * swiglu(xt, w1[e], w3[e], w2[e]).astype(jnp.float32)
    return out.astype(x.dtype).reshape(bsz, T, D)


def run_trunk(x, conv0, lru0, s5re0, s5im0, pool0, start_pos, mixer_params, ffn_params):
    (norm_mix_g, norm_ffn_g, ffn_w1, ffn_w3, ffn_w2, router_w, moe_w1, moe_w3, moe_w2, norm_final_g) = ffn_params
    h = x
    convs, lrus, res, ims, pools, vs = [], [], [], [], [], []
    for l in range(DEPTH):
        lp = tuple(p[l] for p in mixer_params)
        mix, (c, hl, sr, si, pb, v) = token_mixer(rmsnorm(h, norm_mix_g[l]), conv0[l], lru0[l], s5re0[l],
                                                  s5im0[l], pool0[l], start_pos, lp)
        h = h + mix
        xn = rmsnorm(h, norm_ffn_g[l])
        if l % 2 == 0:
            h = h + swiglu(xn, ffn_w1[l // 2], ffn_w3[l // 2], ffn_w2[l // 2])
        else:
            h = h + moe_swiglu(xn, router_w[l // 2], moe_w1[l // 2], moe_w3[l // 2], moe_w2[l // 2])
        convs.append(c)
        lrus.append(hl)
        res.append(sr)
        ims.append(si)
        pools.append(pb)
        vs.append(v)
    y = rmsnorm(h, norm_final_g)
    return y, jnp.stack(convs), jnp.stack(lrus), jnp.stack(res), jnp.stack(ims), jnp.stack(pools), jnp.stack(vs)


def setup_inputs(seed: int = 0) -> dict:
    key = jax.random.key(seed)
    ks = iter(jax.random.split(key, 48))

    def nrm(shape, scale):
        return jax.random.normal(next(ks), shape, jnp.float32) * scale

    def unif(shape, lo, hi):
        return jax.random.uniform(next(ks), shape, jnp.float32, lo, hi)

    x_prompt = nrm((BATCH, SEQ, D_MODEL), 1.0)
    x_sample = nrm((DEC_BATCH, DEC_SEQ, D_MODEL), 1.0)
    state_conv = nrm((DEPTH, DEC_BATCH, CONV_W - 1, D_B), 1.0)
    state_lru = nrm((DEPTH, DEC_BATCH, D_B), 0.5)
    state_s5_re = nrm((DEPTH, DEC_BATCH, N_GROUPS_C, P_STATE), 0.3)
    state_s5_im = nrm((DEPTH, DEC_BATCH, N_GROUPS_C, P_STATE), 0.3)
    state_pool = nrm((DEPTH, DEC_BATCH, POOL_BUF, D_D), 1.0)

    norm_mix_g = 1.0 + nrm((DEPTH, D_MODEL), 0.02)
    w_in = nrm((DEPTH, D_MODEL, IN_COLS), D_MODEL ** -0.5)
    b_in = nrm((DEPTH, IN_COLS), 0.02)
    gmlp_ln_g = 1.0 + nrm((DEPTH, D_A), 0.02)
    gmlp_ln_b = nrm((DEPTH, D_A), 0.02)
    gmlp_w_s = nrm((DEPTH, N_HEADS_A, CHUNK, CHUNK), 0.5 * CHUNK ** -0.5)
    gmlp_b_s = 1.0 + nrm((DEPTH, N_HEADS_A, CHUNK), 0.02)
    conv_w = nrm((DEPTH, CONV_W, D_B), CONV_W ** -0.5)
    conv_b = nrm((DEPTH, D_B), 0.02)
    lru_w_a = nrm((DEPTH, N_HEADS_B, HEAD_DIM_B, HEAD_DIM_B), HEAD_DIM_B ** -0.5)
    lru_b_a = nrm((DEPTH, D_B), 0.02)
    lru_w_x = nrm((DEPTH, N_HEADS_B, HEAD_DIM_B, HEAD_DIM_B), HEAD_DIM_B ** -0.5)
    lru_b_x = nrm((DEPTH, D_B), 0.02)
    a0 = unif((DEPTH, D_B), 0.9, 0.999)
    root = a0 ** (1.0 / LRU_C)
    lru_lam = jnp.log(root) - jnp.log1p(-root)
    s5_a_re = -0.5 + nrm((DEPTH, N_GROUPS_C, P_STATE), 0.01)
    s5_a_im = math.pi * jnp.arange(P_STATE, dtype=jnp.float32) + nrm((DEPTH, N_GROUPS_C, P_STATE), 0.01)
    s5_log_dt = unif((DEPTH, N_GROUPS_C), math.log(DT_MIN), math.log(DT_MAX))
    s5_b_re = nrm((DEPTH, N_GROUPS_C, P_STATE, GROUP_C), (2.0 * GROUP_C) ** -0.5)
    s5_b_im = nrm((DEPTH, N_GROUPS_C, P_STATE, GROUP_C), (2.0 * GROUP_C) ** -0.5)
    s5_c_re = nrm((DEPTH, N_GROUPS_C, GROUP_C, P_STATE), (2.0 * P_STATE) ** -0.5)
    s5_c_im = nrm((DEPTH, N_GROUPS_C, GROUP_C, P_STATE), (2.0 * P_STATE) ** -0.5)
    s5_d = nrm((DEPTH, D_C), 0.5)
    s5_w_glu = nrm((DEPTH, D_C, D_C), D_C ** -0.5)
    s5_b_glu = nrm((DEPTH, D_C), 0.02)
    pool_w = nrm((DEPTH, N_GROUPS_D, GROUP_D, GROUP_D), GROUP_D ** -0.5)
    pool_scale = 1.0 + nrm((DEPTH, D_D), 0.02)
    w_branch = nrm((DEPTH, N_BRANCH, D_BRANCH, D_MODEL), D_BRANCH ** -0.5)
    w_out = nrm((DEPTH, D_MODEL, D_MODEL), D_MODEL ** -0.5)
    norm_ffn_g = 1.0 + nrm((DEPTH, D_MODEL), 0.02)
    ffn_w1 = nrm((N_DENSE, D_MODEL, D_FF), D_MODEL ** -0.5)
    ffn_w3 = nrm((N_DENSE, D_MODEL, D_FF), D_MODEL ** -0.5)
    ffn_w2 = nrm((N_DENSE, D_FF, D_MODEL), D_FF ** -0.5)
    router_w = nrm((N_MOE, D_MODEL, N_EXPERTS), D_MODEL ** -0.5)
    moe_w1 = nrm((N_MOE, N_EXPERTS, D_MODEL, D_FF_EXPERT), D_MODEL ** -0.5)
    moe_w3 = nrm((N_MOE, N_EXPERTS, D_MODEL, D_FF_EXPERT), D_MODEL ** -0.5)
    moe_w2 = nrm((N_MOE, N_EXPERTS, D_FF_EXPERT, D_MODEL), D_FF_EXPERT ** -0.5)
    norm_final_g = 1.0 + nrm((D_MODEL,), 0.02)
    return {
        'x_prompt': x_prompt, 'x_sample': x_sample,
        'state_conv': state_conv, 'state_lru': state_lru,
        'state_s5_re': state_s5_re, 'state_s5_im': state_s5_im, 'state_pool': state_pool,
        'norm_mix_g': norm_mix_g, 'w_in': w_in, 'b_in': b_in,
        'gmlp_ln_g': gmlp_ln_g, 'gmlp_ln_b': gmlp_ln_b, 'gmlp_w_s': gmlp_w_s, 'gmlp_b_s': gmlp_b_s,
        'conv_w': conv_w, 'conv_b': conv_b,
        'lru_w_a': lru_w_a, 'lru_b_a': lru_b_a, 'lru_w_x': lru_w_x, 'lru_b_x': lru_b_x, 'lru_lam': lru_lam,
        's5_a_re': s5_a_re, 's5_a_im': s5_a_im, 's5_log_dt': s5_log_dt,
        's5_b_re': s5_b_re, 's5_b_im': s5_b_im, 's5_c_re': s5_c_re, 's5_c_im': s5_c_im,
        's5_d': s5_d, 's5_w_glu': s5_w_glu, 's5_b_glu': s5_b_glu,
        'pool_w': pool_w, 'pool_scale': pool_scale,
        'w_branch': w_branch, 'w_out': w_out, 'norm_ffn_g': norm_ffn_g,
        'ffn_w1': ffn_w1, 'ffn_w3': ffn_w3, 'ffn_w2': ffn_w2,
        'router_w': router_w, 'moe_w1': moe_w1, 'moe_w3': moe_w3, 'moe_w2': moe_w2,
        'norm_final_g': norm_final_g,
    }


def reference(x_prompt, x_sample, state_conv, state_lru, state_s5_re, state_s5_im, state_pool,
              norm_mix_g, w_in, b_in, gmlp_ln_g, gmlp_ln_b, gmlp_w_s, gmlp_b_s, conv_w, conv_b,
              lru_w_a, lru_b_a, lru_w_x, lru_b_x, lru_lam,
              s5_a_re, s5_a_im, s5_log_dt, s5_b_re, s5_b_im, s5_c_re, s5_c_im, s5_d, s5_w_glu, s5_b_glu,
              pool_w, pool_scale, w_branch, w_out, norm_ffn_g,
              ffn_w1, ffn_w3, ffn_w2, router_w, moe_w1, moe_w3, moe_w2, norm_final_g):
    mixer_params = (w_in, b_in, gmlp_ln_g, gmlp_ln_b, gmlp_w_s, gmlp_b_s, conv_w, conv_b,
                    lru_w_a, lru_b_a, lru_w_x, lru_b_x, lru_lam,
                    s5_a_re, s5_a_im, s5_log_dt, s5_b_re, s5_b_im, s5_c_re, s5_c_im, s5_d, s5_w_glu, s5_b_glu,
                    pool_w, pool_scale, w_branch, w_out)
    ffn_params = (norm_mix_g, norm_ffn_g, ffn_w1, ffn_w3, ffn_w2, router_w, moe_w1, moe_w3, moe_w2, norm_final_g)
    bsz = x_prompt.shape[0]
    conv0 = jnp.zeros((DEPTH, bsz, CONV_W - 1, D_B), x_prompt.dtype)
    lru0 = jnp.zeros((DEPTH, bsz, D_B), jnp.float32)
    s5re0 = jnp.zeros((DEPTH, bsz, N_GROUPS_C, P_STATE), jnp.float32)
    s5im0 = jnp.zeros((DEPTH, bsz, N_GROUPS_C, P_STATE), jnp.float32)
    pool0 = jnp.zeros((DEPTH, bsz, POOL_BUF, D_D), x_prompt.dtype)
    y_prompt, conv_p, lru_p, s5re_p, s5im_p, pool_p, _ = run_trunk(
        x_prompt, conv0, lru0, s5re0, s5im0, pool0, 0, mixer_params, ffn_params)
    y_sample, conv_s, lru_s, s5re_s, s5im_s, pool_s, v_s = run_trunk(
        x_sample, state_conv, state_lru, state_s5_re, state_s5_im, state_pool, PAST_LEN, mixer_params, ffn_params)
    return (y_prompt, y_sample, conv_p, lru_p, s5re_p, s5im_p, pool_p, conv_s, lru_s, s5re_s, s5im_s, pool_s, v_s)
```

```python
import functools
import math

import jax
import jax.numpy as jnp
from jax import lax
from jax.experimental import pallas as pl
from jax.experimental.pallas import tpu as pltpu

F32 = jnp.float32
BF16 = jnp.bfloat16
I32 = jnp.int32

SUBLANES = 8
LANES = 128

D_MODEL = 1024
D_BR = 256
N_BRANCH = 4
N_HEADS = 4
HEAD_DIM = D_BR // N_HEADS
CHUNK = 128
CONV_W = 4
LRU_C = 8.0
N_GROUPS_C = 16
GROUP_C = 16
P_STATE = 64
S5_W = N_GROUPS_C * P_STATE
POOL_WINDOWS = (2, 4, 8, 16)
POOL_BUF = max(POOL_WINDOWS) - 1
GROUP_D = D_BR // len(POOL_WINDOWS)
COLS_A = 5 * D_BR
IN_COLS = COLS_A + N_BRANCH * D_MODEL
N_EXPERTS = 8
EPS = 1e-6
PAST_LEN = 16384
SQRT_2_OVER_PI = math.sqrt(2.0 / math.pi)
ROW_TILE = D_MODEL // LANES


def _gelu(x):
    return x * (0.5 * (1.0 + jnp.tanh(SQRT_2_OVER_PI * (x + 0.044715 * (x * x * x)))))


def _sigmoid(x):
    return 1.0 / (1.0 + jnp.exp(-x))


def _rms(x, g):
    return x * lax.rsqrt(jnp.mean(x * x, axis=-1, keepdims=True) + EPS) * g


def _dot(a, b):
    return jnp.dot(a, b, preferred_element_type=F32)


def _const_spec(shape, single=False):
    nd = len(shape)
    if single:
        return pl.BlockSpec(shape, lambda *_: (0,) * nd, pipeline_mode=pl.Buffered(1))
    return pl.BlockSpec(shape, lambda *_: (0,) * nd)


def _mixer_body(cfg, *refs):
    B, Tt, start_pos, mm_gmlp, emit_v = cfg
    R = Tt * B
    (h_ref, conv0_ref, lru0_ref, sre0_ref, sim0_ref, pool0_ref,
     ng_ref, win_ref, bin_ref, lng_ref, lnb_ref, ws_ref, bsm_ref,
     cw_ref, cb_ref, wa_ref, ba_ref, wx_ref, bx_ref, lam_ref,
     lbr_ref, lbi_ref, bblk_ref, cre_ref, cim_ref, d_ref, wglu_ref, bglu_ref,
     pw_ref, ps_ref, wb_ref, wo_ref) = refs[:32]
    n_out = 7 if emit_v else 6
    outs = refs[32:32 + n_out]
    out_ref, convo_ref, lruo_ref, sreo_ref, simo_ref, poolo_ref = outs[:6]
    (xn_s, cext_s, pext_s, lruh_s, sre_s, sim_s, u_s, v_s, mix_s, xc_s,
     a_s, b_s, yc_s, yd_s, s5_s) = refs[32 + n_out:]

    i = pl.program_id(0)
    SB = min(R, 256)
    SB5 = max(B, min(R, 256))
    lane256 = lax.broadcasted_iota(I32, (1, D_BR), 1)

    @pl.when(i == 0)
    def _():
        cext_s[0:(CONV_W - 1) * B, :] = conv0_ref[...]
        pext_s[0:POOL_BUF * B, :] = pool0_ref[...]
        lruh_s[...] = lru0_ref[...]
        sre_s[...] = sre0_ref[...]
        sim_s[...] = sim0_ref[...]

    def blocks(n, size, fn):
        if n == 1:
            fn(0)
        else:
            def body(k, c):
                fn(pl.multiple_of(k * size, size))
                return c
            lax.fori_loop(0, n, body, 0)

    def stage_in(r0):
        rows = pl.ds(r0, SB)
        xn = _rms(h_ref[rows, :], ng_ref[...]).astype(BF16)
        xn_s[rows, :] = xn
        pa = _dot(xn, win_ref[:, 0:COLS_A]) + bin_ref[:, 0:COLS_A]
        u_s[rows, :] = _gelu(pa[:, 0:D_BR])
        gv = _gelu(pa[:, D_BR:2 * D_BR])
        mu = jnp.mean(gv, axis=-1, keepdims=True)
        var = jnp.mean(jnp.square(gv - mu), axis=-1, keepdims=True)
        v = (gv - mu) * lax.rsqrt(var + EPS) * lng_ref[...] + lnb_ref[...]
        v_s[0, rows, :] = v[:, 0:LANES]
        v_s[1, rows, :] = v[:, LANES:2 * LANES]
        cext_s[pl.ds(r0 + (CONV_W - 1) * B, SB), :] = pa[:, 2 * D_BR:3 * D_BR]
        xc_s[rows, :] = pa[:, 3 * D_BR:4 * D_BR]
        pext_s[pl.ds(r0 + POOL_BUF * B, SB), :] = pa[:, 4 * D_BR:5 * D_BR]

    blocks(R // SB, SB, stage_in)

    if mm_gmlp:
        tril = (lax.broadcasted_iota(I32, (CHUNK, CHUNK), 0)
                >= lax.broadcasted_iota(I32, (CHUNK, CHUNK), 1))
        wm = [jnp.where(tril, ws_ref[hd], 0.0).astype(BF16) for hd in range(N_HEADS)]
        head = lane256 // HEAD_DIM
        for b in range(B):
            vb = jnp.concatenate([v_s[0, pl.ds(b, Tt, stride=B), :],
                                  v_s[1, pl.ds(b, Tt, stride=B), :]], axis=1).astype(BF16)
            mixed = bsm_ref[...]
            for hd in range(N_HEADS):
                mixed = mixed + jnp.where(head == hd, _dot(wm[hd], vb), 0.0)
            mix_s[0, pl.ds(b, Tt, stride=B), :] = mixed[:, 0:LANES]
            mix_s[1, pl.ds(b, Tt, stride=B), :] = mixed[:, LANES:2 * LANES]
    else:
        for t in range(Tt):
            for half in range(2):
                lo = half * LANES
                acc = jnp.broadcast_to(bsm_ref[t:t + 1, lo:lo + LANES], (B, LANES))
                for s in range(t + 1):
                    w = ws_ref[t * Tt + s:t * Tt + s + 1, lo:lo + LANES]
                    acc = acc + w * v_s[half, s * B:(s + 1) * B, :]
                mix_s[half, t * B:(t + 1) * B, :] = acc
    if emit_v:
        vo_ref = outs[6]
        vo_ref[:, 0:LANES] = v_s[0]
        vo_ref[:, LANES:2 * LANES] = v_s[1]

    log_sig_lam = (jnp.minimum(lam_ref[...], 0.0)
                   - jnp.log1p(jnp.exp(-jnp.abs(lam_ref[...]))))

    def stage_lru(r0):
        rows = pl.ds(r0, SB)
        conv = cb_ref[...] + cext_s[pl.ds(r0, SB), :] * cw_ref[0:1, :]
        for k in range(1, CONV_W):
            conv = conv + cext_s[pl.ds(r0 + k * B, SB), :] * cw_ref[k:k + 1, :]
        cbf = conv.astype(BF16)
        r = _sigmoid(_dot(cbf, wa_ref[...]) + ba_ref[...])
        ig = _sigmoid(_dot(cbf, wx_ref[...]) + bx_ref[...])
        a = jnp.exp(LRU_C * r * log_sig_lam)
        mult = jnp.sqrt(1.0 - a * a)
        if start_pos == 0:
            row = lax.broadcasted_iota(I32, (SB, D_BR), 0) + r0
            mult = jnp.where(jnp.logical_and(i == 0, row < B), 1.0, mult)
        a_s[rows, :] = a
        b_s[rows, :] = mult * ig * conv

    blocks(R // SB, SB, stage_lru)

    def lru_step(t, hprev):
        rows = pl.ds(t * B, B) if isinstance(t, int) else pl.ds(pl.multiple_of(t * B, B), B)
        hnew = a_s[rows, :] * hprev + b_s[rows, :]
        b_s[rows, :] = hnew
        return hnew

    if Tt <= 8:
        hl = lruh_s[...]
        for t in range(Tt):
            hl = lru_step(t, hl)
    else:
        hl = lax.fori_loop(0, Tt, lru_step, lruh_s[...], unroll=8)
    lruh_s[...] = hl
    lruo_ref[...] = hl
    conv_tail = cext_s[R:R + (CONV_W - 1) * B, :]
    convo_ref[...] = conv_tail
    cext_s[0:(CONV_W - 1) * B, :] = conv_tail

    lbr = jnp.broadcast_to(lbr_ref[...], (B, S5_W))
    lbi = jnp.broadcast_to(lbi_ref[...], (B, S5_W))
    steps5 = SB5 // B

    def stage_s5(r0):
        rows = pl.ds(r0, SB5)
        xc = xc_s[rows, :]
        s5_s[...] = _dot(xc.astype(BF16), bblk_ref[...])

        def s5_step(t, carry):
            sre, sim = carry
            rr = pl.ds(t * B, B) if isinstance(t, int) else pl.ds(pl.multiple_of(t * B, B), B)
            nre = lbr * sre - lbi * sim + s5_s[rr, 0:S5_W]
            nim = lbr * sim + lbi * sre + s5_s[rr, S5_W:2 * S5_W]
            s5_s[rr, 0:S5_W] = nre
            s5_s[rr, S5_W:2 * S5_W] = nim
            return nre, nim

        carry = (sre_s[...], sim_s[...])
        if steps5 <= 8:
            for t in range(steps5):
                carry = s5_step(t, carry)
        else:
            carry = lax.fori_loop(0, steps5, s5_step, carry, unroll=4)
        sre_s[...] = carry[0]
        sim_s[...] = carry[1]
        y = (_dot(s5_s[:, 0:S5_W].astype(BF16), cre_ref[...])
             - _dot(s5_s[:, S5_W:2 * S5_W].astype(BF16), cim_ref[...])
             + d_ref[...] * xc)
        y = _gelu(y)
        y = y * _sigmoid(_dot(y.astype(BF16), wglu_ref[...]) + bglu_ref[...])
        yc_s[rows, :] = y.astype(BF16)

    blocks(R // SB5, SB5, stage_s5)
    sreo_ref[...] = sre_s[...]
    simo_ref[...] = sim_s[...]

    wlane = jnp.where(lane256 < GROUP_D, POOL_WINDOWS[0],
                      jnp.where(lane256 < 2 * GROUP_D, POOL_WINDOWS[1],
                                jnp.where(lane256 < 3 * GROUP_D, POOL_WINDOWS[2], POOL_WINDOWS[3])))

    def stage_pool(r0):
        base = r0 + POOL_BUF * B
        tok = pext_s[pl.ds(base, SB), :]
        acc = tok
        sums = {}
        for j in range(1, max(POOL_WINDOWS)):
            acc = acc + pext_s[pl.ds(base - j * B, SB), :]
            if j + 1 in POOL_WINDOWS:
                sums[j + 1] = acc
        sel = jnp.where(lane256 < GROUP_D, sums[2],
                        jnp.where(lane256 < 2 * GROUP_D, sums[4],
                                  jnp.where(lane256 < 3 * GROUP_D, sums[8], sums[16])))
        if start_pos >= POOL_BUF:
            cnt = wlane.astype(F32)
        else:
            row = lax.broadcasted_iota(I32, (SB, D_BR), 0) + r0
            tpos = start_pos + i * Tt + lax.shift_right_logical(row, jnp.full_like(row, int(math.log2(B))))
            cnt = jnp.minimum(wlane, tpos + 1).astype(F32)
        diff = sel / cnt - tok
        yd_s[pl.ds(r0, SB), :] = (_dot(diff.astype(BF16), pw_ref[...]) * ps_ref[...]).astype(BF16)

    blocks(R // SB, SB, stage_pool)
    pool_tail = pext_s[R:R + POOL_BUF * B, :]
    poolo_ref[...] = pool_tail
    pext_s[0:POOL_BUF * B, :] = pool_tail

    SBM = min(R, 128)

    def stage_merge(r0):
        rows = pl.ds(r0, SBM)
        xn = xn_s[rows, :]
        ya = (u_s[rows, :] * jnp.concatenate([mix_s[0, rows, :], mix_s[1, rows, :]], axis=1)).astype(BF16)
        ys = (ya, b_s[rows, :].astype(BF16), yc_s[rows, :], yd_s[rows, :])
        merged = None
        for n in range(N_BRANCH):
            c0 = COLS_A + n * D_MODEL
            gate = _sigmoid(_dot(xn, win_ref[:, c0:c0 + D_MODEL]) + bin_ref[:, c0:c0 + D_MODEL])
            term = gate * _dot(ys[n], wb_ref[n * D_BR:(n + 1) * D_BR, :])
            merged = term if merged is None else merged + term
        out_ref[rows, :] = h_ref[rows, :] + _dot(merged.astype(BF16), wo_ref[...])

    blocks(R // SBM, SBM, stage_merge)


def _mixer_call(h_all, blk0, nsteps, B, Tt, start_pos, mm_gmlp, emit_v, states, lp):
    R = Tt * B
    cfg = (B, Tt, start_pos, mm_gmlp, emit_v)
    conv0, lru0, sre0, sim0, pool0 = states
    small = [conv0, lru0, sre0, sim0, pool0]
    params = list(lp)
    big = {1, 16, 24, 25}
    in_specs = [pl.BlockSpec((R, D_MODEL), lambda i: (blk0 + i, 0))]
    in_specs += [_const_spec(a.shape) for a in small]
    in_specs += [_const_spec(a.shape, single=(k in big)) for k, a in enumerate(params)]
    out_shape = [jax.ShapeDtypeStruct((nsteps * R, D_MODEL), F32),
                 jax.ShapeDtypeStruct(conv0.shape, F32), jax.ShapeDtypeStruct(lru0.shape, F32),
                 jax.ShapeDtypeStruct(sre0.shape, F32), jax.ShapeDtypeStruct(sim0.shape, F32),
                 jax.ShapeDtypeStruct(pool0.shape, F32)]
    out_specs = [pl.BlockSpec((R, D_MODEL), lambda i: (i, 0))] + [_const_spec(s.shape) for s in out_shape[1:]]
    if emit_v:
        out_shape.append(jax.ShapeDtypeStruct((R, D_BR), F32))
        out_specs.append(_const_spec((R, D_BR)))
    SB5 = max(B, min(R, 256))
    scratch = [
        pltpu.VMEM((R, D_MODEL), BF16),
        pltpu.VMEM((R + (CONV_W - 1) * B, D_BR), F32),
        pltpu.VMEM((R + POOL_BUF * B, D_BR), F32),
        pltpu.VMEM((B, D_BR), F32),
        pltpu.VMEM((B, S5_W), F32), pltpu.VMEM((B, S5_W), F32),
        pltpu.VMEM((R, D_BR), F32),
        pltpu.VMEM((2, R, LANES), F32),
        pltpu.VMEM((2, R, LANES), F32),
        pltpu.VMEM((R, D_BR), F32),
        pltpu.VMEM((R, D_BR), F32), pltpu.VMEM((R, D_BR), F32),
        pltpu.VMEM((R, D_BR), BF16), pltpu.VMEM((R, D_BR), BF16),
        pltpu.VMEM((SB5, 2 * S5_W), F32),
    ]
    return pl.pallas_call(
        functools.partial(_mixer_body, cfg),
        grid=(nsteps,),
        in_specs=in_specs,
        out_specs=out_specs,
        out_shape=out_shape,
        scratch_shapes=scratch,
        compiler_params=pltpu.CompilerParams(dimension_semantics=("arbitrary",),
                                             vmem_limit_bytes=60 * 1024 * 1024),
        name="mixer_b%d" % B,
    )(h_all, *small, *params)


def _s5prep_body(are_ref, aim_ref, ldt_ref, bre_ref, bim_ref, lbr_ref, lbi_ref, bbr_ref, bbi_ref):
    a_re, a_im = are_ref[...], aim_ref[...]
    dt = jnp.exp(ldt_ref[...])
    mag = jnp.exp(a_re * dt)
    lb_re = mag * jnp.cos(a_im * dt)
    lb_im = mag * jnp.sin(a_im * dt)
    den = a_re * a_re + a_im * a_im
    n_re = lb_re - 1.0
    q_re = (n_re * a_re + lb_im * a_im) / den
    q_im = (lb_im * a_re - n_re * a_im) / den
    lbr_ref[...] = lb_re
    lbi_ref[...] = lb_im
    bbr_ref[...] = q_re * bre_ref[...] - q_im * bim_ref[...]
    bbi_ref[...] = q_re * bim_ref[...] + q_im * bre_ref[...]


def _s5prep(a_re, a_im, log_dt, b_re, b_im):
    depth = a_re.shape[0]
    rows = depth * N_GROUPS_C * GROUP_C

    def rep(x):
        return jnp.broadcast_to(x[:, :, None, :], (depth, N_GROUPS_C, GROUP_C, P_STATE)).reshape(rows, P_STATE)

    ldt = jnp.broadcast_to(log_dt[:, :, None, None], (depth, N_GROUPS_C, GROUP_C, P_STATE)).reshape(rows, P_STATE)
    b_re_t = jnp.transpose(b_re, (0, 1, 3, 2)).reshape(rows, P_STATE)
    b_im_t = jnp.transpose(b_im, (0, 1, 3, 2)).reshape(rows, P_STATE)
    shp = jax.ShapeDtypeStruct((rows, P_STATE), F32)
    lbr, lbi, bbr, bbi = pl.pallas_call(_s5prep_body, out_shape=[shp] * 4, name="s5prep")(
        rep(a_re), rep(a_im), ldt, b_re_t, b_im_t)
    r4 = lambda x: x.reshape(depth, N_GROUPS_C, GROUP_C, P_STATE)
    return r4(lbr)[:, :, 0, :], r4(lbi)[:, :, 0, :], r4(bbr), r4(bbi)


def _ffn_body(x_ref, g_ref, w1_ref, w3_ref, w2_ref, o_ref, xn_s, acc_s):
    f = pl.program_id(1)

    @pl.when(f == 0)
    def _():
        xn_s[...] = _rms(x_ref[...], g_ref[...]).astype(BF16)
        acc_s[...] = jnp.zeros_like(acc_s)

    xn = xn_s[...]
    a = _dot(xn, w1_ref[...])
    b = _dot(xn, w3_ref[...])
    acc_s[...] += _dot((a * _sigmoid(a) * b).astype(BF16), w2_ref[...])

    @pl.when(f == pl.num_programs(1) - 1)
    def _():
        o_ref[...] = x_ref[...] + acc_s[...]


def _ffn_call(h, g, w1, w3, w2, tm, tf):
    n = h.shape[0]
    dff = w1.shape[1]
    return pl.pallas_call(
        _ffn_body,
        grid=(n // tm, dff // tf),
        in_specs=[pl.BlockSpec((tm, D_MODEL), lambda i, f: (i, 0)),
                  pl.BlockSpec((1, D_MODEL), lambda i, f: (0, 0)),
                  pl.BlockSpec((D_MODEL, tf), lambda i, f: (0, f)),
                  pl.BlockSpec((D_MODEL, tf), lambda i, f: (0, f)),
                  pl.BlockSpec((tf, D_MODEL), lambda i, f: (f, 0))],
        out_specs=pl.BlockSpec((tm, D_MODEL), lambda i, f: (i, 0)),
        out_shape=jax.ShapeDtypeStruct((n, D_MODEL), F32),
        scratch_shapes=[pltpu.VMEM((tm, D_MODEL), BF16), pltpu.VMEM((tm, D_MODEL), F32)],
        compiler_params=pltpu.CompilerParams(dimension_semantics=("arbitrary", "arbitrary"),
                                             vmem_limit_bytes=56 * 1024 * 1024),
        name="ffn",
    )(h, g, w1, w3, w2)


def _router_body(x_ref, g_ref, rw_ref, idx_ref, gate_ref):
    xn = _rms(x_ref[...], g_ref[...])
    logits = jnp.dot(xn, rw_ref[...], preferred_element_type=F32, precision=lax.Precision.HIGHEST)
    lane = lax.broadcasted_iota(I32, logits.shape, 1)
    neg = jnp.float32(-jnp.inf)
    logits = jnp.where(lane < N_EXPERTS, logits, neg)
    m1 = jnp.max(logits, axis=-1, keepdims=True)
    i1 = jnp.min(jnp.where(logits == m1, lane, LANES), axis=-1, keepdims=True)
    rest = jnp.where(lane == i1, neg, logits)
    m2 = jnp.max(rest, axis=-1, keepdims=True)
    i2 = jnp.min(jnp.where(rest == m2, lane, LANES), axis=-1, keepdims=True)
    e = jnp.exp(m2 - m1)
    g1 = 1.0 / (1.0 + e)
    g2 = e / (1.0 + e)
    idx_ref[...] = jnp.where(lane == 0, i1, jnp.where(lane == 1, i2, 0))
    gate_ref[...] = jnp.where(lane == 0, g1, jnp.where(lane == 1, g2, 0.0))


def _router_call(h, g, rw_pad, tm):
    n = h.shape[0]
    return pl.pallas_call(
        _router_body,
        grid=(n // tm,),
        in_specs=[pl.BlockSpec((tm, D_MODEL), lambda i: (i, 0)),
                  _const_spec((1, D_MODEL)), _const_spec((D_MODEL, LANES))],
        out_specs=[pl.BlockSpec((tm, LANES), lambda i: (i, 0))] * 2,
        out_shape=[jax.ShapeDtypeStruct((n, LANES), I32), jax.ShapeDtypeStruct((n, LANES), F32)],
        compiler_params=pltpu.CompilerParams(dimension_semantics=("arbitrary",)),
        name="router",
    )(h, g, rw_pad)


def _row_copy_wait(hbm_ref, sem, nrows):
    pltpu.make_async_copy(hbm_ref.at[pl.ds(0, nrows)], hbm_ref.at[pl.ds(0, nrows)], sem).wait()


def _dispatch_body(tm, pos_ref, x_ref, g_ref, xs_in_ref, xs_ref, buf, sem):
    del xs_in_ref
    xn = _rms(x_ref[...], g_ref[...])
    for k in range(ROW_TILE):
        buf[pl.ds(k, tm, stride=ROW_TILE), :] = xn[:, k * LANES:(k + 1) * LANES]

    def issue(r, c):
        src = buf.at[pl.ds(pl.multiple_of(r * ROW_TILE, ROW_TILE), ROW_TILE), :]
        pltpu.make_async_copy(src, xs_ref.at[pos_ref[0, 0, 2 * r]], sem.at[0]).start()
        pltpu.make_async_copy(src, xs_ref.at[pos_ref[0, 0, 2 * r + 1]], sem.at[0]).start()
        return c

    lax.fori_loop(0, tm, issue, 0)
    _row_copy_wait(xs_ref, sem.at[0], 2 * tm)


def _dispatch_call(h, g, pos, xs_zero, tm):
    n = h.shape[0]
    nt = n // tm
    return pl.pallas_call(
        functools.partial(_dispatch_body, tm),
        grid=(nt,),
        in_specs=[pl.BlockSpec((1, 1, 2 * tm), lambda i: (i, 0, 0), memory_space=pltpu.SMEM),
                  pl.BlockSpec((tm, D_MODEL), lambda i: (i, 0)),
                  _const_spec((1, D_MODEL)),
                  pl.BlockSpec(memory_space=pl.ANY)],
        out_specs=pl.BlockSpec(memory_space=pl.ANY),
        out_shape=jax.ShapeDtypeStruct(xs_zero.shape, F32),
        scratch_shapes=[pltpu.VMEM((tm * ROW_TILE, LANES), F32), pltpu.SemaphoreType.DMA((1,))],
        input_output_aliases={3: 0},
        compiler_params=pltpu.CompilerParams(dimension_semantics=("arbitrary",)),
        name="dispatch",
    )(pos.reshape(nt, 1, 2 * tm), h, g, xs_zero)


def _experts_body(tm, te_ref, tv_ref, xs_ref, w1_ref, w3_ref, w2_ref, y_ref, x_s, acc_s):
    t = pl.program_id(0)
    f = pl.program_id(1)
    valid = tv_ref[t] > 0

    @pl.when(jnp.logical_and(valid, f == 0))
    def _():
        for k in range(ROW_TILE):
            x_s[:, k * LANES:(k + 1) * LANES] = xs_ref[pl.ds(k, tm, stride=ROW_TILE), :].astype(BF16)
        acc_s[...] = jnp.zeros_like(acc_s)

    @pl.when(valid)
    def _():
        x = x_s[...]
        a = _dot(x, w1_ref[0])
        b = _dot(x, w3_ref[0])
        acc_s[...] += _dot((a * _sigmoid(a) * b).astype(BF16), w2_ref[0])

    last = f == pl.num_programs(1) - 1

    @pl.when(jnp.logical_and(valid, last))
    def _():
        acc = acc_s[...]
        for k in range(ROW_TILE):
            y_ref[pl.ds(k, tm, stride=ROW_TILE), :] = acc[:, k * LANES:(k + 1) * LANES]

    @pl.when(jnp.logical_and(jnp.logical_not(valid), last))
    def _():
        y_ref[...] = jnp.zeros_like(y_ref)


def _experts_call(xs2d, tile_expert, tile_valid, w1, w3, w2, tm, tf):
    rows = xs2d.shape[0] // ROW_TILE
    nt = rows // tm
    dff = w1.shape[2]
    grid_spec = pltpu.PrefetchScalarGridSpec(
        num_scalar_prefetch=2,
        grid=(nt, dff // tf),
        in_specs=[pl.BlockSpec((tm * ROW_TILE, LANES), lambda t, f, te, tv: (t, 0)),
                  pl.BlockSpec((1, D_MODEL, tf), lambda t, f, te, tv: (te[t], 0, f)),
                  pl.BlockSpec((1, D_MODEL, tf), lambda t, f, te, tv: (te[t], 0, f)),
                  pl.BlockSpec((1, tf, D_MODEL), lambda t, f, te, tv: (te[t], f, 0))],
        out_specs=pl.BlockSpec((tm * ROW_TILE, LANES), lambda t, f, te, tv: (t, 0)),
        scratch_shapes=[pltpu.VMEM((tm, D_MODEL), BF16), pltpu.VMEM((tm, D_MODEL), F32)])
    return pl.pallas_call(
        functools.partial(_experts_body, tm),
        grid_spec=grid_spec,
        out_shape=jax.ShapeDtypeStruct(xs2d.shape, F32),
        compiler_params=pltpu.CompilerParams(dimension_semantics=("arbitrary", "arbitrary"),
                                             vmem_limit_bytes=56 * 1024 * 1024),
        name="experts",
    )(tile_expert, tile_valid, xs2d, w1, w3, w2)


def _combine_body(tm, final, pos_ref, h_ref, gate_ref, *rest):
    if final:
        ng_ref, y_ref, o_ref, buf, sem = rest
    else:
        y_ref, o_ref, buf, sem = rest

    def issue(r, c):
        for j in range(2):
            dst = buf.at[j, pl.ds(pl.multiple_of(r * ROW_TILE, ROW_TILE), ROW_TILE), :]
            pltpu.make_async_copy(y_ref.at[pos_ref[0, 0, 2 * r + j]], dst, sem.at[0]).start()
        return c

    lax.fori_loop(0, tm, issue, 0)
    _row_copy_wait(y_ref, sem.at[0], 2 * tm)
    g0 = gate_ref[:, 0:1]
    g1 = gate_ref[:, 1:2]
    for k in range(ROW_TILE):
        cols = slice(k * LANES, (k + 1) * LANES)
        moe = (g0 * buf[0, pl.ds(k, tm, stride=ROW_TILE), :]
               + g1 * buf[1, pl.ds(k, tm, stride=ROW_TILE), :])
        o_ref[:, cols] = h_ref[:, cols] + moe
    if final:
        o_ref[...] = _rms(o_ref[...], ng_ref[...])


def _combine_call(h, gates, pos, y3, tm, final_g=None):
    n = h.shape[0]
    nt = n // tm
    final = final_g is not None
    in_specs = [pl.BlockSpec((1, 1, 2 * tm), lambda i: (i, 0, 0), memory_space=pltpu.SMEM),
                pl.BlockSpec((tm, D_MODEL), lambda i: (i, 0)),
                pl.BlockSpec((tm, LANES), lambda i: (i, 0))]
    args = [pos.reshape(nt, 1, 2 * tm), h, gates]
    if final:
        in_specs.append(_const_spec((1, D_MODEL)))
        args.append(final_g)
    in_specs.append(pl.BlockSpec(memory_space=pl.ANY))
    args.append(y3)
    return pl.pallas_call(
        functools.partial(_combine_body, tm, final),
        grid=(nt,),
        in_specs=in_specs,
        out_specs=pl.BlockSpec((tm, D_MODEL), lambda i: (i, 0)),
        out_shape=jax.ShapeDtypeStruct((n, D_MODEL), F32),
        scratch_shapes=[pltpu.VMEM((2, tm * ROW_TILE, LANES), F32), pltpu.SemaphoreType.DMA((1,))],
        compiler_params=pltpu.CompilerParams(dimension_semantics=("arbitrary",)),
        name="combine",
    )(*args)


def _rms_body(x_ref, g_ref, o_ref):
    o_ref[...] = _rms(x_ref[...], g_ref[...])


def _rms_call(h, g, tm):
    n = h.shape[0]
    return pl.pallas_call(
        _rms_body, grid=(n // tm,),
        in_specs=[pl.BlockSpec((tm, D_MODEL), lambda i: (i, 0)), _const_spec((1, D_MODEL))],
        out_specs=pl.BlockSpec((tm, D_MODEL), lambda i: (i, 0)),
        out_shape=jax.ShapeDtypeStruct((n, D_MODEL), F32),
        name="rms",
    )(h, g)


MOE_TM = 768
MOE_TF = 896
TOK_TM = 768
FFN_TF = 1408


def _moe_layer(h, norm_g, rw_pad, w1, w3, w2, final_g):
    n = h.shape[0]
    idx, gates = _router_call(h, norm_g, rw_pad, TOK_TM)
    e_flat = idx[:, 0:2].reshape(2 * n)
    onehot = (e_flat[:, None] == jnp.arange(N_EXPERTS, dtype=I32)[None, :]).astype(I32)
    csum = jnp.cumsum(onehot, axis=0)
    rank = jnp.sum((csum - onehot) * onehot, axis=1)
    counts = csum[-1]
    padded = ((counts + MOE_TM - 1) // MOE_TM) * MOE_TM
    ends = jnp.cumsum(padded)
    starts = ends - padded
    pos = (jnp.sum(onehot * starts[None, :], axis=1) + rank).astype(I32)
    n_tiles = (2 * n) // MOE_TM + N_EXPERTS
    tile_start = jnp.arange(n_tiles, dtype=I32) * MOE_TM
    tile_valid = (tile_start < ends[-1]).astype(I32)
    tile_expert = jnp.minimum(jnp.sum((tile_start[:, None] >= ends[None, :]).astype(I32), axis=1),
                              N_EXPERTS - 1).astype(I32)
    last_valid = jnp.max(jnp.where(tile_valid > 0, tile_expert, 0))
    tile_expert = jnp.where(tile_valid > 0, tile_expert, last_valid)
    xs_zero = jnp.zeros((n_tiles * MOE_TM, ROW_TILE, LANES), F32)
    xs = _dispatch_call(h, norm_g, pos, xs_zero, TOK_TM)
    y2d = _experts_call(xs.reshape(n_tiles * MOE_TM * ROW_TILE, LANES), tile_expert, tile_valid,
                        w1, w3, w2, MOE_TM, MOE_TF)
    return _combine_call(h, gates, pos, y2d.reshape(n_tiles * MOE_TM, ROW_TILE, LANES), TOK_TM, final_g)


def _block_diag(w):
    k, a, b = w.shape
    eye = jnp.eye(k, dtype=w.dtype)
    return jnp.einsum('kab,kj->kajb', w, eye).reshape(k * a, k * b)


def kernel(x_prompt, x_sample, state_conv, state_lru, state_s5_re, state_s5_im, state_pool, norm_mix_g, w_in, b_in, gmlp_ln_g, gmlp_ln_b, gmlp_w_s, gmlp_b_s, conv_w, conv_b, lru_w_a, lru_b_a, lru_w_x, lru_b_x, lru_lam, s5_a_re, s5_a_im, s5_log_dt, s5_b_re, s5_b_im, s5_c_re, s5_c_im, s5_d, s5_w_glu, s5_b_glu, pool_w, pool_scale, w_branch, w_out, norm_ffn_g, ffn_w1, ffn_w3, ffn_w2, router_w, moe_w1, moe_w3, moe_w2, norm_final_g):
    depth = w_in.shape[0]
    bp, tp, _ = x_prompt.shape
    bs, ts, _ = x_sample.shape
    n_p, n_s = bp * tp, bs * ts
    assert n_p % (CHUNK * bp) == 0 and n_p % n_s == 0

    h = jnp.concatenate([jnp.transpose(x_prompt, (1, 0, 2)).reshape(n_p, D_MODEL),
                         jnp.transpose(x_sample, (1, 0, 2)).reshape(n_s, D_MODEL)], axis=0)

    lbr, lbi, bbr, bbi = _s5prep(s5_a_re, s5_a_im, s5_log_dt, s5_b_re, s5_b_im)
    row = lambda x: x.reshape(1, -1)
    lane_head = jnp.arange(D_BR) // HEAD_DIM

    conv_p, lru_p, sre_p, sim_p, pool_p = [], [], [], [], []
    conv_s, lru_s, sre_s, sim_s, pool_s, v_s = [], [], [], [], [], []
    zeros_p = (jnp.zeros(((CONV_W - 1) * bp, D_BR), F32), jnp.zeros((bp, D_BR), F32),
               jnp.zeros((bp, S5_W), F32), jnp.zeros((bp, S5_W), F32),
               jnp.zeros((POOL_BUF * bp, D_BR), F32))
    for l in range(depth):
        bblk = jnp.concatenate([_block_diag(bbr[l]), _block_diag(bbi[l])], axis=1).astype(BF16)
        cre = _block_diag(jnp.transpose(s5_c_re[l], (0, 2, 1))).astype(BF16)
        cim = _block_diag(jnp.transpose(s5_c_im[l], (0, 2, 1))).astype(BF16)
        common_tail = (
            conv_w[l], row(conv_b[l]),
            _block_diag(lru_w_a[l]).astype(BF16), row(lru_b_a[l]),
            _block_diag(lru_w_x[l]).astype(BF16), row(lru_b_x[l]), row(lru_lam[l]),
            row(lbr[l]), row(lbi[l]), bblk, cre, cim, row(s5_d[l]),
            s5_w_glu[l].astype(BF16), row(s5_b_glu[l]),
            _block_diag(pool_w[l]).astype(BF16), row(pool_scale[l]),
            w_branch[l].reshape(N_BRANCH * D_BR, D_MODEL).astype(BF16), w_out[l].astype(BF16))
        common_head = (row(norm_mix_g[l]), w_in[l].astype(BF16), row(b_in[l]),
                       row(gmlp_ln_g[l]), row(gmlp_ln_b[l]))
        bsm_p = gmlp_b_s[l][lane_head, :].T
        lp_p = common_head + (gmlp_w_s[l], bsm_p) + common_tail
        ws_small = jnp.transpose(gmlp_w_s[l][:, :ts, :ts], (1, 2, 0))[:, :, lane_head].reshape(ts * ts, D_BR)
        bsm_s = gmlp_b_s[l][lane_head, :ts].T
        lp_s = common_head + (ws_small, bsm_s) + common_tail

        outs_p = _mixer_call(h, 0, tp // CHUNK, bp, CHUNK, 0, True, False, zeros_p, lp_p)
        st_s = (jnp.transpose(state_conv[l], (1, 0, 2)).reshape((CONV_W - 1) * bs, D_BR),
                state_lru[l], state_s5_re[l].reshape(bs, S5_W), state_s5_im[l].reshape(bs, S5_W),
                jnp.transpose(state_pool[l], (1, 0, 2)).reshape(POOL_BUF * bs, D_BR))
        outs_s = _mixer_call(h, n_p // n_s, 1, bs, ts, PAST_LEN, False, True, st_s, lp_s)
        h = jnp.concatenate([outs_p[0], outs_s[0]], axis=0)
        for lst, o in zip((conv_p, lru_p, sre_p, sim_p, pool_p), outs_p[1:6]):
            lst.append(o)
        for lst, o in zip((conv_s, lru_s, sre_s, sim_s, pool_s, v_s), outs_s[1:7]):
            lst.append(o)

        if l % 2 == 0:
            k = l // 2
            h = _ffn_call(h, row(norm_ffn_g[l]), ffn_w1[k].astype(BF16), ffn_w3[k].astype(BF16),
                          ffn_w2[k].astype(BF16), TOK_TM, FFN_TF)
            if l == depth - 1:
                h = _rms_call(h, row(norm_final_g), TOK_TM)
        else:
            k = l // 2
            rw_pad = jnp.pad(router_w[k], ((0, 0), (0, LANES - N_EXPERTS)))
            h = _moe_layer(h, row(norm_ffn_g[l]), rw_pad, moe_w1[k].astype(BF16), moe_w3[k].astype(BF16),
                           moe_w2[k].astype(BF16), row(norm_final_g) if l == depth - 1 else None)

    def tm_to_bm(x, t, b):
        return jnp.transpose(x.reshape(t, b, x.shape[-1]), (1, 0, 2))

    y_prompt = tm_to_bm(h[:n_p], tp, bp)
    y_sample = tm_to_bm(h[n_p:], ts, bs)
    stack = lambda lst, f: jnp.stack([f(o) for o in lst])
    return (
        y_prompt, y_sample,
        stack(conv_p, lambda o: tm_to_bm(o, CONV_W - 1, bp)),
        stack(lru_p, lambda o: o),
        stack(sre_p, lambda o: o.reshape(bp, N_GROUPS_C, P_STATE)),
        stack(sim_p, lambda o: o.reshape(bp, N_GROUPS_C, P_STATE)),
        stack(pool_p, lambda o: tm_to_bm(o, POOL_BUF, bp)),
        stack(conv_s, lambda o: tm_to_bm(o, CONV_W - 1, bs)),
        stack(lru_s, lambda o: o),
        stack(sre_s, lambda o: o.reshape(bs, N_GROUPS_C, P_STATE)),
        stack(sim_s, lambda o: o.reshape(bs, N_GROUPS_C, P_STATE)),
        stack(pool_s, lambda o: tm_to_bm(o, POOL_BUF, bs)),
        stack(v_s, lambda o: tm_to_bm(o, ts, bs)),
    )
```

```python
import functools
import math

import jax
import jax.numpy as jnp
from jax import lax
from jax.experimental import pallas as pl
from jax.experimental.pallas import tpu as pltpu

F32 = jnp.float32
BF16 = jnp.bfloat16
I32 = jnp.int32

SUBLANES = 8
LANES = 128

D_MODEL = 1024
D_BR = 256
N_BRANCH = 4
N_HEADS = 4
HEAD_DIM = D_BR // N_HEADS
CHUNK = 128
CONV_W = 4
LRU_C = 8.0
N_GROUPS_C = 16
GROUP_C = 16
P_STATE = 64
S5_W = N_GROUPS_C * P_STATE
POOL_WINDOWS = (2, 4, 8, 16)
POOL_BUF = max(POOL_WINDOWS) - 1
GROUP_D = D_BR // len(POOL_WINDOWS)
COLS_A = 5 * D_BR
IN_COLS = COLS_A + N_BRANCH * D_MODEL
N_EXPERTS = 8
EPS = 1e-6
PAST_LEN = 16384
SQRT_2_OVER_PI = math.sqrt(2.0 / math.pi)
ROW_TILE = D_MODEL // LANES


def _gelu(x):
    return x * (0.5 * (1.0 + jnp.tanh(SQRT_2_OVER_PI * (x + 0.044715 * (x * x * x)))))


def _sigmoid(x):
    return 1.0 / (1.0 + jnp.exp(-x))


def _rms(x, g):
    return x * lax.rsqrt(jnp.mean(x * x, axis=-1, keepdims=True) + EPS) * g


def _dot(a, b):
    return jnp.dot(a, b, preferred_element_type=F32)


def _const_spec(shape, single=False):
    nd = len(shape)
    if single:
        return pl.BlockSpec(shape, lambda *_: (0,) * nd, pipeline_mode=pl.Buffered(1))
    return pl.BlockSpec(shape, lambda *_: (0,) * nd)


def _mixer_body(cfg, *refs):
    B, Tt, start_pos, mm_gmlp, emit_v = cfg
    R = Tt * B
    (h_ref, conv0_ref, lru0_ref, sre0_ref, sim0_ref, pool0_ref,
     ng_ref, win_ref, bin_ref, lng_ref, lnb_ref, ws_ref, bsm_ref,
     cw_ref, cb_ref, wa_ref, ba_ref, wx_ref, bx_ref, lam_ref,
     lbr_ref, lbi_ref, bblk_ref, cre_ref, cim_ref, d_ref, wglu_ref, bglu_ref,
     pw_ref, ps_ref, wb_ref, wo_ref) = refs[:32]
    n_out = 7 if emit_v else 6
    outs = refs[32:32 + n_out]
    out_ref, convo_ref, lruo_ref, sreo_ref, simo_ref, poolo_ref = outs[:6]
    (xn_s, cext_s, pext_s, lruh_s, sre_s, sim_s, u_s, v_s, mix_s, xc_s,
     a_s, b_s, yc_s, yd_s, s5_s) = refs[32 + n_out:]

    i = pl.program_id(0)
    SB = min(R, 256)
    SB5 = max(B, min(R, 256))
    lane256 = lax.broadcasted_iota(I32, (1, D_BR), 1)

    @pl.when(i == 0)
    def _():
        cext_s[0:(CONV_W - 1) * B, :] = conv0_ref[...]
        pext_s[0:POOL_BUF * B, :] = pool0_ref[...]
        lruh_s[...] = lru0_ref[...]
        sre_s[...] = sre0_ref[...]
        sim_s[...] = sim0_ref[...]

    def blocks(n, size, fn):
        if n == 1:
            fn(0)
        else:
            def body(k, c):
                fn(pl.multiple_of(k * size, size))
                return c
            lax.fori_loop(0, n, body, 0)

    def stage_in(r0):
        rows = pl.ds(r0, SB)
        xn = _rms(h_ref[rows, :], ng_ref[...]).astype(BF16)
        xn_s[rows, :] = xn
        pa = _dot(xn, win_ref[:, 0:COLS_A]) + bin_ref[:, 0:COLS_A]
        u_s[rows, :] = _gelu(pa[:, 0:D_BR])
        gv = _gelu(pa[:, D_BR:2 * D_BR])
        mu = jnp.mean(gv, axis=-1, keepdims=True)
        var = jnp.mean(jnp.square(gv - mu), axis=-1, keepdims=True)
        v = (gv - mu) * lax.rsqrt(var + EPS) * lng_ref[...] + lnb_ref[...]
        v_s[0, rows, :] = v[:, 0:LANES]
        v_s[1, rows, :] = v[:, LANES:2 * LANES]
        cext_s[pl.ds(r0 + (CONV_W - 1) * B, SB), :] = pa[:, 2 * D_BR:3 * D_BR]
        xc_s[rows, :] = pa[:, 3 * D_BR:4 * D_BR]
        pext_s[pl.ds(r0 + POOL_BUF * B, SB), :] = pa[:, 4 * D_BR:5 * D_BR]

    def stage_gmlp():
        if mm_gmlp:
            tril = (lax.broadcasted_iota(I32, (CHUNK, CHUNK), 0)
                    >= lax.broadcasted_iota(I32, (CHUNK, CHUNK), 1))
            wm = [jnp.where(tril, ws_ref[hd], 0.0).astype(BF16) for hd in range(N_HEADS)]
            head = lane256 // HEAD_DIM
            for b in range(B):
                vb = jnp.concatenate([v_s[0, pl.ds(b, Tt, stride=B), :],
                                      v_s[1, pl.ds(b, Tt, stride=B), :]], axis=1).astype(BF16)
                mixed = bsm_ref[...]
                for hd in range(N_HEADS):
                    mixed = mixed + jnp.where(head == hd, _dot(wm[hd], vb), 0.0)
                mix_s[0, pl.ds(b, Tt, stride=B), :] = mixed[:, 0:LANES]
                mix_s[1, pl.ds(b, Tt, stride=B), :] = mixed[:, LANES:2 * LANES]
        else:
            for t in range(Tt):
                for half in range(2):
                    lo = half * LANES
                    acc = jnp.broadcast_to(bsm_ref[t:t + 1, lo:lo + LANES], (B, LANES))
                    for s in range(t + 1):
                        w = ws_ref[t * Tt + s:t * Tt + s + 1, lo:lo + LANES]
                        acc = acc + w * v_s[half, s * B:(s + 1) * B, :]
                    mix_s[half, t * B:(t + 1) * B, :] = acc
        if emit_v:
            vo_ref = outs[6]
            vo_ref[:, 0:LANES] = v_s[0]
            vo_ref[:, LANES:2 * LANES] = v_s[1]

    log_sig_lam = (jnp.minimum(lam_ref[...], 0.0)
                   - jnp.log1p(jnp.exp(-jnp.abs(lam_ref[...]))))

    def stage_lru(r0):
        rows = pl.ds(r0, SB)
        conv = cb_ref[...] + cext_s[pl.ds(r0, SB), :] * cw_ref[0:1, :]
        for k in range(1, CONV_W):
            conv = conv + cext_s[pl.ds(r0 + k * B, SB), :] * cw_ref[k:k + 1, :]
        cbf = conv.astype(BF16)
        r = _sigmoid(_dot(cbf, wa_ref[...]) + ba_ref[...])
        ig = _sigmoid(_dot(cbf, wx_ref[...]) + bx_ref[...])
        a = jnp.exp(LRU_C * r * log_sig_lam)
        mult = jnp.sqrt(1.0 - a * a)
        if start_pos == 0:
            row = lax.broadcasted_iota(I32, (SB, D_BR), 0) + r0
            mult = jnp.where(jnp.logical_and(i == 0, row < B), 1.0, mult)
        a_s[rows, :] = a
        b_s[rows, :] = mult * ig * conv

    def lru_step(t, hprev):
        rows = pl.ds(t * B, B) if isinstance(t, int) else pl.ds(pl.multiple_of(t * B, B), B)
        hnew = a_s[rows, :] * hprev + b_s[rows, :]
        b_s[rows, :] = hnew
        return hnew

    def lru_scan():
        if Tt <= 8:
            hl = lruh_s[...]
            for t in range(Tt):
                hl = lru_step(t, hl)
        else:
            hl = lax.fori_loop(0, Tt, lru_step, lruh_s[...], unroll=8)
        lruh_s[...] = hl
        lruo_ref[...] = hl
        conv_tail = cext_s[R:R + (CONV_W - 1) * B, :]
        convo_ref[...] = conv_tail
        cext_s[0:(CONV_W - 1) * B, :] = conv_tail

    lbr = jnp.broadcast_to(lbr_ref[...], (B, S5_W))
    lbi = jnp.broadcast_to(lbi_ref[...], (B, S5_W))
    steps5 = SB5 // B

    def stage_s5(r0):
        rows = pl.ds(r0, SB5)
        xc = xc_s[rows, :]
        s5_s[...] = _dot(xc.astype(BF16), bblk_ref[...])

        def s5_step(t, carry):
            sre, sim = carry
            rr = pl.ds(t * B, B) if isinstance(t, int) else pl.ds(pl.multiple_of(t * B, B), B)
            nre = lbr * sre - lbi * sim + s5_s[rr, 0:S5_W]
            nim = lbr * sim + lbi * sre + s5_s[rr, S5_W:2 * S5_W]
            s5_s[rr, 0:S5_W] = nre
            s5_s[rr, S5_W:2 * S5_W] = nim
            return nre, nim

        carry = (sre_s[...], sim_s[...])
        for t in range(steps5):
            carry = s5_step(t, carry)
        sre_s[...] = carry[0]
        sim_s[...] = carry[1]
        y = (_dot(s5_s[:, 0:S5_W].astype(BF16), cre_ref[...])
             - _dot(s5_s[:, S5_W:2 * S5_W].astype(BF16), cim_ref[...])
             + d_ref[...] * xc)
        y = _gelu(y)
        y = y * _sigmoid(_dot(y.astype(BF16), wglu_ref[...]) + bglu_ref[...])
        yc_s[rows, :] = y.astype(BF16)

    wlane = jnp.where(lane256 < GROUP_D, POOL_WINDOWS[0],
                      jnp.where(lane256 < 2 * GROUP_D, POOL_WINDOWS[1],
                                jnp.where(lane256 < 3 * GROUP_D, POOL_WINDOWS[2], POOL_WINDOWS[3])))

    def stage_pool(r0):
        base = r0 + POOL_BUF * B
        tok = pext_s[pl.ds(base, SB), :]
        acc = tok
        sums = {}
        for j in range(1, max(POOL_WINDOWS)):
            acc = acc + pext_s[pl.ds(base - j * B, SB), :]
            if j + 1 in POOL_WINDOWS:
                sums[j + 1] = acc
        sel = jnp.where(lane256 < GROUP_D, sums[2],
                        jnp.where(lane256 < 2 * GROUP_D, sums[4],
                                  jnp.where(lane256 < 3 * GROUP_D, sums[8], sums[16])))
        if start_pos >= POOL_BUF:
            cnt = wlane.astype(F32)
        else:
            row = lax.broadcasted_iota(I32, (SB, D_BR), 0) + r0
            tpos = start_pos + i * Tt + lax.shift_right_logical(row, jnp.full_like(row, int(math.log2(B))))
            cnt = jnp.minimum(wlane, tpos + 1).astype(F32)
        diff = sel / cnt - tok
        yd_s[pl.ds(r0, SB), :] = (_dot(diff.astype(BF16), pw_ref[...]) * ps_ref[...]).astype(BF16)

    def pool_tail():
        tail = pext_s[R:R + POOL_BUF * B, :]
        poolo_ref[...] = tail
        pext_s[0:POOL_BUF * B, :] = tail

    SBM = min(R, 128)

    def stage_merge(r0):
        rows = pl.ds(r0, SBM)
        xn = xn_s[rows, :]
        ya = (u_s[rows, :] * jnp.concatenate([mix_s[0, rows, :], mix_s[1, rows, :]], axis=1)).astype(BF16)
        ys = (ya, b_s[rows, :].astype(BF16), yc_s[rows, :], yd_s[rows, :])
        merged = None
        for n in range(N_BRANCH):
            c0 = COLS_A + n * D_MODEL
            gate = _sigmoid(_dot(xn, win_ref[:, c0:c0 + D_MODEL]) + bin_ref[:, c0:c0 + D_MODEL])
            term = gate * _dot(ys[n], wb_ref[n * D_BR:(n + 1) * D_BR, :])
            merged = term if merged is None else merged + term
        out_ref[rows, :] = h_ref[rows, :] + _dot(merged.astype(BF16), wo_ref[...])

    def stage_front(r0):
        stage_in(r0)
        stage_lru(r0)
        stage_pool(r0)

    blocks(R // SB, SB, stage_front)
    stage_gmlp()
    lru_scan()
    pool_tail()

    def merge_rows(r0):
        for j in range(SB5 // SBM):
            rj = r0 + j * SBM
            stage_merge(rj if isinstance(rj, int) else pl.multiple_of(rj, SBM))

    nb5 = R // SB5
    stage_s5(0)
    if nb5 > 1:
        def piped(k, c):
            r0 = pl.multiple_of(k * SB5, SB5)
            stage_s5(r0)
            merge_rows(r0 - SB5)
            return c
        lax.fori_loop(1, nb5, piped, 0)
    merge_rows((nb5 - 1) * SB5)
    sreo_ref[...] = sre_s[...]
    simo_ref[...] = sim_s[...]


def _mixer_call(h_all, blk0, nsteps, B, Tt, start_pos, mm_gmlp, emit_v, states, lp):
    R = Tt * B
    cfg = (B, Tt, start_pos, mm_gmlp, emit_v)
    conv0, lru0, sre0, sim0, pool0 = states
    small = [conv0, lru0, sre0, sim0, pool0]
    params = list(lp)
    big = {1, 16, 24, 25}
    in_specs = [pl.BlockSpec((R, D_MODEL), lambda i: (blk0 + i, 0))]
    in_specs += [_const_spec(a.shape) for a in small]
    in_specs += [_const_spec(a.shape, single=(k in big)) for k, a in enumerate(params)]
    out_shape = [jax.ShapeDtypeStruct(h_all.shape, F32),
                 jax.ShapeDtypeStruct(conv0.shape, F32), jax.ShapeDtypeStruct(lru0.shape, F32),
                 jax.ShapeDtypeStruct(sre0.shape, F32), jax.ShapeDtypeStruct(sim0.shape, F32),
                 jax.ShapeDtypeStruct(pool0.shape, F32)]
    out_specs = ([pl.BlockSpec((R, D_MODEL), lambda i: (blk0 + i, 0))]
                 + [_const_spec(s.shape) for s in out_shape[1:]])
    if emit_v:
        out_shape.append(jax.ShapeDtypeStruct((R, D_BR), F32))
        out_specs.append(_const_spec((R, D_BR)))
    SB5 = max(B, min(R, 256))
    scratch = [
        pltpu.VMEM((R, D_MODEL), BF16),
        pltpu.VMEM((R + (CONV_W - 1) * B, D_BR), F32),
        pltpu.VMEM((R + POOL_BUF * B, D_BR), F32),
        pltpu.VMEM((B, D_BR), F32),
        pltpu.VMEM((B, S5_W), F32), pltpu.VMEM((B, S5_W), F32),
        pltpu.VMEM((R, D_BR), F32),
        pltpu.VMEM((2, R, LANES), F32),
        pltpu.VMEM((2, R, LANES), F32),
        pltpu.VMEM((R, D_BR), F32),
        pltpu.VMEM((R, D_BR), F32), pltpu.VMEM((R, D_BR), F32),
        pltpu.VMEM((R, D_BR), BF16), pltpu.VMEM((R, D_BR), BF16),
        pltpu.VMEM((SB5, 2 * S5_W), F32),
    ]
    return pl.pallas_call(
        functools.partial(_mixer_body, cfg),
        grid=(nsteps,),
        in_specs=in_specs,
        out_specs=out_specs,
        out_shape=out_shape,
        scratch_shapes=scratch,
        input_output_aliases={0: 0},
        compiler_params=pltpu.CompilerParams(dimension_semantics=("arbitrary",),
                                             vmem_limit_bytes=60 * 1024 * 1024),
        name="mixer_b%d" % B,
    )(h_all, *small, *params)


def _s5prep_body(are_ref, aim_ref, ldt_ref, bre_ref, bim_ref, lbr_ref, lbi_ref, bbr_ref, bbi_ref):
    a_re, a_im = are_ref[...], aim_ref[...]
    dt = jnp.exp(ldt_ref[...])
    mag = jnp.exp(a_re * dt)
    lb_re = mag * jnp.cos(a_im * dt)
    lb_im = mag * jnp.sin(a_im * dt)
    den = a_re * a_re + a_im * a_im
    n_re = lb_re - 1.0
    q_re = (n_re * a_re + lb_im * a_im) / den
    q_im = (lb_im * a_re - n_re * a_im) / den
    lbr_ref[...] = lb_re
    lbi_ref[...] = lb_im
    bbr_ref[...] = q_re * bre_ref[...] - q_im * bim_ref[...]
    bbi_ref[...] = q_re * bim_ref[...] + q_im * bre_ref[...]


def _s5prep(a_re, a_im, log_dt, b_re, b_im):
    depth = a_re.shape[0]
    rows = depth * N_GROUPS_C * GROUP_C

    def rep(x):
        return jnp.broadcast_to(x[:, :, None, :], (depth, N_GROUPS_C, GROUP_C, P_STATE)).reshape(rows, P_STATE)

    ldt = jnp.broadcast_to(log_dt[:, :, None, None], (depth, N_GROUPS_C, GROUP_C, P_STATE)).reshape(rows, P_STATE)
    b_re_t = jnp.transpose(b_re, (0, 1, 3, 2)).reshape(rows, P_STATE)
    b_im_t = jnp.transpose(b_im, (0, 1, 3, 2)).reshape(rows, P_STATE)
    shp = jax.ShapeDtypeStruct((rows, P_STATE), F32)
    lbr, lbi, bbr, bbi = pl.pallas_call(_s5prep_body, out_shape=[shp] * 4, name="s5prep")(
        rep(a_re), rep(a_im), ldt, b_re_t, b_im_t)
    r4 = lambda x: x.reshape(depth, N_GROUPS_C, GROUP_C, P_STATE)
    return r4(lbr)[:, :, 0, :], r4(lbi)[:, :, 0, :], r4(bbr), r4(bbi)


def _ffn_body(x_ref, g_ref, w1_ref, w3_ref, w2_ref, o_ref, xn_s, acc_s):
    f = pl.program_id(1)

    @pl.when(f == 0)
    def _():
        xn_s[...] = _rms(x_ref[...], g_ref[...]).astype(BF16)
        acc_s[...] = jnp.zeros_like(acc_s)

    xn = xn_s[...]
    a = _dot(xn, w1_ref[...])
    b = _dot(xn, w3_ref[...])
    acc_s[...] += _dot((a * _sigmoid(a) * b).astype(BF16), w2_ref[...])

    @pl.when(f == pl.num_programs(1) - 1)
    def _():
        o_ref[...] = x_ref[...] + acc_s[...]


def _ffn_call(h, g, w1, w3, w2, tm, tf):
    n = h.shape[0]
    dff = w1.shape[1]
    return pl.pallas_call(
        _ffn_body,
        grid=(n // tm, dff // tf),
        in_specs=[pl.BlockSpec((tm, D_MODEL), lambda i, f: (i, 0)),
                  pl.BlockSpec((1, D_MODEL), lambda i, f: (0, 0)),
                  pl.BlockSpec((D_MODEL, tf), lambda i, f: (0, f)),
                  pl.BlockSpec((D_MODEL, tf), lambda i, f: (0, f)),
                  pl.BlockSpec((tf, D_MODEL), lambda i, f: (f, 0))],
        out_specs=pl.BlockSpec((tm, D_MODEL), lambda i, f: (i, 0)),
        out_shape=jax.ShapeDtypeStruct((n, D_MODEL), F32),
        scratch_shapes=[pltpu.VMEM((tm, D_MODEL), BF16), pltpu.VMEM((tm, D_MODEL), F32)],
        compiler_params=pltpu.CompilerParams(dimension_semantics=("arbitrary", "arbitrary"),
                                             vmem_limit_bytes=56 * 1024 * 1024),
        name="ffn",
    )(h, g, w1, w3, w2)


def _router_body(x_ref, g_ref, rw_ref, idx_ref, gate_ref):
    xn = _rms(x_ref[...], g_ref[...])
    logits = jnp.dot(xn, rw_ref[...], preferred_element_type=F32, precision=lax.Precision.HIGHEST)
    lane = lax.broadcasted_iota(I32, logits.shape, 1)
    neg = jnp.float32(-jnp.inf)
    logits = jnp.where(lane < N_EXPERTS, logits, neg)
    m1 = jnp.max(logits, axis=-1, keepdims=True)
    i1 = jnp.min(jnp.where(logits == m1, lane, LANES), axis=-1, keepdims=True)
    rest = jnp.where(lane == i1, neg, logits)
    m2 = jnp.max(rest, axis=-1, keepdims=True)
    i2 = jnp.min(jnp.where(rest == m2, lane, LANES), axis=-1, keepdims=True)
    e = jnp.exp(m2 - m1)
    g1 = 1.0 / (1.0 + e)
    g2 = e / (1.0 + e)
    idx_ref[...] = jnp.where(lane == 0, i1, jnp.where(lane == 1, i2, 0))
    gate_ref[...] = jnp.where(lane == 0, g1, jnp.where(lane == 1, g2, 0.0))


def _router_call(h, g, rw_pad, tm):
    n = h.shape[0]
    return pl.pallas_call(
        _router_body,
        grid=(n // tm,),
        in_specs=[pl.BlockSpec((tm, D_MODEL), lambda i: (i, 0)),
                  _const_spec((1, D_MODEL)), _const_spec((D_MODEL, LANES))],
        out_specs=[pl.BlockSpec((tm, LANES), lambda i: (i, 0))] * 2,
        out_shape=[jax.ShapeDtypeStruct((n, LANES), I32), jax.ShapeDtypeStruct((n, LANES), F32)],
        compiler_params=pltpu.CompilerParams(dimension_semantics=("arbitrary",)),
        name="router",
    )(h, g, rw_pad)


def _row_copy_wait(hbm_ref, sem, nrows):
    pltpu.make_async_copy(hbm_ref.at[pl.ds(0, nrows)], hbm_ref.at[pl.ds(0, nrows)], sem).wait()


def _dispatch_body(tm, pos_ref, x_ref, g_ref, xs_in_ref, xs_ref, buf, sem):
    del xs_in_ref
    xn = _rms(x_ref[...], g_ref[...])
    for k in range(ROW_TILE):
        buf[pl.ds(k, tm, stride=ROW_TILE), :] = xn[:, k * LANES:(k + 1) * LANES]

    def issue(r, c):
        src = buf.at[pl.ds(pl.multiple_of(r * ROW_TILE, ROW_TILE), ROW_TILE), :]
        pltpu.make_async_copy(src, xs_ref.at[pos_ref[0, 0, 2 * r]], sem.at[0]).start(priority=0)
        pltpu.make_async_copy(src, xs_ref.at[pos_ref[0, 0, 2 * r + 1]], sem.at[0]).start(priority=1)
        return c

    lax.fori_loop(0, tm, issue, 0)
    _row_copy_wait(xs_ref, sem.at[0], 2 * tm)


def _dispatch_call(h, g, pos, xs_zero, tm):
    n = h.shape[0]
    nt = n // tm
    return pl.pallas_call(
        functools.partial(_dispatch_body, tm),
        grid=(nt,),
        in_specs=[pl.BlockSpec((1, 1, 2 * tm), lambda i: (i, 0, 0), memory_space=pltpu.SMEM),
                  pl.BlockSpec((tm, D_MODEL), lambda i: (i, 0)),
                  _const_spec((1, D_MODEL)),
                  pl.BlockSpec(memory_space=pl.ANY)],
        out_specs=pl.BlockSpec(memory_space=pl.ANY),
        out_shape=jax.ShapeDtypeStruct(xs_zero.shape, F32),
        scratch_shapes=[pltpu.VMEM((tm * ROW_TILE, LANES), F32), pltpu.SemaphoreType.DMA((1,))],
        input_output_aliases={3: 0},
        compiler_params=pltpu.CompilerParams(dimension_semantics=("arbitrary",)),
        name="dispatch",
    )(pos.reshape(nt, 1, 2 * tm), h, g, xs_zero)


def _experts_body(tm, te_ref, tv_ref, xs_ref, w1_ref, w3_ref, w2_ref, y_ref, x_s, acc_s):
    t = pl.program_id(0)
    f = pl.program_id(1)
    valid = tv_ref[t] > 0

    @pl.when(jnp.logical_and(valid, f == 0))
    def _():
        for k in range(ROW_TILE):
            x_s[:, k * LANES:(k + 1) * LANES] = xs_ref[pl.ds(k, tm, stride=ROW_TILE), :].astype(BF16)
        acc_s[...] = jnp.zeros_like(acc_s)

    @pl.when(valid)
    def _():
        x = x_s[...]
        a = _dot(x, w1_ref[0].astype(BF16))
        b = _dot(x, w3_ref[0].astype(BF16))
        acc_s[...] += _dot((a * _sigmoid(a) * b).astype(BF16), w2_ref[0].astype(BF16))

    last = f == pl.num_programs(1) - 1

    @pl.when(jnp.logical_and(valid, last))
    def _():
        acc = acc_s[...]
        for k in range(ROW_TILE):
            y_ref[pl.ds(k, tm, stride=ROW_TILE), :] = acc[:, k * LANES:(k + 1) * LANES]

    @pl.when(jnp.logical_and(jnp.logical_not(valid), last))
    def _():
        y_ref[...] = jnp.zeros_like(y_ref)


def _experts_call(xs2d, tile_expert, tile_valid, w1, w3, w2, tm, tf):
    rows = xs2d.shape[0] // ROW_TILE
    nt = rows // tm
    dff = w1.shape[2]
    grid_spec = pltpu.PrefetchScalarGridSpec(
        num_scalar_prefetch=2,
        grid=(nt, dff // tf),
        in_specs=[pl.BlockSpec((tm * ROW_TILE, LANES), lambda t, f, te, tv: (t, 0)),
                  pl.BlockSpec((1, D_MODEL, tf), lambda t, f, te, tv: (te[t], 0, f)),
                  pl.BlockSpec((1, D_MODEL, tf), lambda t, f, te, tv: (te[t], 0, f)),
                  pl.BlockSpec((1, tf, D_MODEL), lambda t, f, te, tv: (te[t], f, 0))],
        out_specs=pl.BlockSpec((tm * ROW_TILE, LANES), lambda t, f, te, tv: (t, 0)),
        scratch_shapes=[pltpu.VMEM((tm, D_MODEL), BF16), pltpu.VMEM((tm, D_MODEL), F32)])
    return pl.pallas_call(
        functools.partial(_experts_body, tm),
        grid_spec=grid_spec,
        out_shape=jax.ShapeDtypeStruct(xs2d.shape, F32),
        compiler_params=pltpu.CompilerParams(dimension_semantics=("arbitrary", "arbitrary"),
                                             vmem_limit_bytes=56 * 1024 * 1024),
        name="experts",
    )(tile_expert, tile_valid, xs2d, w1, w3, w2)


def _combine_body(tm, final, pos_ref, h_ref, gate_ref, *rest):
    if final:
        ng_ref, y_ref, o_ref, buf, sem = rest
    else:
        y_ref, o_ref, buf, sem = rest

    def issue(r, c):
        for j in range(2):
            dst = buf.at[j, pl.ds(pl.multiple_of(r * ROW_TILE, ROW_TILE), ROW_TILE), :]
            pltpu.make_async_copy(y_ref.at[pos_ref[0, 0, 2 * r + j]], dst, sem.at[0]).start(priority=j)
        return c

    lax.fori_loop(0, tm, issue, 0)
    _row_copy_wait(y_ref, sem.at[0], 2 * tm)
    g0 = gate_ref[:, 0:1]
    g1 = gate_ref[:, 1:2]
    for k in range(ROW_TILE):
        cols = slice(k * LANES, (k + 1) * LANES)
        moe = (g0 * buf[0, pl.ds(k, tm, stride=ROW_TILE), :]
               + g1 * buf[1, pl.ds(k, tm, stride=ROW_TILE), :])
        o_ref[:, cols] = h_ref[:, cols] + moe
    if final:
        o_ref[...] = _rms(o_ref[...], ng_ref[...])


def _combine_call(h, gates, pos, y3, tm, final_g=None):
    n = h.shape[0]
    nt = n // tm
    final = final_g is not None
    in_specs = [pl.BlockSpec((1, 1, 2 * tm), lambda i: (i, 0, 0), memory_space=pltpu.SMEM),
                pl.BlockSpec((tm, D_MODEL), lambda i: (i, 0)),
                pl.BlockSpec((tm, LANES), lambda i: (i, 0))]
    args = [pos.reshape(nt, 1, 2 * tm), h, gates]
    if final:
        in_specs.append(_const_spec((1, D_MODEL)))
        args.append(final_g)
    in_specs.append(pl.BlockSpec(memory_space=pl.ANY))
    args.append(y3)
    return pl.pallas_call(
        functools.partial(_combine_body, tm, final),
        grid=(nt,),
        in_specs=in_specs,
        out_specs=pl.BlockSpec((tm, D_MODEL), lambda i: (i, 0)),
        out_shape=jax.ShapeDtypeStruct((n, D_MODEL), F32),
        scratch_shapes=[pltpu.VMEM((2, tm * ROW_TILE, LANES), F32), pltpu.SemaphoreType.DMA((1,))],
        compiler_params=pltpu.CompilerParams(dimension_semantics=("arbitrary",)),
        name="combine",
    )(*args)


def _rms_body(x_ref, g_ref, o_ref):
    o_ref[...] = _rms(x_ref[...], g_ref[...])


def _rms_call(h, g, tm):
    n = h.shape[0]
    return pl.pallas_call(
        _rms_body, grid=(n // tm,),
        in_specs=[pl.BlockSpec((tm, D_MODEL), lambda i: (i, 0)), _const_spec((1, D_MODEL))],
        out_specs=pl.BlockSpec((tm, D_MODEL), lambda i: (i, 0)),
        out_shape=jax.ShapeDtypeStruct((n, D_MODEL), F32),
        name="rms",
    )(h, g)


MOE_TM = 768
MOE_TF = 512
TOK_TM = 768
FFN_TF = 1408


def _moe_layer(h, norm_g, rw_pad, w1, w3, w2, e0, final_g):
    n = h.shape[0]
    idx, gates = _router_call(h, norm_g, rw_pad, TOK_TM)
    e_flat = idx[:, 0:2].reshape(2 * n)
    onehot = (e_flat[:, None] == jnp.arange(N_EXPERTS, dtype=I32)[None, :]).astype(I32)
    csum = jnp.cumsum(onehot, axis=0)
    rank = jnp.sum((csum - onehot) * onehot, axis=1)
    counts = csum[-1]
    padded = ((counts + MOE_TM - 1) // MOE_TM) * MOE_TM
    ends = jnp.cumsum(padded)
    starts = ends - padded
    pos = (jnp.sum(onehot * starts[None, :], axis=1) + rank).astype(I32)
    n_tiles = (2 * n) // MOE_TM + N_EXPERTS
    tile_start = jnp.arange(n_tiles, dtype=I32) * MOE_TM
    tile_valid = (tile_start < ends[-1]).astype(I32)
    tile_expert = jnp.minimum(jnp.sum((tile_start[:, None] >= ends[None, :]).astype(I32), axis=1),
                              N_EXPERTS - 1).astype(I32)
    last_valid = jnp.max(jnp.where(tile_valid > 0, tile_expert, 0))
    tile_expert = jnp.where(tile_valid > 0, tile_expert, last_valid)
    xs_zero = jnp.zeros((n_tiles * MOE_TM, ROW_TILE, LANES), F32)
    xs = _dispatch_call(h, norm_g, pos, xs_zero, TOK_TM)
    y2d = _experts_call(xs.reshape(n_tiles * MOE_TM * ROW_TILE, LANES), tile_expert + e0, tile_valid,
                        w1, w3, w2, MOE_TM, MOE_TF)
    return _combine_call(h, gates, pos, y2d.reshape(n_tiles * MOE_TM, ROW_TILE, LANES), TOK_TM, final_g)


def _block_diag(w):
    k, a, b = w.shape
    eye = jnp.eye(k, dtype=w.dtype)
    return jnp.einsum('kab,kj->kajb', w, eye).reshape(k * a, k * b)


def kernel(x_prompt, x_sample, state_conv, state_lru, state_s5_re, state_s5_im, state_pool, norm_mix_g, w_in, b_in, gmlp_ln_g, gmlp_ln_b, gmlp_w_s, gmlp_b_s, conv_w, conv_b, lru_w_a, lru_b_a, lru_w_x, lru_b_x, lru_lam, s5_a_re, s5_a_im, s5_log_dt, s5_b_re, s5_b_im, s5_c_re, s5_c_im, s5_d, s5_w_glu, s5_b_glu, pool_w, pool_scale, w_branch, w_out, norm_ffn_g, ffn_w1, ffn_w3, ffn_w2, router_w, moe_w1, moe_w3, moe_w2, norm_final_g):
    depth = w_in.shape[0]
    bp, tp, _ = x_prompt.shape
    bs, ts, _ = x_sample.shape
    n_p, n_s = bp * tp, bs * ts
    assert n_p % (CHUNK * bp) == 0 and n_p % n_s == 0

    h = jnp.concatenate([jnp.transpose(x_prompt, (1, 0, 2)).reshape(n_p, D_MODEL),
                         jnp.transpose(x_sample, (1, 0, 2)).reshape(n_s, D_MODEL)], axis=0)

    lbr, lbi, bbr, bbi = _s5prep(s5_a_re, s5_a_im, s5_log_dt, s5_b_re, s5_b_im)
    row = lambda x: x.reshape(1, -1)
    lane_head = jnp.arange(D_BR) // HEAD_DIM

    conv_p, lru_p, sre_p, sim_p, pool_p = [], [], [], [], []
    conv_s, lru_s, sre_s, sim_s, pool_s, v_s = [], [], [], [], [], []
    zeros_p = (jnp.zeros(((CONV_W - 1) * bp, D_BR), F32), jnp.zeros((bp, D_BR), F32),
               jnp.zeros((bp, S5_W), F32), jnp.zeros((bp, S5_W), F32),
               jnp.zeros((POOL_BUF * bp, D_BR), F32))
    for l in range(depth):
        bblk = jnp.concatenate([_block_diag(bbr[l]), _block_diag(bbi[l])], axis=1).astype(BF16)
        cre = _block_diag(jnp.transpose(s5_c_re[l], (0, 2, 1))).astype(BF16)
        cim = _block_diag(jnp.transpose(s5_c_im[l], (0, 2, 1))).astype(BF16)
        common_tail = (
            conv_w[l], row(conv_b[l]),
            _block_diag(lru_w_a[l]).astype(BF16), row(lru_b_a[l]),
            _block_diag(lru_w_x[l]).astype(BF16), row(lru_b_x[l]), row(lru_lam[l]),
            row(lbr[l]), row(lbi[l]), bblk, cre, cim, row(s5_d[l]),
            s5_w_glu[l].astype(BF16), row(s5_b_glu[l]),
            _block_diag(pool_w[l]).astype(BF16), row(pool_scale[l]),
            w_branch[l].reshape(N_BRANCH * D_BR, D_MODEL).astype(BF16), w_out[l].astype(BF16))
        common_head = (row(norm_mix_g[l]), w_in[l].astype(BF16), row(b_in[l]),
                       row(gmlp_ln_g[l]), row(gmlp_ln_b[l]))
        bsm_p = gmlp_b_s[l][lane_head, :].T
        lp_p = common_head + (gmlp_w_s[l], bsm_p) + common_tail
        ws_small = jnp.transpose(gmlp_w_s[l][:, :ts, :ts], (1, 2, 0))[:, :, lane_head].reshape(ts * ts, D_BR)
        bsm_s = gmlp_b_s[l][lane_head, :ts].T
        lp_s = common_head + (ws_small, bsm_s) + common_tail

        outs_p = _mixer_call(h, 0, tp // CHUNK, bp, CHUNK, 0, True, False, zeros_p, lp_p)
        st_s = (jnp.transpose(state_conv[l], (1, 0, 2)).reshape((CONV_W - 1) * bs, D_BR),
                state_lru[l], state_s5_re[l].reshape(bs, S5_W), state_s5_im[l].reshape(bs, S5_W),
                jnp.transpose(state_pool[l], (1, 0, 2)).reshape(POOL_BUF * bs, D_BR))
        outs_s = _mixer_call(outs_p[0], n_p // n_s, 1, bs, ts, PAST_LEN, False, True, st_s, lp_s)
        h = outs_s[0]
        for lst, o in zip((conv_p, lru_p, sre_p, sim_p, pool_p), outs_p[1:6]):
            lst.append(o)
        for lst, o in zip((conv_s, lru_s, sre_s, sim_s, pool_s, v_s), outs_s[1:7]):
            lst.append(o)

        if l % 2 == 0:
            k = l // 2
            h = _ffn_call(h, row(norm_ffn_g[l]), ffn_w1[k].astype(BF16), ffn_w3[k].astype(BF16),
                          ffn_w2[k].astype(BF16), TOK_TM, FFN_TF)
            if l == depth - 1:
                h = _rms_call(h, row(norm_final_g), TOK_TM)
        else:
            k = l // 2
            rw_pad = jnp.pad(router_w[k], ((0, 0), (0, LANES - N_EXPERTS)))
            stk = lambda w: w.reshape((-1,) + w.shape[2:])
            h = _moe_layer(h, row(norm_ffn_g[l]), rw_pad, stk(moe_w1), stk(moe_w3), stk(moe_w2),
                           k * N_EXPERTS, row(norm_final_g) if l == depth - 1 else None)

    def tm_to_bm(x, t, b):
        return jnp.transpose(x.reshape(t, b, x.shape[-1]), (1, 0, 2))

    y_prompt = tm_to_bm(h[:n_p], tp, bp)
    y_sample = tm_to_bm(h[n_p:], ts, bs)
    stack = lambda lst, f: jnp.stack([f(o) for o in lst])
    return (
        y_prompt, y_sample,
        stack(conv_p, lambda o: tm_to_bm(o, CONV_W - 1, bp)),
        stack(lru_p, lambda o: o),
        stack(sre_p, lambda o: o.reshape(bp, N_GROUPS_C, P_STATE)),
        stack(sim_p, lambda o: o.reshape(bp, N_GROUPS_C, P_STATE)),
        stack(pool_p, lambda o: tm_to_bm(o, POOL_BUF, bp)),
        stack(conv_s, lambda o: tm_to_bm(o, CONV_W - 1, bs)),
        stack(lru_s, lambda o: o),
        stack(sre_s, lambda o: o.reshape(bs, N_GROUPS_C, P_STATE)),
        stack(sim_s, lambda o: o.reshape(bs, N_GROUPS_C, P_STATE)),
        stack(pool_s, lambda o: tm_to_bm(o, POOL_BUF, bs)),
        stack(v_s, lambda o: tm_to_bm(o, ts, bs)),
    )
```

```python
import functools
import math

import jax
import jax.numpy as jnp
from jax import lax
from jax.experimental import pallas as pl
from jax.experimental.pallas import tpu as pltpu

F32 = jnp.float32
BF16 = jnp.bfloat16
I32 = jnp.int32

SUBLANES = 8
LANES = 128

D_MODEL = 1024
D_BR = 256
N_BRANCH = 4
N_HEADS = 4
HEAD_DIM = D_BR // N_HEADS
CHUNK = 128
CONV_W = 4
LRU_C = 8.0
N_GROUPS_C = 16
GROUP_C = 16
P_STATE = 64
S5_W = N_GROUPS_C * P_STATE
POOL_WINDOWS = (2, 4, 8, 16)
POOL_BUF = max(POOL_WINDOWS) - 1
GROUP_D = D_BR // len(POOL_WINDOWS)
COLS_A = 5 * D_BR
IN_COLS = COLS_A + N_BRANCH * D_MODEL
N_EXPERTS = 8
EPS = 1e-6
PAST_LEN = 16384
SQRT_2_OVER_PI = math.sqrt(2.0 / math.pi)
ROW_TILE = D_MODEL // LANES


def _gelu(x):
    return x * (0.5 * (1.0 + jnp.tanh(SQRT_2_OVER_PI * (x + 0.044715 * (x * x * x)))))


def _sigmoid(x):
    return 1.0 / (1.0 + jnp.exp(-x))


def _rms(x, g):
    return x * lax.rsqrt(jnp.mean(x * x, axis=-1, keepdims=True) + EPS) * g


def _dot(a, b):
    return jnp.dot(a, b, preferred_element_type=F32)


def _const_spec(shape, single=False):
    nd = len(shape)
    if single:
        return pl.BlockSpec(shape, lambda *_: (0,) * nd, pipeline_mode=pl.Buffered(1))
    return pl.BlockSpec(shape, lambda *_: (0,) * nd)


def _mixer_body(cfg, *refs):
    B, Tt, start_pos, mm_gmlp, emit_v = cfg
    R = Tt * B
    refs = list(refs)
    h_ref = refs.pop(0)
    (conv0_ref, lru0_ref, sre0_ref, sim0_ref, pool0_ref,
     ng_ref, win_ref, bin_ref, lng_ref, lnb_ref, ws_ref, bsm_ref,
     cw_ref, cb_ref, wa_ref, ba_ref, wx_ref, bx_ref, lam_ref,
     lbr_ref, lbi_ref, bblk_ref, cre_ref, cim_ref, d_ref, wglu_ref, bglu_ref,
     pw_ref, ps_ref, wb_ref, wo_ref) = refs[:31]
    n_out = 7 if emit_v else 6
    outs = refs[31:31 + n_out]
    out_ref, convo_ref, lruo_ref, sreo_ref, simo_ref, poolo_ref = outs[:6]
    (xn_s, cext_s, pext_s, lruh_s, sre_s, sim_s, u_s, v_s, mix_s, xc_s,
     a_s, b_s, yc_s, yd_s, s5_s) = refs[31 + n_out:]

    i = pl.program_id(0)
    SB = min(R, 256)
    SB5 = max(B, min(R, 256))
    lane256 = lax.broadcasted_iota(I32, (1, D_BR), 1)
    ydt = yc_s.dtype

    def mm(a, w):
        return _dot(a.astype(BF16), w)

    def init_carries():
        cext_s[0:(CONV_W - 1) * B, :] = conv0_ref[...]
        pext_s[0:POOL_BUF * B, :] = pool0_ref[...]
        lruh_s[...] = lru0_ref[...]
        sre_s[...] = sre0_ref[...]
        sim_s[...] = sim0_ref[...]

    def blocks(n, size, fn):
        if n == 1:
            fn(0)
        else:
            def body(k, c):
                fn(pl.multiple_of(k * size, size))
                return c
            lax.fori_loop(0, n, body, 0)

    def stage_in(r0):
        rows = pl.ds(r0, SB)
        xn = _rms(h_ref[rows, :], ng_ref[...]).astype(BF16)
        xn_s[rows, :] = xn
        pa = _dot(xn, win_ref[:, 0:COLS_A]) + bin_ref[:, 0:COLS_A]
        u_s[rows, :] = _gelu(pa[:, 0:D_BR])
        gv = _gelu(pa[:, D_BR:2 * D_BR])
        mu = jnp.mean(gv, axis=-1, keepdims=True)
        var = jnp.mean(jnp.square(gv - mu), axis=-1, keepdims=True)
        v = (gv - mu) * lax.rsqrt(var + EPS) * lng_ref[...] + lnb_ref[...]
        v_s[0, rows, :] = v[:, 0:LANES]
        v_s[1, rows, :] = v[:, LANES:2 * LANES]
        cext_s[pl.ds(r0 + (CONV_W - 1) * B, SB), :] = pa[:, 2 * D_BR:3 * D_BR]
        xc_s[rows, :] = pa[:, 3 * D_BR:4 * D_BR]
        pext_s[pl.ds(r0 + POOL_BUF * B, SB), :] = pa[:, 4 * D_BR:5 * D_BR]

    def stage_gmlp():
        if mm_gmlp:
            tril = (lax.broadcasted_iota(I32, (CHUNK, CHUNK), 0)
                    >= lax.broadcasted_iota(I32, (CHUNK, CHUNK), 1))
            wm = [jnp.where(tril, ws_ref[hd], 0.0).astype(BF16) for hd in range(N_HEADS)]
            head = lane256 // HEAD_DIM
            for b in range(B):
                vb = jnp.concatenate([v_s[0, pl.ds(b, Tt, stride=B), :],
                                      v_s[1, pl.ds(b, Tt, stride=B), :]], axis=1).astype(BF16)
                mixed = bsm_ref[...]
                for hd in range(N_HEADS):
                    mixed = mixed + jnp.where(head == hd, mm(wm[hd], vb), 0.0)
                mix_s[0, pl.ds(b, Tt, stride=B), :] = mixed[:, 0:LANES]
                mix_s[1, pl.ds(b, Tt, stride=B), :] = mixed[:, LANES:2 * LANES]
        else:
            for t in range(Tt):
                for half in range(2):
                    lo = half * LANES
                    acc = jnp.broadcast_to(bsm_ref[t:t + 1, lo:lo + LANES], (B, LANES))
                    for s in range(t + 1):
                        w = ws_ref[t * Tt + s:t * Tt + s + 1, lo:lo + LANES]
                        acc = acc + w * v_s[half, s * B:(s + 1) * B, :]
                    mix_s[half, t * B:(t + 1) * B, :] = acc
        if emit_v:
            vo_ref = outs[6]
            vo_ref[:, 0:LANES] = v_s[0]
            vo_ref[:, LANES:2 * LANES] = v_s[1]

    log_sig_lam = (jnp.minimum(lam_ref[...], 0.0)
                   - jnp.log1p(jnp.exp(-jnp.abs(lam_ref[...]))))

    def stage_lru(r0):
        rows = pl.ds(r0, SB)
        conv = cb_ref[...] + cext_s[pl.ds(r0, SB), :] * cw_ref[0:1, :]
        for k in range(1, CONV_W):
            conv = conv + cext_s[pl.ds(r0 + k * B, SB), :] * cw_ref[k:k + 1, :]
        cbf = conv.astype(BF16)
        r = _sigmoid(mm(cbf, wa_ref[...]) + ba_ref[...])
        ig = _sigmoid(mm(cbf, wx_ref[...]) + bx_ref[...])
        a = jnp.exp(LRU_C * r * log_sig_lam)
        mult = jnp.sqrt(1.0 - a * a)
        if start_pos == 0:
            row = lax.broadcasted_iota(I32, (SB, D_BR), 0) + r0
            mult = jnp.where(jnp.logical_and(i == 0, row < B), 1.0, mult)
        a_s[rows, :] = a
        b_s[rows, :] = mult * ig * conv

    def lru_step(t, hprev):
        rows = pl.ds(t * B, B) if isinstance(t, int) else pl.ds(pl.multiple_of(t * B, B), B)
        hnew = a_s[rows, :] * hprev + b_s[rows, :]
        b_s[rows, :] = hnew
        return hnew

    def lru_scan():
        if Tt <= 8:
            hl = lruh_s[...]
            for t in range(Tt):
                hl = lru_step(t, hl)
        else:
            hl = lax.fori_loop(0, Tt, lru_step, lruh_s[...], unroll=8)
        lruh_s[...] = hl
        lruo_ref[...] = hl
        conv_tail = cext_s[R:R + (CONV_W - 1) * B, :]
        convo_ref[...] = conv_tail
        cext_s[0:(CONV_W - 1) * B, :] = conv_tail

    lbr = jnp.broadcast_to(lbr_ref[...], (B, S5_W))
    lbi = jnp.broadcast_to(lbi_ref[...], (B, S5_W))
    steps5 = SB5 // B

    def stage_s5(r0):
        rows = pl.ds(r0, SB5)
        xc = xc_s[rows, :]
        s5_s[...] = mm(xc, bblk_ref[...])

        def s5_step(t, carry):
            sre, sim = carry
            rr = pl.ds(t * B, B) if isinstance(t, int) else pl.ds(pl.multiple_of(t * B, B), B)
            nre = lbr * sre - lbi * sim + s5_s[rr, 0:S5_W]
            nim = lbr * sim + lbi * sre + s5_s[rr, S5_W:2 * S5_W]
            s5_s[rr, 0:S5_W] = nre
            s5_s[rr, S5_W:2 * S5_W] = nim
            return nre, nim

        carry = (sre_s[...], sim_s[...])
        for t in range(steps5):
            carry = s5_step(t, carry)
        sre_s[...] = carry[0]
        sim_s[...] = carry[1]
        y = (mm(s5_s[:, 0:S5_W], cre_ref[...])
             - mm(s5_s[:, S5_W:2 * S5_W], cim_ref[...])
             + d_ref[...] * xc)
        y = _gelu(y)
        y = y * _sigmoid(mm(y, wglu_ref[...]) + bglu_ref[...])
        yc_s[rows, :] = y.astype(ydt)

    wlane = jnp.where(lane256 < GROUP_D, POOL_WINDOWS[0],
                      jnp.where(lane256 < 2 * GROUP_D, POOL_WINDOWS[1],
                                jnp.where(lane256 < 3 * GROUP_D, POOL_WINDOWS[2], POOL_WINDOWS[3])))

    def stage_pool(r0):
        base = r0 + POOL_BUF * B
        tok = pext_s[pl.ds(base, SB), :]
        acc = tok
        sums = {}
        for j in range(1, max(POOL_WINDOWS)):
            acc = acc + pext_s[pl.ds(base - j * B, SB), :]
            if j + 1 in POOL_WINDOWS:
                sums[j + 1] = acc
        sel = jnp.where(lane256 < GROUP_D, sums[2],
                        jnp.where(lane256 < 2 * GROUP_D, sums[4],
                                  jnp.where(lane256 < 3 * GROUP_D, sums[8], sums[16])))
        if start_pos >= POOL_BUF:
            cnt = wlane.astype(F32)
        else:
            row = lax.broadcasted_iota(I32, (SB, D_BR), 0) + r0
            tpos = start_pos + i * Tt + lax.shift_right_logical(row, jnp.full_like(row, int(math.log2(B))))
            cnt = jnp.minimum(wlane, tpos + 1).astype(F32)
        diff = sel / cnt - tok
        yd_s[pl.ds(r0, SB), :] = (mm(diff, pw_ref[...]) * ps_ref[...]).astype(ydt)

    def pool_tail():
        tail = pext_s[R:R + POOL_BUF * B, :]
        poolo_ref[...] = tail
        pext_s[0:POOL_BUF * B, :] = tail

    SBM = min(R, 128)

    def stage_merge(r0):
        rows = pl.ds(r0, SBM)
        xn = xn_s[rows, :]
        ya = u_s[rows, :] * jnp.concatenate([mix_s[0, rows, :], mix_s[1, rows, :]], axis=1)
        ys = (ya, b_s[rows, :], yc_s[rows, :], yd_s[rows, :])
        merged = None
        for n in range(N_BRANCH):
            c0 = COLS_A + n * D_MODEL
            logits = _dot(xn, win_ref[:, c0:c0 + D_MODEL]) + bin_ref[:, c0:c0 + D_MODEL]
            term = _sigmoid(logits) * mm(ys[n], wb_ref[n * D_BR:(n + 1) * D_BR, :])
            merged = term if merged is None else merged + term
        out_ref[rows, :] = h_ref[rows, :] + mm(merged, wo_ref[...])

    def stage_front(r0):
        stage_in(r0)
        stage_lru(r0)
        stage_pool(r0)

    def sequence_stages():
        blocks(R // SB, SB, stage_front)
        stage_gmlp()
        lru_scan()
        pool_tail()

    def merge_rows(r0):
        for j in range(SB5 // SBM):
            rj = r0 + j * SBM
            stage_merge(rj if isinstance(rj, int) else pl.multiple_of(rj, SBM))

    nb5 = R // SB5
    pl.when(i == 0)(init_carries)
    sequence_stages()
    stage_s5(0)
    if nb5 > 1:
        def piped(k, c):
            r0 = pl.multiple_of(k * SB5, SB5)
            stage_s5(r0)
            merge_rows(r0 - SB5)
            return c
        lax.fori_loop(1, nb5, piped, 0)
    merge_rows((nb5 - 1) * SB5)
    sreo_ref[...] = sre_s[...]
    simo_ref[...] = sim_s[...]


def _mixer_call(h_all, blk0, nsteps, B, Tt, start_pos, mm_gmlp, emit_v, states, lp):
    R = Tt * B
    cfg = (B, Tt, start_pos, mm_gmlp, emit_v)
    conv0, lru0, sre0, sim0, pool0 = states
    small = [conv0, lru0, sre0, sim0, pool0]
    params = list(lp)
    big = {1, 16, 24, 25}
    h_spec = pl.BlockSpec((R, D_MODEL), lambda i: (blk0 + i, 0))
    in_specs = [h_spec]
    in_specs += [_const_spec(a.shape) for a in small]
    in_specs += [_const_spec(a.shape, single=(k in big)) for k, a in enumerate(params)]
    out_shape = [jax.ShapeDtypeStruct(h_all.shape, F32),
                 jax.ShapeDtypeStruct(conv0.shape, F32), jax.ShapeDtypeStruct(lru0.shape, F32),
                 jax.ShapeDtypeStruct(sre0.shape, F32), jax.ShapeDtypeStruct(sim0.shape, F32),
                 jax.ShapeDtypeStruct(pool0.shape, F32)]
    out_specs = [h_spec] + [_const_spec(s.shape) for s in out_shape[1:]]
    if emit_v:
        out_shape.append(jax.ShapeDtypeStruct((R, D_BR), F32))
        out_specs.append(_const_spec((R, D_BR)))
    SB5 = max(B, min(R, 256))
    scratch = [
        pltpu.VMEM((R, D_MODEL), BF16),
        pltpu.VMEM((R + (CONV_W - 1) * B, D_BR), F32),
        pltpu.VMEM((R + POOL_BUF * B, D_BR), F32),
        pltpu.VMEM((B, D_BR), F32),
        pltpu.VMEM((B, S5_W), F32), pltpu.VMEM((B, S5_W), F32),
        pltpu.VMEM((R, D_BR), F32),
        pltpu.VMEM((2, R, LANES), F32),
        pltpu.VMEM((2, R, LANES), F32),
        pltpu.VMEM((R, D_BR), F32),
        pltpu.VMEM((R, D_BR), F32), pltpu.VMEM((R, D_BR), F32),
        pltpu.VMEM((R, D_BR), BF16), pltpu.VMEM((R, D_BR), BF16),
        pltpu.VMEM((SB5, 2 * S5_W), F32),
    ]
    return pl.pallas_call(
        functools.partial(_mixer_body, cfg),
        grid=(nsteps,),
        in_specs=in_specs,
        out_specs=out_specs,
        out_shape=out_shape,
        scratch_shapes=scratch,
        input_output_aliases={0: 0},
        compiler_params=pltpu.CompilerParams(dimension_semantics=("arbitrary",),
                                             vmem_limit_bytes=60 * 1024 * 1024),
        name="mixer_b%d" % B,
    )(h_all, *small, *params)


def _s5prep_body(are_ref, aim_ref, ldt_ref, bre_ref, bim_ref, lbr_ref, lbi_ref, bbr_ref, bbi_ref):
    a_re, a_im = are_ref[...], aim_ref[...]
    dt = jnp.exp(ldt_ref[...])
    mag = jnp.exp(a_re * dt)
    lb_re = mag * jnp.cos(a_im * dt)
    lb_im = mag * jnp.sin(a_im * dt)
    den = a_re * a_re + a_im * a_im
    n_re = lb_re - 1.0
    q_re = (n_re * a_re + lb_im * a_im) / den
    q_im = (lb_im * a_re - n_re * a_im) / den
    lbr_ref[...] = lb_re
    lbi_ref[...] = lb_im
    bbr_ref[...] = q_re * bre_ref[...] - q_im * bim_ref[...]
    bbi_ref[...] = q_re * bim_ref[...] + q_im * bre_ref[...]


def _s5prep(a_re, a_im, log_dt, b_re, b_im):
    depth = a_re.shape[0]
    rows = depth * N_GROUPS_C * GROUP_C

    def rep(x):
        return jnp.broadcast_to(x[:, :, None, :], (depth, N_GROUPS_C, GROUP_C, P_STATE)).reshape(rows, P_STATE)

    ldt = jnp.broadcast_to(log_dt[:, :, None, None], (depth, N_GROUPS_C, GROUP_C, P_STATE)).reshape(rows, P_STATE)
    b_re_t = jnp.transpose(b_re, (0, 1, 3, 2)).reshape(rows, P_STATE)
    b_im_t = jnp.transpose(b_im, (0, 1, 3, 2)).reshape(rows, P_STATE)
    shp = jax.ShapeDtypeStruct((rows, P_STATE), F32)
    lbr, lbi, bbr, bbi = pl.pallas_call(_s5prep_body, out_shape=[shp] * 4, name="s5prep")(
        rep(a_re), rep(a_im), ldt, b_re_t, b_im_t)
    r4 = lambda x: x.reshape(depth, N_GROUPS_C, GROUP_C, P_STATE)
    return r4(lbr)[:, :, 0, :], r4(lbi)[:, :, 0, :], r4(bbr), r4(bbi)


def _ffn_body(chunks, x_ref, g_ref, w1_ref, w3_ref, w2_ref, o_ref):
    x = x_ref[...]
    xn = _rms(x, g_ref[...]).astype(BF16)
    acc = None
    for c0, cw in chunks:
        a = _dot(xn, w1_ref[:, c0:c0 + cw])
        b = _dot(xn, w3_ref[:, c0:c0 + cw])
        t = _dot((a * _sigmoid(a) * b).astype(BF16), w2_ref[c0:c0 + cw, :])
        acc = t if acc is None else acc + t
    o_ref[...] = x + acc


def _ffn_call(h, g, w1, w3, w2, tm, tf):
    dff = w1.shape[1]
    chunks = tuple((c0, min(tf, dff - c0)) for c0 in range(0, dff, tf))
    return pl.pallas_call(
        functools.partial(_ffn_body, chunks),
        grid=(h.shape[0] // tm,),
        in_specs=[pl.BlockSpec((tm, D_MODEL), lambda i: (i, 0)),
                  _const_spec((1, D_MODEL)),
                  _const_spec((D_MODEL, dff), single=True),
                  _const_spec((D_MODEL, dff), single=True),
                  _const_spec((dff, D_MODEL), single=True)],
        out_specs=pl.BlockSpec((tm, D_MODEL), lambda i: (i, 0)),
        out_shape=jax.ShapeDtypeStruct(h.shape, F32),
        input_output_aliases={0: 0},
        compiler_params=pltpu.CompilerParams(dimension_semantics=("arbitrary",),
                                             vmem_limit_bytes=56 * 1024 * 1024),
        name="ffn",
    )(h, g, w1, w3, w2)


def _router_body(x_ref, g_ref, rw_ref, idx_ref, gate_ref):
    xn = _rms(x_ref[...], g_ref[...])
    logits = jnp.dot(xn, rw_ref[...], preferred_element_type=F32, precision=lax.Precision.HIGHEST)
    lane = lax.broadcasted_iota(I32, logits.shape, 1)
    neg = jnp.float32(-jnp.inf)
    logits = jnp.where(lane < N_EXPERTS, logits, neg)
    m1 = jnp.max(logits, axis=-1, keepdims=True)
    i1 = jnp.min(jnp.where(logits == m1, lane, LANES), axis=-1, keepdims=True)
    rest = jnp.where(lane == i1, neg, logits)
    m2 = jnp.max(rest, axis=-1, keepdims=True)
    i2 = jnp.min(jnp.where(rest == m2, lane, LANES), axis=-1, keepdims=True)
    e = jnp.exp(m2 - m1)
    g1 = 1.0 / (1.0 + e)
    g2 = e / (1.0 + e)
    idx_ref[...] = jnp.where(lane == 0, i1, jnp.where(lane == 1, i2, 0))
    gate_ref[...] = jnp.where(lane == 0, g1, jnp.where(lane == 1, g2, 0.0))


def _router_call(h, g, rw_pad, tm):
    n = h.shape[0]
    return pl.pallas_call(
        _router_body,
        grid=(n // tm,),
        in_specs=[pl.BlockSpec((tm, D_MODEL), lambda i: (i, 0)),
                  _const_spec((1, D_MODEL)), _const_spec((D_MODEL, LANES))],
        out_specs=[pl.BlockSpec((tm, LANES), lambda i: (i, 0))] * 2,
        out_shape=[jax.ShapeDtypeStruct((n, LANES), I32), jax.ShapeDtypeStruct((n, LANES), F32)],
        compiler_params=pltpu.CompilerParams(dimension_semantics=("arbitrary",)),
        name="router",
    )(h, g, rw_pad)


def _row_copy_wait(hbm_ref, sem, nrows):
    pltpu.make_async_copy(hbm_ref.at[pl.ds(0, nrows)], hbm_ref.at[pl.ds(0, nrows)], sem).wait()


def _dispatch_body(tm, pos_ref, x_ref, g_ref, xs_in_ref, xs_ref, buf, sem):
    del xs_in_ref
    xn = _rms(x_ref[...], g_ref[...])
    for k in range(ROW_TILE):
        buf[pl.ds(k, tm, stride=ROW_TILE), :] = xn[:, k * LANES:(k + 1) * LANES]

    def issue(r, c):
        src = buf.at[pl.ds(pl.multiple_of(r * ROW_TILE, ROW_TILE), ROW_TILE), :]
        pltpu.make_async_copy(src, xs_ref.at[pos_ref[0, 0, 2 * r]], sem.at[0]).start(priority=0)
        pltpu.make_async_copy(src, xs_ref.at[pos_ref[0, 0, 2 * r + 1]], sem.at[0]).start(priority=1)
        return c

    lax.fori_loop(0, tm, issue, 0)
    _row_copy_wait(xs_ref, sem.at[0], 2 * tm)


def _dispatch_call(h, g, pos, xs_zero, tm):
    n = h.shape[0]
    nt = n // tm
    return pl.pallas_call(
        functools.partial(_dispatch_body, tm),
        grid=(nt,),
        in_specs=[pl.BlockSpec((1, 1, 2 * tm), lambda i: (i, 0, 0), memory_space=pltpu.SMEM),
                  pl.BlockSpec((tm, D_MODEL), lambda i: (i, 0)),
                  _const_spec((1, D_MODEL)),
                  pl.BlockSpec(memory_space=pl.ANY)],
        out_specs=pl.BlockSpec(memory_space=pl.ANY),
        out_shape=jax.ShapeDtypeStruct(xs_zero.shape, F32),
        scratch_shapes=[pltpu.VMEM((tm * ROW_TILE, LANES), F32), pltpu.SemaphoreType.DMA((1,))],
        input_output_aliases={3: 0},
        compiler_params=pltpu.CompilerParams(dimension_semantics=("arbitrary",)),
        name="dispatch",
    )(pos.reshape(nt, 1, 2 * tm), h, g, xs_zero)


def _experts_body(tm, te_ref, tv_ref, xs_ref, w1_ref, w3_ref, w2_ref, y_ref, x_s, acc_s):
    t = pl.program_id(0)
    f = pl.program_id(1)
    valid = tv_ref[t] > 0

    @pl.when(jnp.logical_and(valid, f == 0))
    def _():
        for k in range(ROW_TILE):
            x_s[:, k * LANES:(k + 1) * LANES] = xs_ref[pl.ds(k, tm, stride=ROW_TILE), :].astype(BF16)
        acc_s[...] = jnp.zeros_like(acc_s)

    @pl.when(valid)
    def _():
        x = x_s[...]
        a = _dot(x, w1_ref[0].astype(BF16))
        b = _dot(x, w3_ref[0].astype(BF16))
        acc_s[...] += _dot((a * _sigmoid(a) * b).astype(BF16), w2_ref[0].astype(BF16))

    last = f == pl.num_programs(1) - 1

    @pl.when(jnp.logical_and(valid, last))
    def _():
        acc = acc_s[...]
        for k in range(ROW_TILE):
            y_ref[pl.ds(k, tm, stride=ROW_TILE), :] = acc[:, k * LANES:(k + 1) * LANES]

    @pl.when(jnp.logical_and(jnp.logical_not(valid), last))
    def _():
        y_ref[...] = jnp.zeros_like(y_ref)


def _experts_call(xs2d, tile_expert, tile_valid, w1, w3, w2, tm, tf):
    rows = xs2d.shape[0] // ROW_TILE
    nt = rows // tm
    dff = w1.shape[2]
    grid_spec = pltpu.PrefetchScalarGridSpec(
        num_scalar_prefetch=2,
        grid=(nt, dff // tf),
        in_specs=[pl.BlockSpec((tm * ROW_TILE, LANES), lambda t, f, te, tv: (t, 0)),
                  pl.BlockSpec((1, D_MODEL, tf), lambda t, f, te, tv: (te[t], 0, f)),
                  pl.BlockSpec((1, D_MODEL, tf), lambda t, f, te, tv: (te[t], 0, f)),
                  pl.BlockSpec((1, tf, D_MODEL), lambda t, f, te, tv: (te[t], f, 0))],
        out_specs=pl.BlockSpec((tm * ROW_TILE, LANES), lambda t, f, te, tv: (t, 0)),
        scratch_shapes=[pltpu.VMEM((tm, D_MODEL), BF16), pltpu.VMEM((tm, D_MODEL), F32)])
    return pl.pallas_call(
        functools.partial(_experts_body, tm),
        grid_spec=grid_spec,
        out_shape=jax.ShapeDtypeStruct(xs2d.shape, F32),
        compiler_params=pltpu.CompilerParams(dimension_semantics=("arbitrary", "arbitrary"),
                                             vmem_limit_bytes=56 * 1024 * 1024),
        name="experts",
    )(tile_expert, tile_valid, xs2d, w1, w3, w2)


def _combine_body(tm, final, pos_ref, h_ref, gate_ref, *rest):
    if final:
        ng_ref, y_ref, o_ref, buf, sem = rest
    else:
        y_ref, o_ref, buf, sem = rest

    def issue(r, c):
        for j in range(2):
            dst = buf.at[j, pl.ds(pl.multiple_of(r * ROW_TILE, ROW_TILE), ROW_TILE), :]
            pltpu.make_async_copy(y_ref.at[pos_ref[0, 0, 2 * r + j]], dst, sem.at[0]).start(priority=j)
        return c

    lax.fori_loop(0, tm, issue, 0)
    _row_copy_wait(y_ref, sem.at[0], 2 * tm)
    g0 = gate_ref[:, 0:1]
    g1 = gate_ref[:, 1:2]
    for k in range(ROW_TILE):
        cols = slice(k * LANES, (k + 1) * LANES)
        moe = (g0 * buf[0, pl.ds(k, tm, stride=ROW_TILE), :]
               + g1 * buf[1, pl.ds(k, tm, stride=ROW_TILE), :])
        o_ref[:, cols] = h_ref[:, cols] + moe
    if final:
        o_ref[...] = _rms(o_ref[...], ng_ref[...])


def _combine_call(h, gates, pos, y3, tm, final_g=None):
    n = h.shape[0]
    nt = n // tm
    final = final_g is not None
    in_specs = [pl.BlockSpec((1, 1, 2 * tm), lambda i: (i, 0, 0), memory_space=pltpu.SMEM),
                pl.BlockSpec((tm, D_MODEL), lambda i: (i, 0)),
                pl.BlockSpec((tm, LANES), lambda i: (i, 0))]
    args = [pos.reshape(nt, 1, 2 * tm), h, gates]
    if final:
        in_specs.append(_const_spec((1, D_MODEL)))
        args.append(final_g)
    in_specs.append(pl.BlockSpec(memory_space=pl.ANY))
    args.append(y3)
    return pl.pallas_call(
        functools.partial(_combine_body, tm, final),
        grid=(nt,),
        in_specs=in_specs,
        out_specs=pl.BlockSpec((tm, D_MODEL), lambda i: (i, 0)),
        out_shape=jax.ShapeDtypeStruct((n, D_MODEL), F32),
        scratch_shapes=[pltpu.VMEM((2, tm * ROW_TILE, LANES), F32), pltpu.SemaphoreType.DMA((1,))],
        compiler_params=pltpu.CompilerParams(dimension_semantics=("arbitrary",)),
        name="combine",
    )(*args)


def _to_tm_body(nb, tt, x_ref, tail_ref, o_ref, slab):
    last = pl.program_id(0) == pl.num_programs(0) - 1

    @pl.when(jnp.logical_not(last))
    def _():
        for b in range(nb):
            for k in range(ROW_TILE):
                slab[k, pl.ds(b, tt, stride=nb), :] = x_ref[b, :, k * LANES:(k + 1) * LANES]
        for k in range(ROW_TILE):
            o_ref[:, k * LANES:(k + 1) * LANES] = slab[k]

    @pl.when(last)
    def _():
        o_ref[0:tail_ref.shape[0], :] = tail_ref[...]


def _from_tm_body(nb, tt, x_ref, o_ref, slab):
    for k in range(ROW_TILE):
        slab[k] = x_ref[:, k * LANES:(k + 1) * LANES]
    for b in range(nb):
        for k in range(ROW_TILE):
            o_ref[b, :, k * LANES:(k + 1) * LANES] = slab[k, pl.ds(b, tt, stride=nb), :]


def _to_tm_call(x, tail, tt):
    nb, t, d = x.shape
    ns = tail.shape[0]
    assert ns <= tt * nb
    nsteps = t // tt
    return pl.pallas_call(
        functools.partial(_to_tm_body, nb, tt),
        grid=(nsteps + 1,),
        in_specs=[pl.BlockSpec((nb, tt, d), lambda i: (0, jnp.minimum(i, nsteps - 1), 0)),
                  _const_spec((ns, d))],
        out_specs=pl.BlockSpec((tt * nb, d), lambda i: (i, 0)),
        out_shape=jax.ShapeDtypeStruct((nb * t + ns, d), F32),
        scratch_shapes=[pltpu.VMEM((ROW_TILE, tt * nb, LANES), F32)],
        compiler_params=pltpu.CompilerParams(dimension_semantics=("arbitrary",)),
        name="to_tm",
    )(x, tail)


def _from_tm_call(h, nb, t, tt):
    d = h.shape[1]
    return pl.pallas_call(
        functools.partial(_from_tm_body, nb, tt),
        grid=(t // tt,),
        in_specs=[pl.BlockSpec((tt * nb, d), lambda i: (i, 0))],
        out_specs=pl.BlockSpec((nb, tt, d), lambda i: (0, i, 0)),
        out_shape=jax.ShapeDtypeStruct((nb, t, d), F32),
        scratch_shapes=[pltpu.VMEM((ROW_TILE, tt * nb, LANES), F32)],
        compiler_params=pltpu.CompilerParams(dimension_semantics=("arbitrary",)),
        name="from_tm",
    )(h)


def _rms_body(x_ref, g_ref, o_ref):
    o_ref[...] = _rms(x_ref[...], g_ref[...])


def _rms_call(h, g, tm):
    n = h.shape[0]
    return pl.pallas_call(
        _rms_body, grid=(n // tm,),
        in_specs=[pl.BlockSpec((tm, D_MODEL), lambda i: (i, 0)), _const_spec((1, D_MODEL))],
        out_specs=pl.BlockSpec((tm, D_MODEL), lambda i: (i, 0)),
        out_shape=jax.ShapeDtypeStruct((n, D_MODEL), F32),
        name="rms",
    )(h, g)


MOE_TM = 1024
MOE_TF = 512
TOK_TM = 768
FFN_TF = 512


def _moe_layer(h, norm_g, rw_pad, w1, w3, w2, e0, final_g):
    n = h.shape[0]
    idx, gates = _router_call(h, norm_g, rw_pad, TOK_TM)
    e_flat = idx[:, 0:2].reshape(2 * n)
    onehot = (e_flat[:, None] == jnp.arange(N_EXPERTS, dtype=I32)[None, :]).astype(I32)
    csum = jnp.cumsum(onehot, axis=0)
    rank = jnp.sum((csum - onehot) * onehot, axis=1)
    counts = csum[-1]
    padded = ((counts + MOE_TM - 1) // MOE_TM) * MOE_TM
    ends = jnp.cumsum(padded)
    starts = ends - padded
    pos = (jnp.sum(onehot * starts[None, :], axis=1) + rank).astype(I32)
    n_tiles = (2 * n) // MOE_TM + N_EXPERTS
    tile_start = jnp.arange(n_tiles, dtype=I32) * MOE_TM
    tile_valid = (tile_start < ends[-1]).astype(I32)
    tile_expert = jnp.minimum(jnp.sum((tile_start[:, None] >= ends[None, :]).astype(I32), axis=1),
                              N_EXPERTS - 1).astype(I32)
    last_valid = jnp.max(jnp.where(tile_valid > 0, tile_expert, 0))
    tile_expert = jnp.where(tile_valid > 0, tile_expert, last_valid)
    xs_zero = jnp.zeros((n_tiles * MOE_TM, ROW_TILE, LANES), F32)
    xs = _dispatch_call(h, norm_g, pos, xs_zero, TOK_TM)
    y2d = _experts_call(xs.reshape(n_tiles * MOE_TM * ROW_TILE, LANES), tile_expert + e0, tile_valid,
                        w1, w3, w2, MOE_TM, MOE_TF)
    return _combine_call(h, gates, pos, y2d.reshape(n_tiles * MOE_TM, ROW_TILE, LANES), TOK_TM, final_g)


def _block_diag(w):
    k, a, b = w.shape
    eye = jnp.eye(k, dtype=w.dtype)
    return jnp.einsum('kab,kj->kajb', w, eye).reshape(k * a, k * b)


def kernel(x_prompt, x_sample, state_conv, state_lru, state_s5_re, state_s5_im, state_pool, norm_mix_g, w_in, b_in, gmlp_ln_g, gmlp_ln_b, gmlp_w_s, gmlp_b_s, conv_w, conv_b, lru_w_a, lru_b_a, lru_w_x, lru_b_x, lru_lam, s5_a_re, s5_a_im, s5_log_dt, s5_b_re, s5_b_im, s5_c_re, s5_c_im, s5_d, s5_w_glu, s5_b_glu, pool_w, pool_scale, w_branch, w_out, norm_ffn_g, ffn_w1, ffn_w3, ffn_w2, router_w, moe_w1, moe_w3, moe_w2, norm_final_g):
    depth = w_in.shape[0]
    bp, tp, _ = x_prompt.shape
    bs, ts, _ = x_sample.shape
    n_p, n_s = bp * tp, bs * ts
    assert n_p % (CHUNK * bp) == 0 and n_p % n_s == 0

    h = _to_tm_call(x_prompt, jnp.transpose(x_sample, (1, 0, 2)).reshape(n_s, D_MODEL), CHUNK)

    lbr, lbi, bbr, bbi = _s5prep(s5_a_re, s5_a_im, s5_log_dt, s5_b_re, s5_b_im)
    row = lambda x: x.reshape(1, -1)
    lane_head = jnp.arange(D_BR) // HEAD_DIM

    conv_p, lru_p, sre_p, sim_p, pool_p = [], [], [], [], []
    conv_s, lru_s, sre_s, sim_s, pool_s, v_s = [], [], [], [], [], []
    zeros_p = (jnp.zeros(((CONV_W - 1) * bp, D_BR), F32), jnp.zeros((bp, D_BR), F32),
               jnp.zeros((bp, S5_W), F32), jnp.zeros((bp, S5_W), F32),
               jnp.zeros((POOL_BUF * bp, D_BR), F32))
    for l in range(depth):
        bblk = jnp.concatenate([_block_diag(bbr[l]), _block_diag(bbi[l])], axis=1).astype(BF16)
        cre = _block_diag(jnp.transpose(s5_c_re[l], (0, 2, 1))).astype(BF16)
        cim = _block_diag(jnp.transpose(s5_c_im[l], (0, 2, 1))).astype(BF16)
        common_tail = (
            conv_w[l], row(conv_b[l]),
            _block_diag(lru_w_a[l]).astype(BF16), row(lru_b_a[l]),
            _block_diag(lru_w_x[l]).astype(BF16), row(lru_b_x[l]), row(lru_lam[l]),
            row(lbr[l]), row(lbi[l]), bblk, cre, cim, row(s5_d[l]),
            s5_w_glu[l].astype(BF16), row(s5_b_glu[l]),
            _block_diag(pool_w[l]).astype(BF16), row(pool_scale[l]),
            w_branch[l].reshape(N_BRANCH * D_BR, D_MODEL).astype(BF16), w_out[l].astype(BF16))
        common_head = (row(norm_mix_g[l]), w_in[l].astype(BF16), row(b_in[l]),
                       row(gmlp_ln_g[l]), row(gmlp_ln_b[l]))
        bsm_p = gmlp_b_s[l][lane_head, :].T
        lp_p = common_head + (gmlp_w_s[l], bsm_p) + common_tail
        ws_small = jnp.transpose(gmlp_w_s[l][:, :ts, :ts], (1, 2, 0))[:, :, lane_head].reshape(ts * ts, D_BR)
        bsm_s = gmlp_b_s[l][lane_head, :ts].T
        lp_s = common_head + (ws_small, bsm_s) + common_tail

        outs_p = _mixer_call(h, 0, tp // CHUNK, bp, CHUNK, 0, True, False, zeros_p, lp_p)
        st_s = (jnp.transpose(state_conv[l], (1, 0, 2)).reshape((CONV_W - 1) * bs, D_BR),
                state_lru[l], state_s5_re[l].reshape(bs, S5_W), state_s5_im[l].reshape(bs, S5_W),
                jnp.transpose(state_pool[l], (1, 0, 2)).reshape(POOL_BUF * bs, D_BR))
        outs_s = _mixer_call(outs_p[0], n_p // n_s, 1, bs, ts, PAST_LEN, False, True, st_s, lp_s)
        h = outs_s[0]
        for lst, o in zip((conv_p, lru_p, sre_p, sim_p, pool_p), outs_p[1:6]):
            lst.append(o)
        for lst, o in zip((conv_s, lru_s, sre_s, sim_s, pool_s, v_s), outs_s[1:7]):
            lst.append(o)

        if l % 2 == 0:
            k = l // 2
            h = _ffn_call(h, row(norm_ffn_g[l]), ffn_w1[k].astype(BF16), ffn_w3[k].astype(BF16),
                          ffn_w2[k].astype(BF16), TOK_TM, FFN_TF)
            if l == depth - 1:
                h = _rms_call(h, row(norm_final_g), TOK_TM)
        else:
            k = l // 2
            rw_pad = jnp.pad(router_w[k], ((0, 0), (0, LANES - N_EXPERTS)))
            stk = lambda w: w.reshape((-1,) + w.shape[2:])
            h = _moe_layer(h, row(norm_ffn_g[l]), rw_pad, stk(moe_w1), stk(moe_w3), stk(moe_w2),
                           k * N_EXPERTS, row(norm_final_g) if l == depth - 1 else None)

    def tm_to_bm(x, t, b):
        return jnp.transpose(x.reshape(t, b, x.shape[-1]), (1, 0, 2))

    y_prompt = _from_tm_call(h, bp, tp, CHUNK)
    y_sample = tm_to_bm(h[n_p:], ts, bs)
    stack = lambda lst, f: jnp.stack([f(o) for o in lst])
    return (
        y_prompt, y_sample,
        stack(conv_p, lambda o: tm_to_bm(o, CONV_W - 1, bp)),
        stack(lru_p, lambda o: o),
        stack(sre_p, lambda o: o.reshape(bp, N_GROUPS_C, P_STATE)),
        stack(sim_p, lambda o: o.reshape(bp, N_GROUPS_C, P_STATE)),
        stack(pool_p, lambda o: tm_to_bm(o, POOL_BUF, bp)),
        stack(conv_s, lambda o: tm_to_bm(o, CONV_W - 1, bs)),
        stack(lru_s, lambda o: o),
        stack(sre_s, lambda o: o.reshape(bs, N_GROUPS_C, P_STATE)),
        stack(sim_s, lambda o: o.reshape(bs, N_GROUPS_C, P_STATE)),
        stack(pool_s, lambda o: tm_to_bm(o, POOL_BUF, bs)),
        stack(v_s, lambda o: tm_to_bm(o, ts, bs)),
    )
```

```python
import functools
import math

import jax
import jax.numpy as jnp
from jax import lax
from jax.experimental import pallas as pl
from jax.experimental.pallas import tpu as pltpu

F32 = jnp.float32
BF16 = jnp.bfloat16
I32 = jnp.int32

SUBLANES = 8
LANES = 128

D_MODEL = 1024
D_BR = 256
N_BRANCH = 4
N_HEADS = 4
HEAD_DIM = D_BR // N_HEADS
CHUNK = 128
CONV_W = 4
LRU_C = 8.0
N_GROUPS_C = 16
GROUP_C = 16
P_STATE = 64
S5_W = N_GROUPS_C * P_STATE
POOL_WINDOWS = (2, 4, 8, 16)
POOL_BUF = max(POOL_WINDOWS) - 1
GROUP_D = D_BR // len(POOL_WINDOWS)
COLS_A = 5 * D_BR
IN_COLS = COLS_A + N_BRANCH * D_MODEL
N_EXPERTS = 8
EPS = 1e-6
PAST_LEN = 16384
SQRT_2_OVER_PI = math.sqrt(2.0 / math.pi)
ROW_TILE = D_MODEL // LANES


def _gelu(x):
    return x * (0.5 * (1.0 + jnp.tanh(SQRT_2_OVER_PI * (x + 0.044715 * (x * x * x)))))


def _sigmoid(x):
    return 1.0 / (1.0 + jnp.exp(-x))


def _rms(x, g):
    return x * lax.rsqrt(jnp.mean(x * x, axis=-1, keepdims=True) + EPS) * g


def _dot(a, b):
    return jnp.dot(a, b, preferred_element_type=F32)


def _const_spec(shape, single=False):
    nd = len(shape)
    if single:
        return pl.BlockSpec(shape, lambda *_: (0,) * nd, pipeline_mode=pl.Buffered(1))
    return pl.BlockSpec(shape, lambda *_: (0,) * nd)


def _mixer_body(cfg, *refs):
    B, Tt, start_pos, mm_gmlp, emit_v = cfg
    R = Tt * B
    refs = list(refs)
    h_ref = refs.pop(0)
    (conv0_ref, lru0_ref, sre0_ref, sim0_ref, pool0_ref,
     ng_ref, win_ref, bin_ref, lng_ref, lnb_ref, ws_ref, bsm_ref,
     cw_ref, cb_ref, wa_ref, ba_ref, wx_ref, bx_ref, lam_ref,
     lbr_ref, lbi_ref, bblk_ref, cre_ref, cim_ref, d_ref, wglu_ref, bglu_ref,
     pw_ref, ps_ref, wb_ref, wo_ref) = refs[:31]
    n_out = 7 if emit_v else 6
    outs = refs[31:31 + n_out]
    out_ref, convo_ref, lruo_ref, sreo_ref, simo_ref, poolo_ref = outs[:6]
    (xn_s, cext_s, pext_s, lruh_s, sre_s, sim_s, u_s, v_s, mix_s, xc_s,
     a_s, b_s, yc_s, yd_s, s5_s) = refs[31 + n_out:]

    i = pl.program_id(0)
    SB = min(R, 256)
    SB5 = max(B, min(R, 256))
    lane256 = lax.broadcasted_iota(I32, (1, D_BR), 1)
    ydt = yc_s.dtype

    def mm(a, w):
        return _dot(a.astype(BF16), w)

    def init_carries():
        cext_s[0:(CONV_W - 1) * B, :] = conv0_ref[...]
        pext_s[0:POOL_BUF * B, :] = pool0_ref[...]
        lruh_s[...] = lru0_ref[...]
        sre_s[...] = sre0_ref[...]
        sim_s[...] = sim0_ref[...]

    def blocks(n, size, fn):
        if n == 1:
            fn(0)
        else:
            def body(k, c):
                fn(pl.multiple_of(k * size, size))
                return c
            lax.fori_loop(0, n, body, 0)

    def stage_in(r0):
        rows = pl.ds(r0, SB)
        xn = _rms(h_ref[rows, :], ng_ref[...]).astype(BF16)
        xn_s[rows, :] = xn
        pa = _dot(xn, win_ref[:, 0:COLS_A]) + bin_ref[:, 0:COLS_A]
        u_s[rows, :] = _gelu(pa[:, 0:D_BR])
        gv = _gelu(pa[:, D_BR:2 * D_BR])
        mu = jnp.mean(gv, axis=-1, keepdims=True)
        var = jnp.mean(jnp.square(gv - mu), axis=-1, keepdims=True)
        v = (gv - mu) * lax.rsqrt(var + EPS) * lng_ref[...] + lnb_ref[...]
        v_s[0, rows, :] = v[:, 0:LANES]
        v_s[1, rows, :] = v[:, LANES:2 * LANES]
        cext_s[pl.ds(r0 + (CONV_W - 1) * B, SB), :] = pa[:, 2 * D_BR:3 * D_BR]
        xc_s[rows, :] = pa[:, 3 * D_BR:4 * D_BR]
        pext_s[pl.ds(r0 + POOL_BUF * B, SB), :] = pa[:, 4 * D_BR:5 * D_BR]

    def stage_gmlp():
        if mm_gmlp:
            tril = (lax.broadcasted_iota(I32, (CHUNK, CHUNK), 0)
                    >= lax.broadcasted_iota(I32, (CHUNK, CHUNK), 1))
            wm = [jnp.where(tril, ws_ref[hd], 0.0).astype(BF16) for hd in range(N_HEADS)]
            head = lane256 // HEAD_DIM
            for b in range(B):
                vb = jnp.concatenate([v_s[0, pl.ds(b, Tt, stride=B), :],
                                      v_s[1, pl.ds(b, Tt, stride=B), :]], axis=1).astype(BF16)
                mixed = bsm_ref[...]
                for hd in range(N_HEADS):
                    mixed = mixed + jnp.where(head == hd, mm(wm[hd], vb), 0.0)
                mix_s[0, pl.ds(b, Tt, stride=B), :] = mixed[:, 0:LANES]
                mix_s[1, pl.ds(b, Tt, stride=B), :] = mixed[:, LANES:2 * LANES]
        else:
            for t in range(Tt):
                for half in range(2):
                    lo = half * LANES
                    acc = jnp.broadcast_to(bsm_ref[t:t + 1, lo:lo + LANES], (B, LANES))
                    for s in range(t + 1):
                        w = ws_ref[t * Tt + s:t * Tt + s + 1, lo:lo + LANES]
                        acc = acc + w * v_s[half, s * B:(s + 1) * B, :]
                    mix_s[half, t * B:(t + 1) * B, :] = acc
        if emit_v:
            vo_ref = outs[6]
            vo_ref[:, 0:LANES] = v_s[0]
            vo_ref[:, LANES:2 * LANES] = v_s[1]

    log_sig_lam = (jnp.minimum(lam_ref[...], 0.0)
                   - jnp.log1p(jnp.exp(-jnp.abs(lam_ref[...]))))

    def stage_lru(r0):
        rows = pl.ds(r0, SB)
        conv = cb_ref[...] + cext_s[pl.ds(r0, SB), :] * cw_ref[0:1, :]
        for k in range(1, CONV_W):
            conv = conv + cext_s[pl.ds(r0 + k * B, SB), :] * cw_ref[k:k + 1, :]
        cbf = conv.astype(BF16)
        r = _sigmoid(mm(cbf, wa_ref[...]) + ba_ref[...])
        ig = _sigmoid(mm(cbf, wx_ref[...]) + bx_ref[...])
        a = jnp.exp(LRU_C * r * log_sig_lam)
        mult = jnp.sqrt(1.0 - a * a)
        if start_pos == 0:
            row = lax.broadcasted_iota(I32, (SB, D_BR), 0) + r0
            mult = jnp.where(jnp.logical_and(i == 0, row < B), 1.0, mult)
        a_s[rows, :] = a
        b_s[rows, :] = mult * ig * conv

    def lru_step(t, hprev):
        rows = pl.ds(t * B, B) if isinstance(t, int) else pl.ds(pl.multiple_of(t * B, B), B)
        hnew = a_s[rows, :] * hprev + b_s[rows, :]
        b_s[rows, :] = hnew
        return hnew

    def lru_scan():
        if Tt <= 8:
            hl = lruh_s[...]
            for t in range(Tt):
                hl = lru_step(t, hl)
        else:
            hl = lax.fori_loop(0, Tt, lru_step, lruh_s[...], unroll=8)
        lruh_s[...] = hl
        lruo_ref[...] = hl
        conv_tail = cext_s[R:R + (CONV_W - 1) * B, :]
        convo_ref[...] = conv_tail
        cext_s[0:(CONV_W - 1) * B, :] = conv_tail

    lbr = jnp.broadcast_to(lbr_ref[...], (B, S5_W))
    lbi = jnp.broadcast_to(lbi_ref[...], (B, S5_W))
    steps5 = SB5 // B

    def stage_s5(r0):
        rows = pl.ds(r0, SB5)
        xc = xc_s[rows, :]
        s5_s[...] = mm(xc, bblk_ref[...])

        def s5_step(t, carry):
            sre, sim = carry
            rr = pl.ds(t * B, B) if isinstance(t, int) else pl.ds(pl.multiple_of(t * B, B), B)
            nre = lbr * sre - lbi * sim + s5_s[rr, 0:S5_W]
            nim = lbr * sim + lbi * sre + s5_s[rr, S5_W:2 * S5_W]
            s5_s[rr, 0:S5_W] = nre
            s5_s[rr, S5_W:2 * S5_W] = nim
            return nre, nim

        carry = (sre_s[...], sim_s[...])
        for t in range(steps5):
            carry = s5_step(t, carry)
        sre_s[...] = carry[0]
        sim_s[...] = carry[1]
        y = (mm(s5_s[:, 0:S5_W], cre_ref[...])
             - mm(s5_s[:, S5_W:2 * S5_W], cim_ref[...])
             + d_ref[...] * xc)
        y = _gelu(y)
        y = y * _sigmoid(mm(y, wglu_ref[...]) + bglu_ref[...])
        yc_s[rows, :] = y.astype(ydt)

    wlane = jnp.where(lane256 < GROUP_D, POOL_WINDOWS[0],
                      jnp.where(lane256 < 2 * GROUP_D, POOL_WINDOWS[1],
                                jnp.where(lane256 < 3 * GROUP_D, POOL_WINDOWS[2], POOL_WINDOWS[3])))

    def stage_pool(r0):
        base = r0 + POOL_BUF * B
        tok = pext_s[pl.ds(base, SB), :]
        acc = tok
        sums = {}
        for j in range(1, max(POOL_WINDOWS)):
            acc = acc + pext_s[pl.ds(base - j * B, SB), :]
            if j + 1 in POOL_WINDOWS:
                sums[j + 1] = acc
        sel = jnp.where(lane256 < GROUP_D, sums[2],
                        jnp.where(lane256 < 2 * GROUP_D, sums[4],
                                  jnp.where(lane256 < 3 * GROUP_D, sums[8], sums[16])))
        if start_pos >= POOL_BUF:
            cnt = wlane.astype(F32)
        else:
            row = lax.broadcasted_iota(I32, (SB, D_BR), 0) + r0
            tpos = start_pos + i * Tt + lax.shift_right_logical(row, jnp.full_like(row, int(math.log2(B))))
            cnt = jnp.minimum(wlane, tpos + 1).astype(F32)
        diff = sel / cnt - tok
        yd_s[pl.ds(r0, SB), :] = (mm(diff, pw_ref[...]) * ps_ref[...]).astype(ydt)

    def pool_tail():
        tail = pext_s[R:R + POOL_BUF * B, :]
        poolo_ref[...] = tail
        pext_s[0:POOL_BUF * B, :] = tail

    SBM = min(R, 128)

    def stage_merge(r0):
        rows = pl.ds(r0, SBM)
        xn = xn_s[rows, :]
        ya = u_s[rows, :] * jnp.concatenate([mix_s[0, rows, :], mix_s[1, rows, :]], axis=1)
        ys = (ya, b_s[rows, :], yc_s[rows, :], yd_s[rows, :])
        merged = None
        for n in range(N_BRANCH):
            c0 = COLS_A + n * D_MODEL
            logits = _dot(xn, win_ref[:, c0:c0 + D_MODEL]) + bin_ref[:, c0:c0 + D_MODEL]
            term = _sigmoid(logits) * mm(ys[n], wb_ref[n * D_BR:(n + 1) * D_BR, :])
            merged = term if merged is None else merged + term
        out_ref[rows, :] = h_ref[rows, :] + mm(merged, wo_ref[...])

    def stage_front(r0):
        stage_in(r0)
        stage_lru(r0)
        stage_pool(r0)

    def sequence_stages():
        blocks(R // SB, SB, stage_front)
        stage_gmlp()
        lru_scan()
        pool_tail()

    def merge_rows(r0):
        for j in range(SB5 // SBM):
            rj = r0 + j * SBM
            stage_merge(rj if isinstance(rj, int) else pl.multiple_of(rj, SBM))

    nb5 = R // SB5
    pl.when(i == 0)(init_carries)
    sequence_stages()
    stage_s5(0)
    if nb5 > 1:
        def piped(k, c):
            r0 = pl.multiple_of(k * SB5, SB5)
            stage_s5(r0)
            merge_rows(r0 - SB5)
            return c
        lax.fori_loop(1, nb5, piped, 0)
    merge_rows((nb5 - 1) * SB5)
    sreo_ref[...] = sre_s[...]
    simo_ref[...] = sim_s[...]


def _mixer_call(h_all, blk0, nsteps, B, Tt, start_pos, mm_gmlp, emit_v, states, lp):
    R = Tt * B
    cfg = (B, Tt, start_pos, mm_gmlp, emit_v)
    conv0, lru0, sre0, sim0, pool0 = states
    small = [conv0, lru0, sre0, sim0, pool0]
    params = list(lp)
    big = {1, 16, 24, 25}
    h_spec = pl.BlockSpec((R, D_MODEL), lambda i: (blk0 + i, 0))
    in_specs = [h_spec]
    in_specs += [_const_spec(a.shape) for a in small]
    in_specs += [_const_spec(a.shape, single=(k in big)) for k, a in enumerate(params)]
    out_shape = [jax.ShapeDtypeStruct(h_all.shape, F32),
                 jax.ShapeDtypeStruct(conv0.shape, F32), jax.ShapeDtypeStruct(lru0.shape, F32),
                 jax.ShapeDtypeStruct(sre0.shape, F32), jax.ShapeDtypeStruct(sim0.shape, F32),
                 jax.ShapeDtypeStruct(pool0.shape, F32)]
    out_specs = [h_spec] + [_const_spec(s.shape) for s in out_shape[1:]]
    if emit_v:
        out_shape.append(jax.ShapeDtypeStruct((R, D_BR), F32))
        out_specs.append(_const_spec((R, D_BR)))
    SB5 = max(B, min(R, 256))
    scratch = [
        pltpu.VMEM((R, D_MODEL), BF16),
        pltpu.VMEM((R + (CONV_W - 1) * B, D_BR), F32),
        pltpu.VMEM((R + POOL_BUF * B, D_BR), F32),
        pltpu.VMEM((B, D_BR), F32),
        pltpu.VMEM((B, S5_W), F32), pltpu.VMEM((B, S5_W), F32),
        pltpu.VMEM((R, D_BR), F32),
        pltpu.VMEM((2, R, LANES), F32),
        pltpu.VMEM((2, R, LANES), F32),
        pltpu.VMEM((R, D_BR), F32),
        pltpu.VMEM((R, D_BR), F32), pltpu.VMEM((R, D_BR), F32),
        pltpu.VMEM((R, D_BR), BF16), pltpu.VMEM((R, D_BR), BF16),
        pltpu.VMEM((SB5, 2 * S5_W), F32),
    ]
    return pl.pallas_call(
        functools.partial(_mixer_body, cfg),
        grid=(nsteps,),
        in_specs=in_specs,
        out_specs=out_specs,
        out_shape=out_shape,
        scratch_shapes=scratch,
        input_output_aliases={0: 0},
        compiler_params=pltpu.CompilerParams(dimension_semantics=("arbitrary",),
                                             vmem_limit_bytes=60 * 1024 * 1024),
        name="mixer_b%d" % B,
    )(h_all, *small, *params)


def _s5prep_body(are_ref, aim_ref, ldt_ref, bre_ref, bim_ref, lbr_ref, lbi_ref, bbr_ref, bbi_ref):
    a_re, a_im = are_ref[...], aim_ref[...]
    dt = jnp.exp(ldt_ref[...])
    mag = jnp.exp(a_re * dt)
    lb_re = mag * jnp.cos(a_im * dt)
    lb_im = mag * jnp.sin(a_im * dt)
    den = a_re * a_re + a_im * a_im
    n_re = lb_re - 1.0
    q_re = (n_re * a_re + lb_im * a_im) / den
    q_im = (lb_im * a_re - n_re * a_im) / den
    lbr_ref[...] = lb_re
    lbi_ref[...] = lb_im
    bbr_ref[...] = q_re * bre_ref[...] - q_im * bim_ref[...]
    bbi_ref[...] = q_re * bim_ref[...] + q_im * bre_ref[...]


def _s5prep(a_re, a_im, log_dt, b_re, b_im):
    depth = a_re.shape[0]
    rows = depth * N_GROUPS_C * GROUP_C

    def rep(x):
        return jnp.broadcast_to(x[:, :, None, :], (depth, N_GROUPS_C, GROUP_C, P_STATE)).reshape(rows, P_STATE)

    ldt = jnp.broadcast_to(log_dt[:, :, None, None], (depth, N_GROUPS_C, GROUP_C, P_STATE)).reshape(rows, P_STATE)
    b_re_t = jnp.transpose(b_re, (0, 1, 3, 2)).reshape(rows, P_STATE)
    b_im_t = jnp.transpose(b_im, (0, 1, 3, 2)).reshape(rows, P_STATE)
    shp = jax.ShapeDtypeStruct((rows, P_STATE), F32)
    lbr, lbi, bbr, bbi = pl.pallas_call(_s5prep_body, out_shape=[shp] * 4, name="s5prep")(
        rep(a_re), rep(a_im), ldt, b_re_t, b_im_t)
    r4 = lambda x: x.reshape(depth, N_GROUPS_C, GROUP_C, P_STATE)
    return r4(lbr)[:, :, 0, :], r4(lbi)[:, :, 0, :], r4(bbr), r4(bbi)


def _ffn_body(chunks, x_ref, g_ref, w1_ref, w3_ref, w2_ref, o_ref):
    x = x_ref[...]
    xn = _rms(x, g_ref[...]).astype(BF16)
    acc = None
    for c0, cw in chunks:
        a = _dot(xn, w1_ref[:, c0:c0 + cw])
        b = _dot(xn, w3_ref[:, c0:c0 + cw])
        t = _dot((a * _sigmoid(a) * b).astype(BF16), w2_ref[c0:c0 + cw, :])
        acc = t if acc is None else acc + t
    o_ref[...] = x + acc


def _ffn_call(h, g, w1, w3, w2, tm, tf):
    dff = w1.shape[1]
    chunks = tuple((c0, min(tf, dff - c0)) for c0 in range(0, dff, tf))
    return pl.pallas_call(
        functools.partial(_ffn_body, chunks),
        grid=(h.shape[0] // tm,),
        in_specs=[pl.BlockSpec((tm, D_MODEL), lambda i: (i, 0)),
                  _const_spec((1, D_MODEL)),
                  _const_spec((D_MODEL, dff), single=True),
                  _const_spec((D_MODEL, dff), single=True),
                  _const_spec((dff, D_MODEL), single=True)],
        out_specs=pl.BlockSpec((tm, D_MODEL), lambda i: (i, 0)),
        out_shape=jax.ShapeDtypeStruct(h.shape, F32),
        input_output_aliases={0: 0},
        compiler_params=pltpu.CompilerParams(dimension_semantics=("arbitrary",),
                                             vmem_limit_bytes=56 * 1024 * 1024),
        name="ffn",
    )(h, g, w1, w3, w2)


def _router_body(x_ref, g_ref, rw_ref, idx_ref, gate_ref):
    xn = _rms(x_ref[...], g_ref[...])
    logits = jnp.dot(xn, rw_ref[...], preferred_element_type=F32, precision=lax.Precision.HIGHEST)
    lane = lax.broadcasted_iota(I32, logits.shape, 1)
    neg = jnp.float32(-jnp.inf)
    logits = jnp.where(lane < N_EXPERTS, logits, neg)
    m1 = jnp.max(logits, axis=-1, keepdims=True)
    i1 = jnp.min(jnp.where(logits == m1, lane, LANES), axis=-1, keepdims=True)
    rest = jnp.where(lane == i1, neg, logits)
    m2 = jnp.max(rest, axis=-1, keepdims=True)
    i2 = jnp.min(jnp.where(rest == m2, lane, LANES), axis=-1, keepdims=True)
    e = jnp.exp(m2 - m1)
    g1 = 1.0 / (1.0 + e)
    g2 = e / (1.0 + e)
    idx_ref[...] = jnp.where(lane == 0, i1, jnp.where(lane == 1, i2, 0))
    gate_ref[...] = jnp.where(lane == 0, g1, jnp.where(lane == 1, g2, 0.0))


def _router_call(h, g, rw_pad, tm):
    n = h.shape[0]
    return pl.pallas_call(
        _router_body,
        grid=(n // tm,),
        in_specs=[pl.BlockSpec((tm, D_MODEL), lambda i: (i, 0)),
                  _const_spec((1, D_MODEL)), _const_spec((D_MODEL, LANES))],
        out_specs=[pl.BlockSpec((tm, LANES), lambda i: (i, 0))] * 2,
        out_shape=[jax.ShapeDtypeStruct((n, LANES), I32), jax.ShapeDtypeStruct((n, LANES), F32)],
        compiler_params=pltpu.CompilerParams(dimension_semantics=("arbitrary",)),
        name="router",
    )(h, g, rw_pad)


def _row_copy_wait(hbm_ref, sem, nrows):
    pltpu.make_async_copy(hbm_ref.at[pl.ds(0, nrows)], hbm_ref.at[pl.ds(0, nrows)], sem).wait()


def _dispatch_body(tm, pos_ref, x_ref, g_ref, xs_in_ref, xs_ref, buf, sem):
    del xs_in_ref
    xn = _rms(x_ref[...], g_ref[...])
    for k in range(ROW_TILE):
        buf[pl.ds(k, tm, stride=ROW_TILE), :] = xn[:, k * LANES:(k + 1) * LANES]

    def issue(r, c):
        src = buf.at[pl.ds(pl.multiple_of(r * ROW_TILE, ROW_TILE), ROW_TILE), :]
        pltpu.make_async_copy(src, xs_ref.at[pos_ref[0, 0, 2 * r]], sem.at[0]).start(priority=0)
        pltpu.make_async_copy(src, xs_ref.at[pos_ref[0, 0, 2 * r + 1]], sem.at[0]).start(priority=1)
        return c

    lax.fori_loop(0, tm, issue, 0)
    _row_copy_wait(xs_ref, sem.at[0], 2 * tm)


def _dispatch_call(h, g, pos, xs_zero, tm):
    n = h.shape[0]
    nt = n // tm
    return pl.pallas_call(
        functools.partial(_dispatch_body, tm),
        grid=(nt,),
        in_specs=[pl.BlockSpec((1, 1, 2 * tm), lambda i: (i, 0, 0), memory_space=pltpu.SMEM),
                  pl.BlockSpec((tm, D_MODEL), lambda i: (i, 0)),
                  _const_spec((1, D_MODEL)),
                  pl.BlockSpec(memory_space=pl.ANY)],
        out_specs=pl.BlockSpec(memory_space=pl.ANY),
        out_shape=jax.ShapeDtypeStruct(xs_zero.shape, F32),
        scratch_shapes=[pltpu.VMEM((tm * ROW_TILE, LANES), F32), pltpu.SemaphoreType.DMA((1,))],
        input_output_aliases={3: 0},
        compiler_params=pltpu.CompilerParams(dimension_semantics=("arbitrary",)),
        name="dispatch",
    )(pos.reshape(nt, 1, 2 * tm), h, g, xs_zero)


def _experts_body(tm, te_ref, tv_ref, tfirst_ref, xs_ref, w1_ref, w3_ref, w2_ref, y_ref,
                  x_s, acc_s, c1_s, c3_s, c2_s):
    del te_ref
    t = pl.program_id(0)
    f = pl.program_id(1)
    valid = tv_ref[t] > 0

    @pl.when(tfirst_ref[t] > 0)
    def _():
        c1_s[f] = w1_ref[0].astype(BF16)
        c3_s[f] = w3_ref[0].astype(BF16)
        c2_s[f] = w2_ref[0].astype(BF16)

    @pl.when(jnp.logical_and(valid, f == 0))
    def _():
        for k in range(ROW_TILE):
            x_s[:, k * LANES:(k + 1) * LANES] = xs_ref[pl.ds(k, tm, stride=ROW_TILE), :].astype(BF16)
        acc_s[...] = jnp.zeros_like(acc_s)

    @pl.when(valid)
    def _():
        x = x_s[...]
        a = _dot(x, c1_s[f])
        b = _dot(x, c3_s[f])
        acc_s[...] += _dot((a * _sigmoid(a) * b).astype(BF16), c2_s[f])

    last = f == pl.num_programs(1) - 1

    @pl.when(jnp.logical_and(valid, last))
    def _():
        acc = acc_s[...]
        for k in range(ROW_TILE):
            y_ref[pl.ds(k, tm, stride=ROW_TILE), :] = acc[:, k * LANES:(k + 1) * LANES]

    @pl.when(jnp.logical_and(jnp.logical_not(valid), last))
    def _():
        y_ref[...] = jnp.zeros_like(y_ref)


def _experts_call(xs2d, tile_expert, tile_valid, tile_first, w1, w3, w2, tm, tf):
    rows = xs2d.shape[0] // ROW_TILE
    nt = rows // tm
    dff = w1.shape[2]
    nf = dff // tf

    def fb(t, f, tfirst):
        return jnp.where(tfirst[t] > 0, f, nf - 1)

    grid_spec = pltpu.PrefetchScalarGridSpec(
        num_scalar_prefetch=3,
        grid=(nt, nf),
        in_specs=[pl.BlockSpec((tm * ROW_TILE, LANES), lambda t, f, te, tv, t1: (t, 0)),
                  pl.BlockSpec((1, D_MODEL, tf), lambda t, f, te, tv, t1: (te[t], 0, fb(t, f, t1))),
                  pl.BlockSpec((1, D_MODEL, tf), lambda t, f, te, tv, t1: (te[t], 0, fb(t, f, t1))),
                  pl.BlockSpec((1, tf, D_MODEL), lambda t, f, te, tv, t1: (te[t], fb(t, f, t1), 0))],
        out_specs=pl.BlockSpec((tm * ROW_TILE, LANES), lambda t, f, te, tv, t1: (t, 0)),
        scratch_shapes=[pltpu.VMEM((tm, D_MODEL), BF16), pltpu.VMEM((tm, D_MODEL), F32),
                        pltpu.VMEM((nf, D_MODEL, tf), BF16), pltpu.VMEM((nf, D_MODEL, tf), BF16),
                        pltpu.VMEM((nf, tf, D_MODEL), BF16)])
    return pl.pallas_call(
        functools.partial(_experts_body, tm),
        grid_spec=grid_spec,
        out_shape=jax.ShapeDtypeStruct(xs2d.shape, F32),
        compiler_params=pltpu.CompilerParams(dimension_semantics=("arbitrary", "arbitrary"),
                                             vmem_limit_bytes=56 * 1024 * 1024),
        name="experts",
    )(tile_expert, tile_valid, tile_first, xs2d, w1, w3, w2)


def _combine_body(tm, final, pos_ref, h_ref, gate_ref, *rest):
    if final:
        ng_ref, y_ref, o_ref, buf, sem = rest
    else:
        y_ref, o_ref, buf, sem = rest

    def issue(r, c):
        for j in range(2):
            dst = buf.at[j, pl.ds(pl.multiple_of(r * ROW_TILE, ROW_TILE), ROW_TILE), :]
            pltpu.make_async_copy(y_ref.at[pos_ref[0, 0, 2 * r + j]], dst, sem.at[0]).start(priority=j)
        return c

    lax.fori_loop(0, tm, issue, 0)
    _row_copy_wait(y_ref, sem.at[0], 2 * tm)
    g0 = gate_ref[:, 0:1]
    g1 = gate_ref[:, 1:2]
    for k in range(ROW_TILE):
        cols = slice(k * LANES, (k + 1) * LANES)
        moe = (g0 * buf[0, pl.ds(k, tm, stride=ROW_TILE), :]
               + g1 * buf[1, pl.ds(k, tm, stride=ROW_TILE), :])
        o_ref[:, cols] = h_ref[:, cols] + moe
    if final:
        o_ref[...] = _rms(o_ref[...], ng_ref[...])


def _combine_call(h, gates, pos, y3, tm, final_g=None):
    n = h.shape[0]
    nt = n // tm
    final = final_g is not None
    in_specs = [pl.BlockSpec((1, 1, 2 * tm), lambda i: (i, 0, 0), memory_space=pltpu.SMEM),
                pl.BlockSpec((tm, D_MODEL), lambda i: (i, 0)),
                pl.BlockSpec((tm, LANES), lambda i: (i, 0))]
    args = [pos.reshape(nt, 1, 2 * tm), h, gates]
    if final:
        in_specs.append(_const_spec((1, D_MODEL)))
        args.append(final_g)
    in_specs.append(pl.BlockSpec(memory_space=pl.ANY))
    args.append(y3)
    return pl.pallas_call(
        functools.partial(_combine_body, tm, final),
        grid=(nt,),
        in_specs=in_specs,
        out_specs=pl.BlockSpec((tm, D_MODEL), lambda i: (i, 0)),
        out_shape=jax.ShapeDtypeStruct((n, D_MODEL), F32),
        scratch_shapes=[pltpu.VMEM((2, tm * ROW_TILE, LANES), F32), pltpu.SemaphoreType.DMA((1,))],
        compiler_params=pltpu.CompilerParams(dimension_semantics=("arbitrary",)),
        name="combine",
    )(*args)


def _to_tm_body(nb, tt, x_ref, tail_ref, o_ref, slab):
    last = pl.program_id(0) == pl.num_programs(0) - 1

    @pl.when(jnp.logical_not(last))
    def _():
        for b in range(nb):
            for k in range(ROW_TILE):
                slab[k, pl.ds(b, tt, stride=nb), :] = x_ref[b, :, k * LANES:(k + 1) * LANES]
        for k in range(ROW_TILE):
            o_ref[:, k * LANES:(k + 1) * LANES] = slab[k]

    @pl.when(last)
    def _():
        o_ref[0:tail_ref.shape[0], :] = tail_ref[...]


def _from_tm_body(nb, tt, x_ref, o_ref, slab):
    for k in range(ROW_TILE):
        slab[k] = x_ref[:, k * LANES:(k + 1) * LANES]
    for b in range(nb):
        for k in range(ROW_TILE):
            o_ref[b, :, k * LANES:(k + 1) * LANES] = slab[k, pl.ds(b, tt, stride=nb), :]


def _to_tm_call(x, tail, tt):
    nb, t, d = x.shape
    ns = tail.shape[0]
    assert ns <= tt * nb
    nsteps = t // tt
    return pl.pallas_call(
        functools.partial(_to_tm_body, nb, tt),
        grid=(nsteps + 1,),
        in_specs=[pl.BlockSpec((nb, tt, d), lambda i: (0, jnp.minimum(i, nsteps - 1), 0)),
                  _const_spec((ns, d))],
        out_specs=pl.BlockSpec((tt * nb, d), lambda i: (i, 0)),
        out_shape=jax.ShapeDtypeStruct((nb * t + ns, d), F32),
        scratch_shapes=[pltpu.VMEM((ROW_TILE, tt * nb, LANES), F32)],
        compiler_params=pltpu.CompilerParams(dimension_semantics=("arbitrary",)),
        name="to_tm",
    )(x, tail)


def _from_tm_call(h, nb, t, tt):
    d = h.shape[1]
    return pl.pallas_call(
        functools.partial(_from_tm_body, nb, tt),
        grid=(t // tt,),
        in_specs=[pl.BlockSpec((tt * nb, d), lambda i: (i, 0))],
        out_specs=pl.BlockSpec((nb, tt, d), lambda i: (0, i, 0)),
        out_shape=jax.ShapeDtypeStruct((nb, t, d), F32),
        scratch_shapes=[pltpu.VMEM((ROW_TILE, tt * nb, LANES), F32)],
        compiler_params=pltpu.CompilerParams(dimension_semantics=("arbitrary",)),
        name="from_tm",
    )(h)


def _rms_body(x_ref, g_ref, o_ref):
    o_ref[...] = _rms(x_ref[...], g_ref[...])


def _rms_call(h, g, tm):
    n = h.shape[0]
    return pl.pallas_call(
        _rms_body, grid=(n // tm,),
        in_specs=[pl.BlockSpec((tm, D_MODEL), lambda i: (i, 0)), _const_spec((1, D_MODEL))],
        out_specs=pl.BlockSpec((tm, D_MODEL), lambda i: (i, 0)),
        out_shape=jax.ShapeDtypeStruct((n, D_MODEL), F32),
        name="rms",
    )(h, g)


MOE_TM = 512
MOE_TF = 512
TOK_TM = 768
FFN_TF = 512


def _moe_layer(h, norm_g, rw_pad, w1, w3, w2, e0, final_g):
    n = h.shape[0]
    idx, gates = _router_call(h, norm_g, rw_pad, TOK_TM)
    e_flat = idx[:, 0:2].reshape(2 * n)
    onehot = (e_flat[:, None] == jnp.arange(N_EXPERTS, dtype=I32)[None, :]).astype(I32)
    csum = jnp.cumsum(onehot, axis=0)
    rank = jnp.sum((csum - onehot) * onehot, axis=1)
    counts = csum[-1]
    padded = ((counts + MOE_TM - 1) // MOE_TM) * MOE_TM
    ends = jnp.cumsum(padded)
    starts = ends - padded
    pos = (jnp.sum(onehot * starts[None, :], axis=1) + rank).astype(I32)
    n_tiles = (2 * n) // MOE_TM + N_EXPERTS
    tile_start = jnp.arange(n_tiles, dtype=I32) * MOE_TM
    tile_valid = (tile_start < ends[-1]).astype(I32)
    tile_expert = jnp.minimum(jnp.sum((tile_start[:, None] >= ends[None, :]).astype(I32), axis=1),
                              N_EXPERTS - 1).astype(I32)
    last_valid = jnp.max(jnp.where(tile_valid > 0, tile_expert, 0))
    tile_expert = jnp.where(tile_valid > 0, tile_expert, last_valid)
    xs_zero = jnp.zeros((n_tiles * MOE_TM, ROW_TILE, LANES), F32)
    xs = _dispatch_call(h, norm_g, pos, xs_zero, TOK_TM)
    prev_expert = jnp.concatenate([jnp.full((1,), -1, I32), tile_expert[:-1]])
    tile_first = jnp.logical_and(tile_expert != prev_expert, tile_valid > 0).astype(I32)
    y2d = _experts_call(xs.reshape(n_tiles * MOE_TM * ROW_TILE, LANES), tile_expert + e0, tile_valid, tile_first,
                        w1, w3, w2, MOE_TM, MOE_TF)
    return _combine_call(h, gates, pos, y2d.reshape(n_tiles * MOE_TM, ROW_TILE, LANES), TOK_TM, final_g)


def _block_diag(w):
    k, a, b = w.shape
    eye = jnp.eye(k, dtype=w.dtype)
    return jnp.einsum('kab,kj->kajb', w, eye).reshape(k * a, k * b)


def kernel(x_prompt, x_sample, state_conv, state_lru, state_s5_re, state_s5_im, state_pool, norm_mix_g, w_in, b_in, gmlp_ln_g, gmlp_ln_b, gmlp_w_s, gmlp_b_s, conv_w, conv_b, lru_w_a, lru_b_a, lru_w_x, lru_b_x, lru_lam, s5_a_re, s5_a_im, s5_log_dt, s5_b_re, s5_b_im, s5_c_re, s5_c_im, s5_d, s5_w_glu, s5_b_glu, pool_w, pool_scale, w_branch, w_out, norm_ffn_g, ffn_w1, ffn_w3, ffn_w2, router_w, moe_w1, moe_w3, moe_w2, norm_final_g):
    depth = w_in.shape[0]
    bp, tp, _ = x_prompt.shape
    bs, ts, _ = x_sample.shape
    n_p, n_s = bp * tp, bs * ts
    assert n_p % (CHUNK * bp) == 0 and n_p % n_s == 0

    h = _to_tm_call(x_prompt, jnp.transpose(x_sample, (1, 0, 2)).reshape(n_s, D_MODEL), CHUNK)

    lbr, lbi, bbr, bbi = _s5prep(s5_a_re, s5_a_im, s5_log_dt, s5_b_re, s5_b_im)
    row = lambda x: x.reshape(1, -1)
    lane_head = jnp.arange(D_BR) // HEAD_DIM

    conv_p, lru_p, sre_p, sim_p, pool_p = [], [], [], [], []
    conv_s, lru_s, sre_s, sim_s, pool_s, v_s = [], [], [], [], [], []
    zeros_p = (jnp.zeros(((CONV_W - 1) * bp, D_BR), F32), jnp.zeros((bp, D_BR), F32),
               jnp.zeros((bp, S5_W), F32), jnp.zeros((bp, S5_W), F32),
               jnp.zeros((POOL_BUF * bp, D_BR), F32))
    for l in range(depth):
        bblk = jnp.concatenate([_block_diag(bbr[l]), _block_diag(bbi[l])], axis=1).astype(BF16)
        cre = _block_diag(jnp.transpose(s5_c_re[l], (0, 2, 1))).astype(BF16)
        cim = _block_diag(jnp.transpose(s5_c_im[l], (0, 2, 1))).astype(BF16)
        common_tail = (
            conv_w[l], row(conv_b[l]),
            _block_diag(lru_w_a[l]).astype(BF16), row(lru_b_a[l]),
            _block_diag(lru_w_x[l]).astype(BF16), row(lru_b_x[l]), row(lru_lam[l]),
            row(lbr[l]), row(lbi[l]), bblk, cre, cim, row(s5_d[l]),
            s5_w_glu[l].astype(BF16), row(s5_b_glu[l]),
            _block_diag(pool_w[l]).astype(BF16), row(pool_scale[l]),
            w_branch[l].reshape(N_BRANCH * D_BR, D_MODEL).astype(BF16), w_out[l].astype(BF16))
        common_head = (row(norm_mix_g[l]), w_in[l].astype(BF16), row(b_in[l]),
                       row(gmlp_ln_g[l]), row(gmlp_ln_b[l]))
        bsm_p = gmlp_b_s[l][lane_head, :].T
        lp_p = common_head + (gmlp_w_s[l], bsm_p) + common_tail
        ws_small = jnp.transpose(gmlp_w_s[l][:, :ts, :ts], (1, 2, 0))[:, :, lane_head].reshape(ts * ts, D_BR)
        bsm_s = gmlp_b_s[l][lane_head, :ts].T
        lp_s = common_head + (ws_small, bsm_s) + common_tail

        outs_p = _mixer_call(h, 0, tp // CHUNK, bp, CHUNK, 0, True, False, zeros_p, lp_p)
        st_s = (jnp.transpose(state_conv[l], (1, 0, 2)).reshape((CONV_W - 1) * bs, D_BR),
                state_lru[l], state_s5_re[l].reshape(bs, S5_W), state_s5_im[l].reshape(bs, S5_W),
                jnp.transpose(state_pool[l], (1, 0, 2)).reshape(POOL_BUF * bs, D_BR))
        outs_s = _mixer_call(outs_p[0], n_p // n_s, 1, bs, ts, PAST_LEN, False, True, st_s, lp_s)
        h = outs_s[0]
        for lst, o in zip((conv_p, lru_p, sre_p, sim_p, pool_p), outs_p[1:6]):
            lst.append(o)
        for lst, o in zip((conv_s, lru_s, sre_s, sim_s, pool_s, v_s), outs_s[1:7]):
            lst.append(o)

        if l % 2 == 0:
            k = l // 2
            h = _ffn_call(h, row(norm_ffn_g[l]), ffn_w1[k].astype(BF16), ffn_w3[k].astype(BF16),
                          ffn_w2[k].astype(BF16), TOK_TM, FFN_TF)
            if l == depth - 1:
                h = _rms_call(h, row(norm_final_g), TOK_TM)
        else:
            k = l // 2
            rw_pad = jnp.pad(router_w[k], ((0, 0), (0, LANES - N_EXPERTS)))
            stk = lambda w: w.reshape((-1,) + w.shape[2:])
            h = _moe_layer(h, row(norm_ffn_g[l]), rw_pad, stk(moe_w1), stk(moe_w3), stk(moe_w2),
                           k * N_EXPERTS, row(norm_final_g) if l == depth - 1 else None)

    def tm_to_bm(x, t, b):
        return jnp.transpose(x.reshape(t, b, x.shape[-1]), (1, 0, 2))

    y_prompt = _from_tm_call(h, bp, tp, CHUNK)
    y_sample = tm_to_bm(h[n_p:], ts, bs)
    stack = lambda lst, f: jnp.stack([f(o) for o in lst])
    return (
        y_prompt, y_sample,
        stack(conv_p, lambda o: tm_to_bm(o, CONV_W - 1, bp)),
        stack(lru_p, lambda o: o),
        stack(sre_p, lambda o: o.reshape(bp, N_GROUPS_C, P_STATE)),
        stack(sim_p, lambda o: o.reshape(bp, N_GROUPS_C, P_STATE)),
        stack(pool_p, lambda o: tm_to_bm(o, POOL_BUF, bp)),
        stack(conv_s, lambda o: tm_to_bm(o, CONV_W - 1, bs)),
        stack(lru_s, lambda o: o),
        stack(sre_s, lambda o: o.reshape(bs, N_GROUPS_C, P_STATE)),
        stack(sim_s, lambda o: o.reshape(bs, N_GROUPS_C, P_STATE)),
        stack(pool_s, lambda o: tm_to_bm(o, POOL_BUF, bs)),
        stack(v_s, lambda o: tm_to_bm(o, ts, bs)),
    )
```

```python
import functools
import math

import jax
import jax.numpy as jnp
from jax import lax
from jax.experimental import pallas as pl
from jax.experimental.pallas import tpu as pltpu

F32 = jnp.float32
BF16 = jnp.bfloat16
I32 = jnp.int32

SUBLANES = 8
LANES = 128

D_MODEL = 1024
D_BR = 256
N_BRANCH = 4
N_HEADS = 4
HEAD_DIM = D_BR // N_HEADS
CHUNK = 128
CONV_W = 4
LRU_C = 8.0
N_GROUPS_C = 16
GROUP_C = 16
P_STATE = 64
S5_W = N_GROUPS_C * P_STATE
POOL_WINDOWS = (2, 4, 8, 16)
POOL_BUF = max(POOL_WINDOWS) - 1
GROUP_D = D_BR // len(POOL_WINDOWS)
COLS_A = 5 * D_BR
IN_COLS = COLS_A + N_BRANCH * D_MODEL
N_EXPERTS = 8
EPS = 1e-6
PAST_LEN = 16384
SQRT_2_OVER_PI = math.sqrt(2.0 / math.pi)
ROW_TILE = D_MODEL // LANES


def _gelu(x):
    return x * (0.5 * (1.0 + jnp.tanh(SQRT_2_OVER_PI * (x + 0.044715 * (x * x * x)))))


def _sigmoid(x):
    return 1.0 / (1.0 + jnp.exp(-x))


def _rms(x, g):
    return x * lax.rsqrt(jnp.mean(x * x, axis=-1, keepdims=True) + EPS) * g


def _dot(a, b):
    return jnp.dot(a, b, preferred_element_type=F32)


def _top2(cols, out_shape):
    m1, i1 = cols[0], jnp.zeros_like(cols[0], I32)
    m2, i2 = jnp.full_like(cols[0], -jnp.inf), jnp.zeros_like(cols[0], I32)
    for e in range(1, len(cols)):
        v = cols[e]
        gt1 = v > m1
        gt2 = v > m2
        m2 = jnp.where(gt1, m1, jnp.where(gt2, v, m2))
        i2 = jnp.where(gt1, i1, jnp.where(gt2, e, i2))
        m1 = jnp.where(gt1, v, m1)
        i1 = jnp.where(gt1, e, i1)
    ex = jnp.exp(m2 - m1)
    g1 = 1.0 / (1.0 + ex)
    g2 = ex / (1.0 + ex)
    lane = lax.broadcasted_iota(I32, out_shape, 1)
    idx = jnp.where(lane == 0, i1, jnp.where(lane == 1, i2, 0))
    gate = jnp.where(lane == 0, g1, jnp.where(lane == 1, g2, 0.0))
    return idx, gate


def _const_spec(shape, single=False):
    nd = len(shape)
    if single:
        return pl.BlockSpec(shape, lambda *_: (0,) * nd, pipeline_mode=pl.Buffered(1))
    return pl.BlockSpec(shape, lambda *_: (0,) * nd)


def _layer_spec(shape, layer, single=False):
    idx = (layer,) + (0,) * (len(shape) - 1)
    blk = (None,) + tuple(shape[1:])
    if single:
        return pl.BlockSpec(blk, lambda *_: idx, pipeline_mode=pl.Buffered(1))
    return pl.BlockSpec(blk, lambda *_: idx)


def _mixer_body(cfg, *refs):
    B, Tt, start_pos, mm_gmlp, emit_v, route = cfg
    R = Tt * B
    refs = list(refs)
    h_ref = refs.pop(0)
    (conv0_ref, lru0_ref, sre0_ref, sim0_ref, pool0_ref,
     ng_ref, win_ref, bin_ref, lng_ref, lnb_ref, ws_ref, bsm_ref,
     cw_ref, cb_ref, wa_ref, ba_ref, wx_ref, bx_ref, lam_ref,
     lbr_ref, lbi_ref, bblk_ref, cre_ref, cim_ref, d_ref, wglu_ref, bglu_ref,
     pw_ref, ps_ref, wb_ref, wo_ref) = refs[:31]
    n_in = 33 if route else 31
    if route:
        nfg_ref, rwt_ref = refs[31:33]
    n_out = 6 + int(emit_v) + 2 * int(route)
    outs = refs[n_in:n_in + n_out]
    out_ref, convo_ref, lruo_ref, sreo_ref, simo_ref, poolo_ref = outs[:6]
    if route:
        idx_ref, gate_ref = outs[-2:]
    (xn_s, cext_s, pext_s, lruh_s, sre_s, sim_s, u_s, v_s, mix_s, xc_s,
     a_s, b_s, yc_s, yd_s, s5_s) = refs[n_in + n_out:]

    i = pl.program_id(0)
    SB = min(R, 256)
    SB5 = max(B, min(R, 256))
    lane256 = lax.broadcasted_iota(I32, (1, D_BR), 1)
    ydt = yc_s.dtype

    def mm(a, w):
        return _dot(a.astype(BF16), w)

    def init_carries():
        cext_s[0:(CONV_W - 1) * B, :] = conv0_ref[...]
        pext_s[0:POOL_BUF * B, :] = pool0_ref[...]
        lruh_s[...] = lru0_ref[...]
        sre_s[...] = sre0_ref[...]
        sim_s[...] = sim0_ref[...]

    def blocks(n, size, fn):
        if n == 1:
            fn(0)
        else:
            def body(k, c):
                fn(pl.multiple_of(k * size, size))
                return c
            lax.fori_loop(0, n, body, 0)

    def stage_in(r0):
        rows = pl.ds(r0, SB)
        xn = _rms(h_ref[rows, :], ng_ref[...]).astype(BF16)
        xn_s[rows, :] = xn
        pa = _dot(xn, win_ref[:, 0:COLS_A]) + bin_ref[:, 0:COLS_A]
        u_s[rows, :] = _gelu(pa[:, 0:D_BR])
        gv = _gelu(pa[:, D_BR:2 * D_BR])
        mu = jnp.mean(gv, axis=-1, keepdims=True)
        var = jnp.mean(jnp.square(gv - mu), axis=-1, keepdims=True)
        v = (gv - mu) * lax.rsqrt(var + EPS) * lng_ref[...] + lnb_ref[...]
        v_s[0, rows, :] = v[:, 0:LANES]
        v_s[1, rows, :] = v[:, LANES:2 * LANES]
        cext_s[pl.ds(r0 + (CONV_W - 1) * B, SB), :] = pa[:, 2 * D_BR:3 * D_BR]
        xc_s[rows, :] = pa[:, 3 * D_BR:4 * D_BR]
        pext_s[pl.ds(r0 + POOL_BUF * B, SB), :] = pa[:, 4 * D_BR:5 * D_BR]

    def stage_gmlp():
        if mm_gmlp:
            tril = (lax.broadcasted_iota(I32, (CHUNK, CHUNK), 0)
                    >= lax.broadcasted_iota(I32, (CHUNK, CHUNK), 1))
            wm = [jnp.where(tril, ws_ref[hd], 0.0).astype(BF16) for hd in range(N_HEADS)]
            head = lane256 // HEAD_DIM
            for b in range(B):
                vb = jnp.concatenate([v_s[0, pl.ds(b, Tt, stride=B), :],
                                      v_s[1, pl.ds(b, Tt, stride=B), :]], axis=1).astype(BF16)
                mixed = bsm_ref[...]
                for hd in range(N_HEADS):
                    mixed = mixed + jnp.where(head == hd, mm(wm[hd], vb), 0.0)
                mix_s[0, pl.ds(b, Tt, stride=B), :] = mixed[:, 0:LANES]
                mix_s[1, pl.ds(b, Tt, stride=B), :] = mixed[:, LANES:2 * LANES]
        else:
            for t in range(Tt):
                for half in range(2):
                    lo = half * LANES
                    acc = jnp.broadcast_to(bsm_ref[t:t + 1, lo:lo + LANES], (B, LANES))
                    for s in range(t + 1):
                        w = ws_ref[t * Tt + s:t * Tt + s + 1, lo:lo + LANES]
                        acc = acc + w * v_s[half, s * B:(s + 1) * B, :]
                    mix_s[half, t * B:(t + 1) * B, :] = acc
        if emit_v:
            vo_ref = outs[6]
            vo_ref[:, 0:LANES] = v_s[0]
            vo_ref[:, LANES:2 * LANES] = v_s[1]

    log_sig_lam = (jnp.minimum(lam_ref[...], 0.0)
                   - jnp.log1p(jnp.exp(-jnp.abs(lam_ref[...]))))

    def stage_lru(r0):
        rows = pl.ds(r0, SB)
        conv = cb_ref[...] + cext_s[pl.ds(r0, SB), :] * cw_ref[0:1, :]
        for k in range(1, CONV_W):
            conv = conv + cext_s[pl.ds(r0 + k * B, SB), :] * cw_ref[k:k + 1, :]
        cbf = conv.astype(BF16)
        r = _sigmoid(mm(cbf, wa_ref[...]) + ba_ref[...])
        ig = _sigmoid(mm(cbf, wx_ref[...]) + bx_ref[...])
        a = jnp.exp(LRU_C * r * log_sig_lam)
        mult = jnp.sqrt(1.0 - a * a)
        if start_pos == 0:
            row = lax.broadcasted_iota(I32, (SB, D_BR), 0) + r0
            mult = jnp.where(jnp.logical_and(i == 0, row < B), 1.0, mult)
        a_s[rows, :] = a
        b_s[rows, :] = mult * ig * conv

    def lru_step(t, hprev):
        rows = pl.ds(t * B, B) if isinstance(t, int) else pl.ds(pl.multiple_of(t * B, B), B)
        hnew = a_s[rows, :] * hprev + b_s[rows, :]
        b_s[rows, :] = hnew
        return hnew

    def lru_scan():
        if Tt <= 8:
            hl = lruh_s[...]
            for t in range(Tt):
                hl = lru_step(t, hl)
        else:
            hl = lax.fori_loop(0, Tt, lru_step, lruh_s[...], unroll=8)
        lruh_s[...] = hl
        lruo_ref[...] = hl
        conv_tail = cext_s[R:R + (CONV_W - 1) * B, :]
        convo_ref[...] = conv_tail
        cext_s[0:(CONV_W - 1) * B, :] = conv_tail

    lbr = jnp.broadcast_to(lbr_ref[...], (B, S5_W))
    lbi = jnp.broadcast_to(lbi_ref[...], (B, S5_W))
    steps5 = SB5 // B

    def stage_s5(r0):
        rows = pl.ds(r0, SB5)
        xc = xc_s[rows, :]
        s5_s[...] = mm(xc, bblk_ref[...])

        def s5_step(t, carry):
            sre, sim = carry
            rr = pl.ds(t * B, B) if isinstance(t, int) else pl.ds(pl.multiple_of(t * B, B), B)
            nre = lbr * sre - lbi * sim + s5_s[rr, 0:S5_W]
            nim = lbr * sim + lbi * sre + s5_s[rr, S5_W:2 * S5_W]
            s5_s[rr, 0:S5_W] = nre
            s5_s[rr, S5_W:2 * S5_W] = nim
            return nre, nim

        carry = (sre_s[...], sim_s[...])
        for t in range(steps5):
            carry = s5_step(t, carry)
        sre_s[...] = carry[0]
        sim_s[...] = carry[1]
        y = (mm(s5_s[:, 0:S5_W], cre_ref[...])
             - mm(s5_s[:, S5_W:2 * S5_W], cim_ref[...])
             + d_ref[...] * xc)
        y = _gelu(y)
        y = y * _sigmoid(mm(y, wglu_ref[...]) + bglu_ref[...])
        yc_s[rows, :] = y.astype(ydt)

    wlane = jnp.where(lane256 < GROUP_D, POOL_WINDOWS[0],
                      jnp.where(lane256 < 2 * GROUP_D, POOL_WINDOWS[1],
                                jnp.where(lane256 < 3 * GROUP_D, POOL_WINDOWS[2], POOL_WINDOWS[3])))

    def stage_pool(r0):
        base = r0 + POOL_BUF * B
        tok = pext_s[pl.ds(base, SB), :]
        acc = tok
        sums = {}
        for j in range(1, max(POOL_WINDOWS)):
            acc = acc + pext_s[pl.ds(base - j * B, SB), :]
            if j + 1 in POOL_WINDOWS:
                sums[j + 1] = acc
        sel = jnp.where(lane256 < GROUP_D, sums[2],
                        jnp.where(lane256 < 2 * GROUP_D, sums[4],
                                  jnp.where(lane256 < 3 * GROUP_D, sums[8], sums[16])))
        if start_pos >= POOL_BUF:
            cnt = wlane.astype(F32)
        else:
            row = lax.broadcasted_iota(I32, (SB, D_BR), 0) + r0
            tpos = start_pos + i * Tt + lax.shift_right_logical(row, jnp.full_like(row, int(math.log2(B))))
            cnt = jnp.minimum(wlane, tpos + 1).astype(F32)
        diff = sel / cnt - tok
        yd_s[pl.ds(r0, SB), :] = (mm(diff, pw_ref[...]) * ps_ref[...]).astype(ydt)

    def pool_tail():
        tail = pext_s[R:R + POOL_BUF * B, :]
        poolo_ref[...] = tail
        pext_s[0:POOL_BUF * B, :] = tail

    SBM = min(R, 128)

    def stage_merge(r0):
        rows = pl.ds(r0, SBM)
        xn = xn_s[rows, :]
        ya = u_s[rows, :] * jnp.concatenate([mix_s[0, rows, :], mix_s[1, rows, :]], axis=1)
        ys = (ya, b_s[rows, :], yc_s[rows, :], yd_s[rows, :])
        merged = None
        for n in range(N_BRANCH):
            c0 = COLS_A + n * D_MODEL
            logits = _dot(xn, win_ref[:, c0:c0 + D_MODEL]) + bin_ref[:, c0:c0 + D_MODEL]
            term = _sigmoid(logits) * mm(ys[n], wb_ref[n * D_BR:(n + 1) * D_BR, :])
            merged = term if merged is None else merged + term
        out = h_ref[rows, :] + mm(merged, wo_ref[...])
        out_ref[rows, :] = out
        if route:
            xr = _rms(out, nfg_ref[...])
            cols = [jnp.sum(xr * rwt_ref[e:e + 1, :], axis=-1, keepdims=True) for e in range(N_EXPERTS)]
            idx, gate = _top2(cols, (SBM, LANES))
            idx_ref[rows, :] = idx
            gate_ref[rows, :] = gate

    def stage_front(r0):
        stage_in(r0)
        stage_lru(r0)
        stage_pool(r0)

    def sequence_stages():
        blocks(R // SB, SB, stage_front)
        stage_gmlp()
        lru_scan()
        pool_tail()

    def merge_rows(r0):
        for j in range(SB5 // SBM):
            rj = r0 + j * SBM
            stage_merge(rj if isinstance(rj, int) else pl.multiple_of(rj, SBM))

    nb5 = R // SB5
    pl.when(i == 0)(init_carries)
    sequence_stages()
    stage_s5(0)
    if nb5 > 1:
        def piped(k, c):
            r0 = pl.multiple_of(k * SB5, SB5)
            stage_s5(r0)
            merge_rows(r0 - SB5)
            return c
        lax.fori_loop(1, nb5, piped, 0)
    merge_rows((nb5 - 1) * SB5)
    sreo_ref[...] = sre_s[...]
    simo_ref[...] = sim_s[...]


def _mixer_call(h_all, blk0, nsteps, B, Tt, start_pos, mm_gmlp, emit_v, states, state_layer, lp, layer,
                router=None):
    R = Tt * B
    route = router is not None
    cfg = (B, Tt, start_pos, mm_gmlp, emit_v, route)
    small = list(states)
    params = list(lp)
    big = {1, 16, 24, 25}
    h_spec = pl.BlockSpec((R, D_MODEL), lambda i: (blk0 + i, 0))
    in_specs = [h_spec]
    if state_layer is None:
        in_specs += [_const_spec(a.shape) for a in small]
        st_shapes = [a.shape for a in small]
    else:
        in_specs += [_layer_spec(a.shape, state_layer) for a in small]
        st_shapes = [a.shape[1:] for a in small]
    in_specs += [_layer_spec(a.shape, layer, single=(k in big)) for k, a in enumerate(params)]
    out_shape = [jax.ShapeDtypeStruct(h_all.shape, F32)] + [jax.ShapeDtypeStruct(s, F32) for s in st_shapes]
    out_specs = [h_spec] + [_const_spec(s.shape) for s in out_shape[1:]]
    if emit_v:
        out_shape.append(jax.ShapeDtypeStruct((R, D_BR), F32))
        out_specs.append(_const_spec((R, D_BR)))
    extra = []
    if route:
        extra = list(router)
        in_specs += [_const_spec(a.shape) for a in extra]
        out_shape += [jax.ShapeDtypeStruct((nsteps * R, LANES), I32),
                      jax.ShapeDtypeStruct((nsteps * R, LANES), F32)]
        out_specs += [pl.BlockSpec((R, LANES), lambda i: (i, 0))] * 2
    SB5 = max(B, min(R, 256))
    scratch = [
        pltpu.VMEM((R, D_MODEL), BF16),
        pltpu.VMEM((R + (CONV_W - 1) * B, D_BR), F32),
        pltpu.VMEM((R + POOL_BUF * B, D_BR), F32),
        pltpu.VMEM((B, D_BR), F32),
        pltpu.VMEM((B, S5_W), F32), pltpu.VMEM((B, S5_W), F32),
        pltpu.VMEM((R, D_BR), F32),
        pltpu.VMEM((2, R, LANES), F32),
        pltpu.VMEM((2, R, LANES), F32),
        pltpu.VMEM((R, D_BR), F32),
        pltpu.VMEM((R, D_BR), F32), pltpu.VMEM((R, D_BR), F32),
        pltpu.VMEM((R, D_BR), BF16), pltpu.VMEM((R, D_BR), BF16),
        pltpu.VMEM((SB5, 2 * S5_W), F32),
    ]
    return pl.pallas_call(
        functools.partial(_mixer_body, cfg),
        grid=(nsteps,),
        in_specs=in_specs,
        out_specs=out_specs,
        out_shape=out_shape,
        scratch_shapes=scratch,
        input_output_aliases={0: 0},
        compiler_params=pltpu.CompilerParams(dimension_semantics=("arbitrary",),
                                             vmem_limit_bytes=60 * 1024 * 1024),
        name="mixer_b%d" % B,
    )(h_all, *small, *params, *extra)


def _s5prep_body(are_ref, aim_ref, ldt_ref, bre_ref, bim_ref, lbr_ref, lbi_ref, bbr_ref, bbi_ref):
    a_re, a_im = are_ref[...], aim_ref[...]
    dt = jnp.exp(ldt_ref[...])
    mag = jnp.exp(a_re * dt)
    lb_re = mag * jnp.cos(a_im * dt)
    lb_im = mag * jnp.sin(a_im * dt)
    den = a_re * a_re + a_im * a_im
    n_re = lb_re - 1.0
    q_re = (n_re * a_re + lb_im * a_im) / den
    q_im = (lb_im * a_re - n_re * a_im) / den
    lbr_ref[...] = lb_re
    lbi_ref[...] = lb_im
    bbr_ref[...] = q_re * bre_ref[...] - q_im * bim_ref[...]
    bbi_ref[...] = q_re * bim_ref[...] + q_im * bre_ref[...]


def _s5prep(a_re, a_im, log_dt, b_re, b_im):
    depth = a_re.shape[0]
    rows = depth * N_GROUPS_C * GROUP_C

    def rep(x):
        return jnp.broadcast_to(x[:, :, None, :], (depth, N_GROUPS_C, GROUP_C, P_STATE)).reshape(rows, P_STATE)

    ldt = jnp.broadcast_to(log_dt[:, :, None, None], (depth, N_GROUPS_C, GROUP_C, P_STATE)).reshape(rows, P_STATE)
    b_re_t = jnp.transpose(b_re, (0, 1, 3, 2)).reshape(rows, P_STATE)
    b_im_t = jnp.transpose(b_im, (0, 1, 3, 2)).reshape(rows, P_STATE)
    shp = jax.ShapeDtypeStruct((rows, P_STATE), F32)
    lbr, lbi, bbr, bbi = pl.pallas_call(_s5prep_body, out_shape=[shp] * 4, name="s5prep")(
        rep(a_re), rep(a_im), ldt, b_re_t, b_im_t)
    r4 = lambda x: x.reshape(depth, N_GROUPS_C, GROUP_C, P_STATE)
    return r4(lbr)[:, :, 0, :], r4(lbi)[:, :, 0, :], r4(bbr), r4(bbi)


def _ffn_body(chunks, x_ref, g_ref, w1_ref, w3_ref, w2_ref, o_ref):
    x = x_ref[...]
    xn = _rms(x, g_ref[...]).astype(BF16)
    acc = None
    for c0, cw in chunks:
        a = _dot(xn, w1_ref[:, c0:c0 + cw])
        b = _dot(xn, w3_ref[:, c0:c0 + cw])
        t = _dot((a * _sigmoid(a) * b).astype(BF16), w2_ref[c0:c0 + cw, :])
        acc = t if acc is None else acc + t
    o_ref[...] = x + acc


def _ffn_call(h, g, w1, w3, w2, tm, tf):
    dff = w1.shape[1]
    chunks = tuple((c0, min(tf, dff - c0)) for c0 in range(0, dff, tf))
    return pl.pallas_call(
        functools.partial(_ffn_body, chunks),
        grid=(h.shape[0] // tm,),
        in_specs=[pl.BlockSpec((tm, D_MODEL), lambda i: (i, 0)),
                  _const_spec((1, D_MODEL)),
                  _const_spec((D_MODEL, dff), single=True),
                  _const_spec((D_MODEL, dff), single=True),
                  _const_spec((dff, D_MODEL), single=True)],
        out_specs=pl.BlockSpec((tm, D_MODEL), lambda i: (i, 0)),
        out_shape=jax.ShapeDtypeStruct(h.shape, F32),
        input_output_aliases={0: 0},
        compiler_params=pltpu.CompilerParams(dimension_semantics=("arbitrary",),
                                             vmem_limit_bytes=56 * 1024 * 1024),
        name="ffn",
    )(h, g, w1, w3, w2)


def _row_copy_wait(hbm_ref, sem, nrows):
    pltpu.make_async_copy(hbm_ref.at[pl.ds(0, nrows)], hbm_ref.at[pl.ds(0, nrows)], sem).wait()


def _dispatch_body(tm, pad_rows, pos_ref, ends_ref, x_ref, g_ref, xs_ref, buf, zbuf, sem):
    @pl.when(pl.program_id(0) == 0)
    def _():
        zbuf[...] = jnp.zeros_like(zbuf)
        n_rows = xs_ref.shape[0]
        for e in range(N_EXPERTS):
            for start in (jnp.maximum(ends_ref[e] - pad_rows, 0), n_rows - (e + 1) * pad_rows):
                cp = pltpu.make_async_copy(zbuf, xs_ref.at[pl.ds(start, pad_rows)], sem.at[1])
                cp.start()
                cp.wait()

    xn = _rms(x_ref[...], g_ref[...])
    for k in range(ROW_TILE):
        buf[pl.ds(k, tm, stride=ROW_TILE), :] = xn[:, k * LANES:(k + 1) * LANES]

    def issue(r, c):
        src = buf.at[pl.ds(pl.multiple_of(r * ROW_TILE, ROW_TILE), ROW_TILE), :]
        pltpu.make_async_copy(src, xs_ref.at[pos_ref[0, 0, 2 * r]], sem.at[0]).start(priority=0)
        pltpu.make_async_copy(src, xs_ref.at[pos_ref[0, 0, 2 * r + 1]], sem.at[0]).start(priority=1)
        return c

    lax.fori_loop(0, tm, issue, 0)
    _row_copy_wait(xs_ref, sem.at[0], 2 * tm)


def _dispatch_call(h, g, pos, ends, n_rows, pad_rows, tm):
    n = h.shape[0]
    nt = n // tm
    return pl.pallas_call(
        functools.partial(_dispatch_body, tm, pad_rows),
        grid=(nt,),
        in_specs=[pl.BlockSpec((1, 1, 2 * tm), lambda i: (i, 0, 0), memory_space=pltpu.SMEM),
                  pl.BlockSpec(memory_space=pltpu.SMEM),
                  pl.BlockSpec((tm, D_MODEL), lambda i: (i, 0)),
                  _const_spec((1, D_MODEL))],
        out_specs=pl.BlockSpec(memory_space=pl.ANY),
        out_shape=jax.ShapeDtypeStruct((n_rows, ROW_TILE, LANES), F32),
        scratch_shapes=[pltpu.VMEM((tm * ROW_TILE, LANES), F32),
                        pltpu.VMEM((pad_rows, ROW_TILE, LANES), F32),
                        pltpu.SemaphoreType.DMA((2,))],
        compiler_params=pltpu.CompilerParams(dimension_semantics=("arbitrary",)),
        name="dispatch",
    )(pos.reshape(nt, 1, 2 * tm), ends, h, g)


def _experts_body(tm, te_ref, tv_ref, tfirst_ref, xs_ref, w1_ref, w3_ref, w2_ref, y_ref,
                  x_s, acc_s, c1_s, c3_s, c2_s):
    del te_ref
    t = pl.program_id(0)
    f = pl.program_id(1)
    valid = tv_ref[t] > 0

    @pl.when(tfirst_ref[t] > 0)
    def _():
        c1_s[f] = w1_ref[0].astype(BF16)
        c3_s[f] = w3_ref[0].astype(BF16)
        c2_s[f] = w2_ref[0].astype(BF16)

    @pl.when(jnp.logical_and(valid, f == 0))
    def _():
        for k in range(ROW_TILE):
            x_s[:, k * LANES:(k + 1) * LANES] = xs_ref[pl.ds(k, tm, stride=ROW_TILE), :].astype(BF16)
        acc_s[...] = jnp.zeros_like(acc_s)

    @pl.when(valid)
    def _():
        x = x_s[...]
        a = _dot(x, c1_s[f])
        b = _dot(x, c3_s[f])
        acc_s[...] += _dot((a * _sigmoid(a) * b).astype(BF16), c2_s[f])

    last = f == pl.num_programs(1) - 1

    @pl.when(jnp.logical_and(valid, last))
    def _():
        acc = acc_s[...]
        for k in range(ROW_TILE):
            y_ref[pl.ds(k, tm, stride=ROW_TILE), :] = acc[:, k * LANES:(k + 1) * LANES]

    @pl.when(jnp.logical_and(jnp.logical_not(valid), last))
    def _():
        y_ref[...] = jnp.zeros_like(y_ref)


def _experts_call(xs2d, tile_expert, tile_valid, tile_first, w1, w3, w2, tm, tf):
    rows = xs2d.shape[0] // ROW_TILE
    nt = rows // tm
    dff = w1.shape[2]
    nf = dff // tf

    def fb(t, f, tfirst):
        return jnp.where(tfirst[t] > 0, f, nf - 1)

    grid_spec = pltpu.PrefetchScalarGridSpec(
        num_scalar_prefetch=3,
        grid=(nt, nf),
        in_specs=[pl.BlockSpec((tm * ROW_TILE, LANES), lambda t, f, te, tv, t1: (t, 0)),
                  pl.BlockSpec((1, D_MODEL, tf), lambda t, f, te, tv, t1: (te[t], 0, fb(t, f, t1))),
                  pl.BlockSpec((1, D_MODEL, tf), lambda t, f, te, tv, t1: (te[t], 0, fb(t, f, t1))),
                  pl.BlockSpec((1, tf, D_MODEL), lambda t, f, te, tv, t1: (te[t], fb(t, f, t1), 0))],
        out_specs=pl.BlockSpec((tm * ROW_TILE, LANES), lambda t, f, te, tv, t1: (t, 0)),
        scratch_shapes=[pltpu.VMEM((tm, D_MODEL), BF16), pltpu.VMEM((tm, D_MODEL), F32),
                        pltpu.VMEM((nf, D_MODEL, tf), BF16), pltpu.VMEM((nf, D_MODEL, tf), BF16),
                        pltpu.VMEM((nf, tf, D_MODEL), BF16)])
    return pl.pallas_call(
        functools.partial(_experts_body, tm),
        grid_spec=grid_spec,
        out_shape=jax.ShapeDtypeStruct(xs2d.shape, F32),
        compiler_params=pltpu.CompilerParams(dimension_semantics=("arbitrary", "arbitrary"),
                                             vmem_limit_bytes=56 * 1024 * 1024),
        name="experts",
    )(tile_expert, tile_valid, tile_first, xs2d, w1, w3, w2)


def _combine_body(tm, final, pos_ref, h_ref, gate_ref, *rest):
    if final:
        ng_ref, y_ref, o_ref, buf, sem = rest
    else:
        y_ref, o_ref, buf, sem = rest

    def issue(r, c):
        for j in range(2):
            dst = buf.at[j, pl.ds(pl.multiple_of(r * ROW_TILE, ROW_TILE), ROW_TILE), :]
            pltpu.make_async_copy(y_ref.at[pos_ref[0, 0, 2 * r + j]], dst, sem.at[0]).start(priority=j)
        return c

    lax.fori_loop(0, tm, issue, 0)
    _row_copy_wait(y_ref, sem.at[0], 2 * tm)
    g0 = gate_ref[:, 0:1]
    g1 = gate_ref[:, 1:2]
    for k in range(ROW_TILE):
        cols = slice(k * LANES, (k + 1) * LANES)
        moe = (g0 * buf[0, pl.ds(k, tm, stride=ROW_TILE), :]
               + g1 * buf[1, pl.ds(k, tm, stride=ROW_TILE), :])
        o_ref[:, cols] = h_ref[:, cols] + moe
    if final:
        o_ref[...] = _rms(o_ref[...], ng_ref[...])


def _combine_call(h, gates, pos, y3, tm, final_g=None):
    n = h.shape[0]
    nt = n // tm
    final = final_g is not None
    in_specs = [pl.BlockSpec((1, 1, 2 * tm), lambda i: (i, 0, 0), memory_space=pltpu.SMEM),
                pl.BlockSpec((tm, D_MODEL), lambda i: (i, 0)),
                pl.BlockSpec((tm, LANES), lambda i: (i, 0))]
    args = [pos.reshape(nt, 1, 2 * tm), h, gates]
    if final:
        in_specs.append(_const_spec((1, D_MODEL)))
        args.append(final_g)
    in_specs.append(pl.BlockSpec(memory_space=pl.ANY))
    args.append(y3)
    return pl.pallas_call(
        functools.partial(_combine_body, tm, final),
        grid=(nt,),
        in_specs=in_specs,
        out_specs=pl.BlockSpec((tm, D_MODEL), lambda i: (i, 0)),
        out_shape=jax.ShapeDtypeStruct((n, D_MODEL), F32),
        scratch_shapes=[pltpu.VMEM((2, tm * ROW_TILE, LANES), F32), pltpu.SemaphoreType.DMA((1,))],
        compiler_params=pltpu.CompilerParams(dimension_semantics=("arbitrary",)),
        name="combine",
    )(*args)


def _to_tm_body(nb, tt, x_ref, tail_ref, o_ref, slab):
    last = pl.program_id(0) == pl.num_programs(0) - 1

    @pl.when(jnp.logical_not(last))
    def _():
        for b in range(nb):
            for k in range(ROW_TILE):
                slab[k, pl.ds(b, tt, stride=nb), :] = x_ref[b, :, k * LANES:(k + 1) * LANES]
        for k in range(ROW_TILE):
            o_ref[:, k * LANES:(k + 1) * LANES] = slab[k]

    @pl.when(last)
    def _():
        o_ref[0:tail_ref.shape[0], :] = tail_ref[...]


def _from_tm_body(nb, tt, x_ref, o_ref, slab):
    for k in range(ROW_TILE):
        slab[k] = x_ref[:, k * LANES:(k + 1) * LANES]
    for b in range(nb):
        for k in range(ROW_TILE):
            o_ref[b, :, k * LANES:(k + 1) * LANES] = slab[k, pl.ds(b, tt, stride=nb), :]


def _to_tm_call(x, tail, tt):
    nb, t, d = x.shape
    ns = tail.shape[0]
    assert ns <= tt * nb
    nsteps = t // tt
    return pl.pallas_call(
        functools.partial(_to_tm_body, nb, tt),
        grid=(nsteps + 1,),
        in_specs=[pl.BlockSpec((nb, tt, d), lambda i: (0, jnp.minimum(i, nsteps - 1), 0)),
                  _const_spec((ns, d))],
        out_specs=pl.BlockSpec((tt * nb, d), lambda i: (i, 0)),
        out_shape=jax.ShapeDtypeStruct((nb * t + ns, d), F32),
        scratch_shapes=[pltpu.VMEM((ROW_TILE, tt * nb, LANES), F32)],
        compiler_params=pltpu.CompilerParams(dimension_semantics=("arbitrary",)),
        name="to_tm",
    )(x, tail)


def _from_tm_call(h, nb, t, tt):
    d = h.shape[1]
    return pl.pallas_call(
        functools.partial(_from_tm_body, nb, tt),
        grid=(t // tt,),
        in_specs=[pl.BlockSpec((tt * nb, d), lambda i: (i, 0))],
        out_specs=pl.BlockSpec((nb, tt, d), lambda i: (0, i, 0)),
        out_shape=jax.ShapeDtypeStruct((nb, t, d), F32),
        scratch_shapes=[pltpu.VMEM((ROW_TILE, tt * nb, LANES), F32)],
        compiler_params=pltpu.CompilerParams(dimension_semantics=("arbitrary",)),
        name="from_tm",
    )(h)


def _rms_body(x_ref, g_ref, o_ref):
    o_ref[...] = _rms(x_ref[...], g_ref[...])


def _rms_call(h, g, tm):
    n = h.shape[0]
    return pl.pallas_call(
        _rms_body, grid=(n // tm,),
        in_specs=[pl.BlockSpec((tm, D_MODEL), lambda i: (i, 0)), _const_spec((1, D_MODEL))],
        out_specs=pl.BlockSpec((tm, D_MODEL), lambda i: (i, 0)),
        out_shape=jax.ShapeDtypeStruct((n, D_MODEL), F32),
        name="rms",
    )(h, g)


MOE_TM = 512
MOE_TF = 512
TOK_TM = 768
FFN_TF = 512


def _moe_layer(h, norm_g, idx, gates, w1, w3, w2, e0, final_g):
    n = h.shape[0]
    e_flat = idx[:, 0:2].reshape(2 * n)
    onehot = (e_flat[:, None] == jnp.arange(N_EXPERTS, dtype=I32)[None, :]).astype(I32)
    csum = jnp.cumsum(onehot, axis=0)
    rank = jnp.sum((csum - onehot) * onehot, axis=1)
    counts = csum[-1]
    padded = ((counts + MOE_TM - 1) // MOE_TM) * MOE_TM
    ends = jnp.cumsum(padded)
    starts = ends - padded
    pos = (jnp.sum(onehot * starts[None, :], axis=1) + rank).astype(I32)
    n_tiles = (2 * n) // MOE_TM + N_EXPERTS
    tile_start = jnp.arange(n_tiles, dtype=I32) * MOE_TM
    tile_valid = (tile_start < ends[-1]).astype(I32)
    tile_expert = jnp.minimum(jnp.sum((tile_start[:, None] >= ends[None, :]).astype(I32), axis=1),
                              N_EXPERTS - 1).astype(I32)
    last_valid = jnp.max(jnp.where(tile_valid > 0, tile_expert, 0))
    tile_expert = jnp.where(tile_valid > 0, tile_expert, last_valid)
    xs = _dispatch_call(h, norm_g, pos, ends.astype(I32), n_tiles * MOE_TM, MOE_TM, TOK_TM)
    prev_expert = jnp.concatenate([jnp.full((1,), -1, I32), tile_expert[:-1]])
    tile_first = jnp.logical_and(tile_expert != prev_expert, tile_valid > 0).astype(I32)
    y2d = _experts_call(xs.reshape(n_tiles * MOE_TM * ROW_TILE, LANES), tile_expert + e0, tile_valid, tile_first,
                        w1, w3, w2, MOE_TM, MOE_TF)
    return _combine_call(h, gates, pos, y2d.reshape(n_tiles * MOE_TM, ROW_TILE, LANES), TOK_TM, final_g)


def _block_diag(w):
    n, k, a, b = w.shape
    eye = jnp.eye(k, dtype=w.dtype)
    return jnp.einsum('lkab,kj->lkajb', w, eye).reshape(n, k * a, k * b)


def kernel(x_prompt, x_sample, state_conv, state_lru, state_s5_re, state_s5_im, state_pool, norm_mix_g, w_in, b_in, gmlp_ln_g, gmlp_ln_b, gmlp_w_s, gmlp_b_s, conv_w, conv_b, lru_w_a, lru_b_a, lru_w_x, lru_b_x, lru_lam, s5_a_re, s5_a_im, s5_log_dt, s5_b_re, s5_b_im, s5_c_re, s5_c_im, s5_d, s5_w_glu, s5_b_glu, pool_w, pool_scale, w_branch, w_out, norm_ffn_g, ffn_w1, ffn_w3, ffn_w2, router_w, moe_w1, moe_w3, moe_w2, norm_final_g):
    depth = w_in.shape[0]
    bp, tp, _ = x_prompt.shape
    bs, ts, _ = x_sample.shape
    n_p, n_s = bp * tp, bs * ts
    assert n_p % (CHUNK * bp) == 0 and n_p % n_s == 0

    h = _to_tm_call(x_prompt, jnp.transpose(x_sample, (1, 0, 2)).reshape(n_s, D_MODEL), CHUNK)

    lbr, lbi, bbr, bbi = _s5prep(s5_a_re, s5_a_im, s5_log_dt, s5_b_re, s5_b_im)
    lane_head = jnp.arange(D_BR) // HEAD_DIM

    rows = lambda x: x.reshape(depth, 1, -1)
    bf = lambda x: x.astype(BF16)
    common_tail = (
        conv_w, rows(conv_b),
        bf(_block_diag(lru_w_a)), rows(lru_b_a), bf(_block_diag(lru_w_x)), rows(lru_b_x), rows(lru_lam),
        rows(lbr), rows(lbi),
        bf(jnp.concatenate([_block_diag(bbr), _block_diag(bbi)], axis=2)),
        bf(_block_diag(jnp.transpose(s5_c_re, (0, 1, 3, 2)))),
        bf(_block_diag(jnp.transpose(s5_c_im, (0, 1, 3, 2)))), rows(s5_d),
        bf(s5_w_glu), rows(s5_b_glu),
        bf(_block_diag(pool_w)), rows(pool_scale),
        bf(w_branch.reshape(depth, N_BRANCH * D_BR, D_MODEL)), bf(w_out))
    common_head = (rows(norm_mix_g), bf(w_in), rows(b_in), rows(gmlp_ln_g), rows(gmlp_ln_b))
    bsm_p = jnp.transpose(gmlp_b_s[:, lane_head, :], (0, 2, 1))
    lp_p = common_head + (gmlp_w_s, bsm_p) + common_tail
    ws_small = jnp.transpose(gmlp_w_s[:, :, :ts, :ts], (0, 2, 3, 1))[..., lane_head].reshape(depth, ts * ts, D_BR)
    bsm_s = jnp.transpose(gmlp_b_s[:, lane_head, :ts], (0, 2, 1))
    lp_s = common_head + (ws_small, bsm_s) + common_tail
    st_s = (jnp.transpose(state_conv, (0, 2, 1, 3)).reshape(depth, (CONV_W - 1) * bs, D_BR),
            state_lru, state_s5_re.reshape(depth, bs, S5_W), state_s5_im.reshape(depth, bs, S5_W),
            jnp.transpose(state_pool, (0, 2, 1, 3)).reshape(depth, POOL_BUF * bs, D_BR))
    row = lambda x: x.reshape(1, -1)

    conv_p, lru_p, sre_p, sim_p, pool_p = [], [], [], [], []
    conv_s, lru_s, sre_s, sim_s, pool_s, v_s = [], [], [], [], [], []
    zeros_p = (jnp.zeros(((CONV_W - 1) * bp, D_BR), F32), jnp.zeros((bp, D_BR), F32),
               jnp.zeros((bp, S5_W), F32), jnp.zeros((bp, S5_W), F32),
               jnp.zeros((POOL_BUF * bp, D_BR), F32))
    for l in range(depth):
        router = (row(norm_ffn_g[l]), jnp.transpose(router_w[l // 2])) if l % 2 == 1 else None
        outs_p = _mixer_call(h, 0, tp // CHUNK, bp, CHUNK, 0, True, False, zeros_p, None, lp_p, l, router)
        outs_s = _mixer_call(outs_p[0], n_p // n_s, 1, bs, ts, PAST_LEN, False, True, st_s, l, lp_s, l, router)
        h = outs_s[0]
        for lst, o in zip((conv_p, lru_p, sre_p, sim_p, pool_p), outs_p[1:6]):
            lst.append(o)
        for lst, o in zip((conv_s, lru_s, sre_s, sim_s, pool_s, v_s), outs_s[1:7]):
            lst.append(o)

        if l % 2 == 0:
            k = l // 2
            h = _ffn_call(h, row(norm_ffn_g[l]), ffn_w1[k].astype(BF16), ffn_w3[k].astype(BF16),
                          ffn_w2[k].astype(BF16), TOK_TM, FFN_TF)
            if l == depth - 1:
                h = _rms_call(h, row(norm_final_g), TOK_TM)
        else:
            k = l // 2
            idx = jnp.concatenate([outs_p[-2], outs_s[-2]], axis=0)
            gates = jnp.concatenate([outs_p[-1], outs_s[-1]], axis=0)
            stk = lambda w: w.reshape((-1,) + w.shape[2:])
            h = _moe_layer(h, row(norm_ffn_g[l]), idx, gates, stk(moe_w1), stk(moe_w3), stk(moe_w2),
                           k * N_EXPERTS, row(norm_final_g) if l == depth - 1 else None)

    def tm_to_bm(x, t, b):
        return jnp.transpose(x.reshape(t, b, x.shape[-1]), (1, 0, 2))

    y_prompt = _from_tm_call(h, bp, tp, CHUNK)
    y_sample = tm_to_bm(h[n_p:], ts, bs)
    stack = lambda lst, f: jnp.stack([f(o) for o in lst])
    return (
        y_prompt, y_sample,
        stack(conv_p, lambda o: tm_to_bm(o, CONV_W - 1, bp)),
        stack(lru_p, lambda o: o),
        stack(sre_p, lambda o: o.reshape(bp, N_GROUPS_C, P_STATE)),
        stack(sim_p, lambda o: o.reshape(bp, N_GROUPS_C, P_STATE)),
        stack(pool_p, lambda o: tm_to_bm(o, POOL_BUF, bp)),
        stack(conv_s, lambda o: tm_to_bm(o, CONV_W - 1, bs)),
        stack(lru_s, lambda o: o),
        stack(sre_s, lambda o: o.reshape(bs, N_GROUPS_C, P_STATE)),
        stack(sim_s, lambda o: o.reshape(bs, N_GROUPS_C, P_STATE)),
        stack(pool_s, lambda o: tm_to_bm(o, POOL_BUF, bs)),
        stack(v_s, lambda o: tm_to_bm(o, ts, bs)),
    )
```

```python
import functools
import math

import jax
import jax.numpy as jnp
from jax import lax
from jax.experimental import pallas as pl
from jax.experimental.pallas import tpu as pltpu

F32 = jnp.float32
BF16 = jnp.bfloat16
I32 = jnp.int32

SUBLANES = 8
LANES = 128

D_MODEL = 1024
D_BR = 256
N_BRANCH = 4
N_HEADS = 4
HEAD_DIM = D_BR // N_HEADS
CHUNK = 128
CONV_W = 4
LRU_C = 8.0
N_GROUPS_C = 16
GROUP_C = 16
P_STATE = 64
S5_W = N_GROUPS_C * P_STATE
POOL_WINDOWS = (2, 4, 8, 16)
POOL_BUF = max(POOL_WINDOWS) - 1
GROUP_D = D_BR // len(POOL_WINDOWS)
COLS_A = 5 * D_BR
IN_COLS = COLS_A + N_BRANCH * D_MODEL
N_EXPERTS = 8
EPS = 1e-6
PAST_LEN = 16384
SQRT_2_OVER_PI = math.sqrt(2.0 / math.pi)
ROW_TILE = D_MODEL // LANES


def _gelu(x):
    return x * (0.5 * (1.0 + jnp.tanh(SQRT_2_OVER_PI * (x + 0.044715 * (x * x * x)))))


def _sigmoid(x):
    return 1.0 / (1.0 + jnp.exp(-x))


def _rms(x, g):
    return x * lax.rsqrt(jnp.mean(x * x, axis=-1, keepdims=True) + EPS) * g


def _dot(a, b):
    return jnp.dot(a, b, preferred_element_type=F32)


def _top2(cols, out_shape):
    m1, i1 = cols[0], jnp.zeros_like(cols[0], I32)
    m2, i2 = jnp.full_like(cols[0], -jnp.inf), jnp.zeros_like(cols[0], I32)
    for e in range(1, len(cols)):
        v = cols[e]
        gt1 = v > m1
        gt2 = v > m2
        m2 = jnp.where(gt1, m1, jnp.where(gt2, v, m2))
        i2 = jnp.where(gt1, i1, jnp.where(gt2, e, i2))
        m1 = jnp.where(gt1, v, m1)
        i1 = jnp.where(gt1, e, i1)
    ex = jnp.exp(m2 - m1)
    g1 = 1.0 / (1.0 + ex)
    g2 = ex / (1.0 + ex)
    lane = lax.broadcasted_iota(I32, out_shape, 1)
    idx = jnp.where(lane == 0, i1, jnp.where(lane == 1, i2, 0))
    gate = jnp.where(lane == 0, g1, jnp.where(lane == 1, g2, 0.0))
    return idx, gate


def _const_spec(shape, single=False):
    nd = len(shape)
    if single:
        return pl.BlockSpec(shape, lambda *_: (0,) * nd, pipeline_mode=pl.Buffered(1))
    return pl.BlockSpec(shape, lambda *_: (0,) * nd)


def _layer_spec(shape, layer, single=False):
    idx = (layer,) + (0,) * (len(shape) - 1)
    blk = (None,) + tuple(shape[1:])
    if single:
        return pl.BlockSpec(blk, lambda *_: idx, pipeline_mode=pl.Buffered(1))
    return pl.BlockSpec(blk, lambda *_: idx)


def _mixer_body(cfg, *refs):
    B, Tt, start_pos, mm_gmlp, emit_v, route = cfg
    R = Tt * B
    refs = list(refs)
    h_ref = refs.pop(0)
    (conv0_ref, lru0_ref, sre0_ref, sim0_ref, pool0_ref,
     ng_ref, win_ref, bin_ref, lng_ref, lnb_ref, ws_ref, bsm_ref,
     cw_ref, cb_ref, wa_ref, ba_ref, wx_ref, bx_ref, lam_ref,
     lbr_ref, lbi_ref, bblk_ref, cre_ref, cim_ref, d_ref, wglu_ref, bglu_ref,
     pw_ref, ps_ref, wb_ref, wo_ref) = refs[:31]
    n_in = 33 if route else 31
    if route:
        nfg_ref, rwt_ref = refs[31:33]
    n_out = 6 + int(emit_v) + 2 * int(route)
    outs = refs[n_in:n_in + n_out]
    out_ref, convo_ref, lruo_ref, sreo_ref, simo_ref, poolo_ref = outs[:6]
    if route:
        idx_ref, gate_ref = outs[-2:]
    (xn_s, cext_s, pext_s, lruh_s, sre_s, sim_s, u_s, v_s, mix_s, xc_s,
     a_s, b_s, yc_s, yd_s, s5_s) = refs[n_in + n_out:]

    i = pl.program_id(0)
    SB = min(R, 256)
    SB5 = max(B, min(R, 256))
    lane256 = lax.broadcasted_iota(I32, (1, D_BR), 1)
    ydt = yc_s.dtype

    def mm(a, w):
        return _dot(a.astype(BF16), w)

    def init_carries():
        cext_s[0:(CONV_W - 1) * B, :] = conv0_ref[...]
        pext_s[0:POOL_BUF * B, :] = pool0_ref[...]
        lruh_s[...] = lru0_ref[...]
        sre_s[...] = sre0_ref[...]
        sim_s[...] = sim0_ref[...]

    def blocks(n, size, fn):
        if n == 1:
            fn(0)
        else:
            def body(k, c):
                fn(pl.multiple_of(k * size, size))
                return c
            lax.fori_loop(0, n, body, 0)

    def stage_in(r0):
        rows = pl.ds(r0, SB)
        xn = _rms(h_ref[rows, :], ng_ref[...]).astype(BF16)
        xn_s[rows, :] = xn
        pa = _dot(xn, win_ref[:, 0:COLS_A]) + bin_ref[:, 0:COLS_A]
        u_s[rows, :] = _gelu(pa[:, 0:D_BR])
        gv = _gelu(pa[:, D_BR:2 * D_BR])
        mu = jnp.mean(gv, axis=-1, keepdims=True)
        var = jnp.mean(jnp.square(gv - mu), axis=-1, keepdims=True)
        v = (gv - mu) * lax.rsqrt(var + EPS) * lng_ref[...] + lnb_ref[...]
        v_s[0, rows, :] = v[:, 0:LANES]
        v_s[1, rows, :] = v[:, LANES:2 * LANES]
        cext_s[pl.ds(r0 + (CONV_W - 1) * B, SB), :] = pa[:, 2 * D_BR:3 * D_BR]
        xc_s[rows, :] = pa[:, 3 * D_BR:4 * D_BR]
        pext_s[pl.ds(r0 + POOL_BUF * B, SB), :] = pa[:, 4 * D_BR:5 * D_BR]

    def stage_gmlp():
        if mm_gmlp:
            tril = (lax.broadcasted_iota(I32, (CHUNK, CHUNK), 0)
                    >= lax.broadcasted_iota(I32, (CHUNK, CHUNK), 1))
            wm = [jnp.where(tril, ws_ref[hd], 0.0).astype(BF16) for hd in range(N_HEADS)]
            head = lane256 // HEAD_DIM
            for b in range(B):
                vb = jnp.concatenate([v_s[0, pl.ds(b, Tt, stride=B), :],
                                      v_s[1, pl.ds(b, Tt, stride=B), :]], axis=1).astype(BF16)
                mixed = bsm_ref[...]
                for hd in range(N_HEADS):
                    mixed = mixed + jnp.where(head == hd, mm(wm[hd], vb), 0.0)
                mix_s[0, pl.ds(b, Tt, stride=B), :] = mixed[:, 0:LANES]
                mix_s[1, pl.ds(b, Tt, stride=B), :] = mixed[:, LANES:2 * LANES]
        else:
            for t in range(Tt):
                for half in range(2):
                    lo = half * LANES
                    acc = jnp.broadcast_to(bsm_ref[t:t + 1, lo:lo + LANES], (B, LANES))
                    for s in range(t + 1):
                        w = ws_ref[t * Tt + s:t * Tt + s + 1, lo:lo + LANES]
                        acc = acc + w * v_s[half, s * B:(s + 1) * B, :]
                    mix_s[half, t * B:(t + 1) * B, :] = acc
        if emit_v:
            vo_ref = outs[6]
            vo_ref[:, 0:LANES] = v_s[0]
            vo_ref[:, LANES:2 * LANES] = v_s[1]

    log_sig_lam = (jnp.minimum(lam_ref[...], 0.0)
                   - jnp.log1p(jnp.exp(-jnp.abs(lam_ref[...]))))

    def stage_lru(r0):
        rows = pl.ds(r0, SB)
        conv = cb_ref[...] + cext_s[pl.ds(r0, SB), :] * cw_ref[0:1, :]
        for k in range(1, CONV_W):
            conv = conv + cext_s[pl.ds(r0 + k * B, SB), :] * cw_ref[k:k + 1, :]
        cbf = conv.astype(BF16)
        r = _sigmoid(mm(cbf, wa_ref[...]) + ba_ref[...])
        ig = _sigmoid(mm(cbf, wx_ref[...]) + bx_ref[...])
        a = jnp.exp(LRU_C * r * log_sig_lam)
        mult = jnp.sqrt(1.0 - a * a)
        if start_pos == 0:
            row = lax.broadcasted_iota(I32, (SB, D_BR), 0) + r0
            mult = jnp.where(jnp.logical_and(i == 0, row < B), 1.0, mult)
        a_s[rows, :] = a
        b_s[rows, :] = mult * ig * conv

    def lru_step(t, hprev):
        rows = pl.ds(t * B, B) if isinstance(t, int) else pl.ds(pl.multiple_of(t * B, B), B)
        hnew = a_s[rows, :] * hprev + b_s[rows, :]
        b_s[rows, :] = hnew
        return hnew

    def lru_scan():
        if Tt <= 8:
            hl = lruh_s[...]
            for t in range(Tt):
                hl = lru_step(t, hl)
        else:
            hl = lax.fori_loop(0, Tt, lru_step, lruh_s[...], unroll=8)
        lruh_s[...] = hl
        lruo_ref[...] = hl
        conv_tail = cext_s[R:R + (CONV_W - 1) * B, :]
        convo_ref[...] = conv_tail
        cext_s[0:(CONV_W - 1) * B, :] = conv_tail

    lbr = jnp.broadcast_to(lbr_ref[...], (B, S5_W))
    lbi = jnp.broadcast_to(lbi_ref[...], (B, S5_W))
    steps5 = SB5 // B

    def stage_s5(r0):
        rows = pl.ds(r0, SB5)
        xc = xc_s[rows, :]
        s5_s[...] = mm(xc, bblk_ref[...])

        def s5_step(t, carry):
            sre, sim = carry
            rr = pl.ds(t * B, B) if isinstance(t, int) else pl.ds(pl.multiple_of(t * B, B), B)
            nre = lbr * sre - lbi * sim + s5_s[rr, 0:S5_W]
            nim = lbr * sim + lbi * sre + s5_s[rr, S5_W:2 * S5_W]
            s5_s[rr, 0:S5_W] = nre
            s5_s[rr, S5_W:2 * S5_W] = nim
            return nre, nim

        carry = (sre_s[...], sim_s[...])
        for t in range(steps5):
            carry = s5_step(t, carry)
        sre_s[...] = carry[0]
        sim_s[...] = carry[1]
        y = (mm(s5_s[:, 0:S5_W], cre_ref[...])
             - mm(s5_s[:, S5_W:2 * S5_W], cim_ref[...])
             + d_ref[...] * xc)
        y = _gelu(y)
        y = y * _sigmoid(mm(y, wglu_ref[...]) + bglu_ref[...])
        yc_s[rows, :] = y.astype(ydt)

    wlane = jnp.where(lane256 < GROUP_D, POOL_WINDOWS[0],
                      jnp.where(lane256 < 2 * GROUP_D, POOL_WINDOWS[1],
                                jnp.where(lane256 < 3 * GROUP_D, POOL_WINDOWS[2], POOL_WINDOWS[3])))

    def stage_pool(r0):
        base = r0 + POOL_BUF * B
        tok = pext_s[pl.ds(base, SB), :]
        acc = tok
        sums = {}
        for j in range(1, max(POOL_WINDOWS)):
            acc = acc + pext_s[pl.ds(base - j * B, SB), :]
            if j + 1 in POOL_WINDOWS:
                sums[j + 1] = acc
        sel = jnp.where(lane256 < GROUP_D, sums[2],
                        jnp.where(lane256 < 2 * GROUP_D, sums[4],
                                  jnp.where(lane256 < 3 * GROUP_D, sums[8], sums[16])))
        if start_pos >= POOL_BUF:
            cnt = wlane.astype(F32)
        else:
            row = lax.broadcasted_iota(I32, (SB, D_BR), 0) + r0
            tpos = start_pos + i * Tt + lax.shift_right_logical(row, jnp.full_like(row, int(math.log2(B))))
            cnt = jnp.minimum(wlane, tpos + 1).astype(F32)
        diff = sel / cnt - tok
        yd_s[pl.ds(r0, SB), :] = (mm(diff, pw_ref[...]) * ps_ref[...]).astype(ydt)

    def pool_tail():
        tail = pext_s[R:R + POOL_BUF * B, :]
        poolo_ref[...] = tail
        pext_s[0:POOL_BUF * B, :] = tail

    SBM = min(R, 128)

    def stage_merge(r0):
        rows = pl.ds(r0, SBM)
        xn = xn_s[rows, :]
        ya = u_s[rows, :] * jnp.concatenate([mix_s[0, rows, :], mix_s[1, rows, :]], axis=1)
        ys = (ya, b_s[rows, :], yc_s[rows, :], yd_s[rows, :])
        merged = None
        for n in range(N_BRANCH):
            c0 = COLS_A + n * D_MODEL
            logits = _dot(xn, win_ref[:, c0:c0 + D_MODEL]) + bin_ref[:, c0:c0 + D_MODEL]
            term = _sigmoid(logits) * mm(ys[n], wb_ref[n * D_BR:(n + 1) * D_BR, :])
            merged = term if merged is None else merged + term
        out = h_ref[rows, :] + mm(merged, wo_ref[...])
        out_ref[rows, :] = out
        if route:
            xr = _rms(out, nfg_ref[...])
            cols = [jnp.sum(xr * rwt_ref[e:e + 1, :], axis=-1, keepdims=True) for e in range(N_EXPERTS)]
            idx, gate = _top2(cols, (SBM, LANES))
            idx_ref[rows, :] = idx
            gate_ref[rows, :] = gate

    def stage_front(r0):
        stage_in(r0)
        stage_lru(r0)
        stage_pool(r0)

    def sequence_stages():
        blocks(R // SB, SB, stage_front)
        stage_gmlp()
        lru_scan()
        pool_tail()

    def merge_rows(r0):
        for j in range(SB5 // SBM):
            rj = r0 + j * SBM
            stage_merge(rj if isinstance(rj, int) else pl.multiple_of(rj, SBM))

    nb5 = R // SB5
    pl.when(i == 0)(init_carries)
    sequence_stages()
    stage_s5(0)
    if nb5 > 1:
        def piped(k, c):
            r0 = pl.multiple_of(k * SB5, SB5)
            stage_s5(r0)
            merge_rows(r0 - SB5)
            return c
        lax.fori_loop(1, nb5, piped, 0)
    merge_rows((nb5 - 1) * SB5)
    sreo_ref[...] = sre_s[...]
    simo_ref[...] = sim_s[...]


def _mixer_call(h_all, blk0, nsteps, B, Tt, start_pos, mm_gmlp, emit_v, states, state_layer, lp, layer,
                router=None):
    R = Tt * B
    route = router is not None
    cfg = (B, Tt, start_pos, mm_gmlp, emit_v, route)
    small = list(states)
    params = list(lp)
    big = {1, 16, 24, 25}
    h_spec = pl.BlockSpec((R, D_MODEL), lambda i: (blk0 + i, 0))
    in_specs = [h_spec]
    if state_layer is None:
        in_specs += [_const_spec(a.shape) for a in small]
        st_shapes = [a.shape for a in small]
    else:
        in_specs += [_layer_spec(a.shape, state_layer) for a in small]
        st_shapes = [a.shape[1:] for a in small]
    in_specs += [_layer_spec(a.shape, layer, single=(k in big)) for k, a in enumerate(params)]
    out_shape = [jax.ShapeDtypeStruct(h_all.shape, F32)] + [jax.ShapeDtypeStruct(s, F32) for s in st_shapes]
    out_specs = [h_spec] + [_const_spec(s.shape) for s in out_shape[1:]]
    if emit_v:
        out_shape.append(jax.ShapeDtypeStruct((R, D_BR), F32))
        out_specs.append(_const_spec((R, D_BR)))
    extra = []
    if route:
        extra = list(router)
        in_specs += [_const_spec(a.shape) for a in extra]
        out_shape += [jax.ShapeDtypeStruct((nsteps * R, LANES), I32),
                      jax.ShapeDtypeStruct((nsteps * R, LANES), F32)]
        out_specs += [pl.BlockSpec((R, LANES), lambda i: (i, 0))] * 2
    SB5 = max(B, min(R, 256))
    scratch = [
        pltpu.VMEM((R, D_MODEL), BF16),
        pltpu.VMEM((R + (CONV_W - 1) * B, D_BR), F32),
        pltpu.VMEM((R + POOL_BUF * B, D_BR), F32),
        pltpu.VMEM((B, D_BR), F32),
        pltpu.VMEM((B, S5_W), F32), pltpu.VMEM((B, S5_W), F32),
        pltpu.VMEM((R, D_BR), F32),
        pltpu.VMEM((2, R, LANES), F32),
        pltpu.VMEM((2, R, LANES), F32),
        pltpu.VMEM((R, D_BR), F32),
        pltpu.VMEM((R, D_BR), F32), pltpu.VMEM((R, D_BR), F32),
        pltpu.VMEM((R, D_BR), BF16), pltpu.VMEM((R, D_BR), BF16),
        pltpu.VMEM((SB5, 2 * S5_W), F32),
    ]
    return pl.pallas_call(
        functools.partial(_mixer_body, cfg),
        grid=(nsteps,),
        in_specs=in_specs,
        out_specs=out_specs,
        out_shape=out_shape,
        scratch_shapes=scratch,
        input_output_aliases={0: 0},
        compiler_params=pltpu.CompilerParams(dimension_semantics=("arbitrary",),
                                             vmem_limit_bytes=60 * 1024 * 1024),
        name="mixer_b%d" % B,
    )(h_all, *small, *params, *extra)


def _s5prep_body(are_ref, aim_ref, ldt_ref, bre_ref, bim_ref, lbr_ref, lbi_ref, bbr_ref, bbi_ref):
    a_re, a_im = are_ref[...], aim_ref[...]
    dt = jnp.exp(ldt_ref[...])
    mag = jnp.exp(a_re * dt)
    lb_re = mag * jnp.cos(a_im * dt)
    lb_im = mag * jnp.sin(a_im * dt)
    den = a_re * a_re + a_im * a_im
    n_re = lb_re - 1.0
    q_re = (n_re * a_re + lb_im * a_im) / den
    q_im = (lb_im * a_re - n_re * a_im) / den
    lbr_ref[...] = lb_re
    lbi_ref[...] = lb_im
    bbr_ref[...] = q_re * bre_ref[...] - q_im * bim_ref[...]
    bbi_ref[...] = q_re * bim_ref[...] + q_im * bre_ref[...]


def _s5prep(a_re, a_im, log_dt, b_re, b_im):
    depth = a_re.shape[0]
    rows = depth * N_GROUPS_C * GROUP_C

    def rep(x):
        return jnp.broadcast_to(x[:, :, None, :], (depth, N_GROUPS_C, GROUP_C, P_STATE)).reshape(rows, P_STATE)

    ldt = jnp.broadcast_to(log_dt[:, :, None, None], (depth, N_GROUPS_C, GROUP_C, P_STATE)).reshape(rows, P_STATE)
    b_re_t = jnp.transpose(b_re, (0, 1, 3, 2)).reshape(rows, P_STATE)
    b_im_t = jnp.transpose(b_im, (0, 1, 3, 2)).reshape(rows, P_STATE)
    shp = jax.ShapeDtypeStruct((rows, P_STATE), F32)
    lbr, lbi, bbr, bbi = pl.pallas_call(_s5prep_body, out_shape=[shp] * 4, name="s5prep")(
        rep(a_re), rep(a_im), ldt, b_re_t, b_im_t)
    r4 = lambda x: x.reshape(depth, N_GROUPS_C, GROUP_C, P_STATE)
    return r4(lbr)[:, :, 0, :], r4(lbi)[:, :, 0, :], r4(bbr), r4(bbi)


def _ffn_body(chunks, x_ref, g_ref, w1_ref, w3_ref, w2_ref, o_ref):
    x = x_ref[...]
    xn = _rms(x, g_ref[...]).astype(BF16)
    acc = None
    for c0, cw in chunks:
        a = _dot(xn, w1_ref[:, c0:c0 + cw])
        b = _dot(xn, w3_ref[:, c0:c0 + cw])
        t = _dot((a * _sigmoid(a) * b).astype(BF16), w2_ref[c0:c0 + cw, :])
        acc = t if acc is None else acc + t
    o_ref[...] = x + acc


def _ffn_call(h, g, w1, w3, w2, tm, tf):
    dff = w1.shape[1]
    chunks = tuple((c0, min(tf, dff - c0)) for c0 in range(0, dff, tf))
    return pl.pallas_call(
        functools.partial(_ffn_body, chunks),
        grid=(h.shape[0] // tm,),
        in_specs=[pl.BlockSpec((tm, D_MODEL), lambda i: (i, 0)),
                  _const_spec((1, D_MODEL)),
                  _const_spec((D_MODEL, dff), single=True),
                  _const_spec((D_MODEL, dff), single=True),
                  _const_spec((dff, D_MODEL), single=True)],
        out_specs=pl.BlockSpec((tm, D_MODEL), lambda i: (i, 0)),
        out_shape=jax.ShapeDtypeStruct(h.shape, F32),
        input_output_aliases={0: 0},
        compiler_params=pltpu.CompilerParams(dimension_semantics=("arbitrary",),
                                             vmem_limit_bytes=56 * 1024 * 1024),
        name="ffn",
    )(h, g, w1, w3, w2)


def _row_copy_wait(hbm_ref, sem, nrows):
    pltpu.make_async_copy(hbm_ref.at[pl.ds(0, nrows)], hbm_ref.at[pl.ds(0, nrows)], sem).wait()


def _dispatch_body(tm, pad_rows, pos_ref, ends_ref, x_ref, g_ref, xs_ref, buf, zbuf, sem):
    @pl.when(pl.program_id(0) == 0)
    def _():
        zbuf[...] = jnp.zeros_like(zbuf)
        n_rows = xs_ref.shape[0]
        for e in range(N_EXPERTS):
            for start in (jnp.maximum(ends_ref[e] - pad_rows, 0), n_rows - (e + 1) * pad_rows):
                cp = pltpu.make_async_copy(zbuf, xs_ref.at[pl.ds(start, pad_rows)], sem.at[1])
                cp.start()
                cp.wait()

    xn = _rms(x_ref[...], g_ref[...])
    for k in range(ROW_TILE):
        buf[pl.ds(k, tm, stride=ROW_TILE), :] = xn[:, k * LANES:(k + 1) * LANES]

    def issue(r, c):
        src = buf.at[pl.ds(pl.multiple_of(r * ROW_TILE, ROW_TILE), ROW_TILE), :]
        pltpu.make_async_copy(src, xs_ref.at[pos_ref[0, 0, 2 * r]], sem.at[0]).start(priority=0)
        pltpu.make_async_copy(src, xs_ref.at[pos_ref[0, 0, 2 * r + 1]], sem.at[0]).start(priority=1)
        return c

    lax.fori_loop(0, tm, issue, 0)
    _row_copy_wait(xs_ref, sem.at[0], 2 * tm)


def _dispatch_call(h, g, pos, ends, n_rows, pad_rows, tm):
    n = h.shape[0]
    nt = n // tm
    return pl.pallas_call(
        functools.partial(_dispatch_body, tm, pad_rows),
        grid=(nt,),
        in_specs=[pl.BlockSpec((1, 1, 2 * tm), lambda i: (i, 0, 0), memory_space=pltpu.SMEM),
                  pl.BlockSpec(memory_space=pltpu.SMEM),
                  pl.BlockSpec((tm, D_MODEL), lambda i: (i, 0)),
                  _const_spec((1, D_MODEL))],
        out_specs=pl.BlockSpec(memory_space=pl.ANY),
        out_shape=jax.ShapeDtypeStruct((n_rows, ROW_TILE, LANES), F32),
        scratch_shapes=[pltpu.VMEM((tm * ROW_TILE, LANES), F32),
                        pltpu.VMEM((pad_rows, ROW_TILE, LANES), F32),
                        pltpu.SemaphoreType.DMA((2,))],
        compiler_params=pltpu.CompilerParams(dimension_semantics=("arbitrary",)),
        name="dispatch",
    )(pos.reshape(nt, 1, 2 * tm), ends, h, g)


def _experts_body(tm, nf, tf, te_ref, tv_ref, tfirst_ref, xs_ref, w1_hbm, w3_hbm, w2_hbm, y_ref,
                  x_s, c1_s, c3_s, c2_s, s1, s3, s2, sem):
    t = pl.program_id(0)
    e = te_ref[t]

    def chunk_copies(f, slot):
        cols = pl.ds(f * tf, tf)
        return (pltpu.make_async_copy(w1_hbm.at[e, :, cols], s1.at[slot], sem.at[slot, 0]),
                pltpu.make_async_copy(w3_hbm.at[e, :, cols], s3.at[slot], sem.at[slot, 1]),
                pltpu.make_async_copy(w2_hbm.at[e, cols, :], s2.at[slot], sem.at[slot, 2]))

    def load_rows():
        for k in range(ROW_TILE):
            x_s[:, k * LANES:(k + 1) * LANES] = xs_ref[pl.ds(k, tm, stride=ROW_TILE), :].astype(BF16)
        return x_s[...]

    def chunk_out(x, f):
        a = _dot(x, c1_s[f])
        b = _dot(x, c3_s[f])
        return _dot((a * _sigmoid(a) * b).astype(BF16), c2_s[f])

    def store_rows(acc):
        for k in range(ROW_TILE):
            y_ref[pl.ds(k, tm, stride=ROW_TILE), :] = acc[:, k * LANES:(k + 1) * LANES]

    @pl.when(tfirst_ref[t] > 0)
    def _():
        for cp in chunk_copies(0, 0):
            cp.start()
        x = load_rows()
        acc = None
        for f in range(nf):
            slot = f % 2
            if f + 1 < nf:
                for cp in chunk_copies(f + 1, 1 - slot):
                    cp.start()
            for cp in chunk_copies(f, slot):
                cp.wait()
            c1_s[f] = s1[slot].astype(BF16)
            c3_s[f] = s3[slot].astype(BF16)
            c2_s[f] = s2[slot].astype(BF16)
            term = chunk_out(x, f)
            acc = term if acc is None else acc + term
        store_rows(acc)

    @pl.when(jnp.logical_and(tv_ref[t] > 0, tfirst_ref[t] == 0))
    def _():
        x = load_rows()
        acc = None
        for f in range(nf):
            term = chunk_out(x, f)
            acc = term if acc is None else acc + term
        store_rows(acc)

    @pl.when(tv_ref[t] == 0)
    def _():
        y_ref[...] = jnp.zeros_like(y_ref)


def _experts_call(xs2d, tile_expert, tile_valid, tile_first, w1, w3, w2, tm, tf):
    rows = xs2d.shape[0] // ROW_TILE
    nt = rows // tm
    dff = w1.shape[2]
    nf = dff // tf
    grid_spec = pltpu.PrefetchScalarGridSpec(
        num_scalar_prefetch=3,
        grid=(nt,),
        in_specs=[pl.BlockSpec((tm * ROW_TILE, LANES), lambda t, te, tv, t1: (t, 0)),
                  pl.BlockSpec(memory_space=pl.ANY), pl.BlockSpec(memory_space=pl.ANY),
                  pl.BlockSpec(memory_space=pl.ANY)],
        out_specs=pl.BlockSpec((tm * ROW_TILE, LANES), lambda t, te, tv, t1: (t, 0)),
        scratch_shapes=[pltpu.VMEM((tm, D_MODEL), BF16),
                        pltpu.VMEM((nf, D_MODEL, tf), BF16), pltpu.VMEM((nf, D_MODEL, tf), BF16),
                        pltpu.VMEM((nf, tf, D_MODEL), BF16),
                        pltpu.VMEM((2, D_MODEL, tf), F32), pltpu.VMEM((2, D_MODEL, tf), F32),
                        pltpu.VMEM((2, tf, D_MODEL), F32),
                        pltpu.SemaphoreType.DMA((2, 3))])
    return pl.pallas_call(
        functools.partial(_experts_body, tm, nf, tf),
        grid_spec=grid_spec,
        out_shape=jax.ShapeDtypeStruct(xs2d.shape, F32),
        compiler_params=pltpu.CompilerParams(dimension_semantics=("arbitrary",),
                                             vmem_limit_bytes=56 * 1024 * 1024),
        name="experts",
    )(tile_expert, tile_valid, tile_first, xs2d, w1, w3, w2)


def _combine_body(tm, final, pos_ref, h_ref, gate_ref, *rest):
    if final:
        ng_ref, y_ref, o_ref, buf, sem = rest
    else:
        y_ref, o_ref, buf, sem = rest

    def issue(r, c):
        for j in range(2):
            dst = buf.at[j, pl.ds(pl.multiple_of(r * ROW_TILE, ROW_TILE), ROW_TILE), :]
            pltpu.make_async_copy(y_ref.at[pos_ref[0, 0, 2 * r + j]], dst, sem.at[0]).start(priority=j)
        return c

    lax.fori_loop(0, tm, issue, 0)
    _row_copy_wait(y_ref, sem.at[0], 2 * tm)
    g0 = gate_ref[:, 0:1]
    g1 = gate_ref[:, 1:2]
    for k in range(ROW_TILE):
        cols = slice(k * LANES, (k + 1) * LANES)
        moe = (g0 * buf[0, pl.ds(k, tm, stride=ROW_TILE), :]
               + g1 * buf[1, pl.ds(k, tm, stride=ROW_TILE), :])
        o_ref[:, cols] = h_ref[:, cols] + moe
    if final:
        o_ref[...] = _rms(o_ref[...], ng_ref[...])


def _combine_call(h, gates, pos, y3, tm, final_g=None):
    n = h.shape[0]
    nt = n // tm
    final = final_g is not None
    in_specs = [pl.BlockSpec((1, 1, 2 * tm), lambda i: (i, 0, 0), memory_space=pltpu.SMEM),
                pl.BlockSpec((tm, D_MODEL), lambda i: (i, 0)),
                pl.BlockSpec((tm, LANES), lambda i: (i, 0))]
    args = [pos.reshape(nt, 1, 2 * tm), h, gates]
    if final:
        in_specs.append(_const_spec((1, D_MODEL)))
        args.append(final_g)
    in_specs.append(pl.BlockSpec(memory_space=pl.ANY))
    args.append(y3)
    return pl.pallas_call(
        functools.partial(_combine_body, tm, final),
        grid=(nt,),
        in_specs=in_specs,
        out_specs=pl.BlockSpec((tm, D_MODEL), lambda i: (i, 0)),
        out_shape=jax.ShapeDtypeStruct((n, D_MODEL), F32),
        scratch_shapes=[pltpu.VMEM((2, tm * ROW_TILE, LANES), F32), pltpu.SemaphoreType.DMA((1,))],
        compiler_params=pltpu.CompilerParams(dimension_semantics=("arbitrary",)),
        name="combine",
    )(*args)


def _to_tm_body(nb, tt, x_ref, tail_ref, o_ref, slab):
    last = pl.program_id(0) == pl.num_programs(0) - 1

    @pl.when(jnp.logical_not(last))
    def _():
        for b in range(nb):
            for k in range(ROW_TILE):
                slab[k, pl.ds(b, tt, stride=nb), :] = x_ref[b, :, k * LANES:(k + 1) * LANES]
        for k in range(ROW_TILE):
            o_ref[:, k * LANES:(k + 1) * LANES] = slab[k]

    @pl.when(last)
    def _():
        o_ref[0:tail_ref.shape[0], :] = tail_ref[...]


def _from_tm_body(nb, tt, x_ref, o_ref, slab):
    for k in range(ROW_TILE):
        slab[k] = x_ref[:, k * LANES:(k + 1) * LANES]
    for b in range(nb):
        for k in range(ROW_TILE):
            o_ref[b, :, k * LANES:(k + 1) * LANES] = slab[k, pl.ds(b, tt, stride=nb), :]


def _to_tm_call(x, tail, tt):
    nb, t, d = x.shape
    ns = tail.shape[0]
    assert ns <= tt * nb
    nsteps = t // tt
    return pl.pallas_call(
        functools.partial(_to_tm_body, nb, tt),
        grid=(nsteps + 1,),
        in_specs=[pl.BlockSpec((nb, tt, d), lambda i: (0, jnp.minimum(i, nsteps - 1), 0)),
                  _const_spec((ns, d))],
        out_specs=pl.BlockSpec((tt * nb, d), lambda i: (i, 0)),
        out_shape=jax.ShapeDtypeStruct((nb * t + ns, d), F32),
        scratch_shapes=[pltpu.VMEM((ROW_TILE, tt * nb, LANES), F32)],
        compiler_params=pltpu.CompilerParams(dimension_semantics=("arbitrary",)),
        name="to_tm",
    )(x, tail)


def _from_tm_call(h, nb, t, tt):
    d = h.shape[1]
    return pl.pallas_call(
        functools.partial(_from_tm_body, nb, tt),
        grid=(t // tt,),
        in_specs=[pl.BlockSpec((tt * nb, d), lambda i: (i, 0))],
        out_specs=pl.BlockSpec((nb, tt, d), lambda i: (0, i, 0)),
        out_shape=jax.ShapeDtypeStruct((nb, t, d), F32),
        scratch_shapes=[pltpu.VMEM((ROW_TILE, tt * nb, LANES), F32)],
        compiler_params=pltpu.CompilerParams(dimension_semantics=("arbitrary",)),
        name="from_tm",
    )(h)


def _rms_body(x_ref, g_ref, o_ref):
    o_ref[...] = _rms(x_ref[...], g_ref[...])


def _rms_call(h, g, tm):
    n = h.shape[0]
    return pl.pallas_call(
        _rms_body, grid=(n // tm,),
        in_specs=[pl.BlockSpec((tm, D_MODEL), lambda i: (i, 0)), _const_spec((1, D_MODEL))],
        out_specs=pl.BlockSpec((tm, D_MODEL), lambda i: (i, 0)),
        out_shape=jax.ShapeDtypeStruct((n, D_MODEL), F32),
        name="rms",
    )(h, g)


MOE_TM = 512
MOE_TF = 512
TOK_TM = 768
FFN_TF = 512


def _moe_layer(h, norm_g, idx, gates, w1, w3, w2, e0, final_g):
    n = h.shape[0]
    e_flat = idx[:, 0:2].reshape(2 * n)
    onehot = (e_flat[:, None] == jnp.arange(N_EXPERTS, dtype=I32)[None, :]).astype(I32)
    csum = jnp.cumsum(onehot, axis=0)
    rank = jnp.sum((csum - onehot) * onehot, axis=1)
    counts = csum[-1]
    padded = ((counts + MOE_TM - 1) // MOE_TM) * MOE_TM
    ends = jnp.cumsum(padded)
    starts = ends - padded
    pos = (jnp.sum(onehot * starts[None, :], axis=1) + rank).astype(I32)
    n_tiles = (2 * n) // MOE_TM + N_EXPERTS
    tile_start = jnp.arange(n_tiles, dtype=I32) * MOE_TM
    tile_valid = (tile_start < ends[-1]).astype(I32)
    tile_expert = jnp.minimum(jnp.sum((tile_start[:, None] >= ends[None, :]).astype(I32), axis=1),
                              N_EXPERTS - 1).astype(I32)
    last_valid = jnp.max(jnp.where(tile_valid > 0, tile_expert, 0))
    tile_expert = jnp.where(tile_valid > 0, tile_expert, last_valid)
    xs = _dispatch_call(h, norm_g, pos, ends.astype(I32), n_tiles * MOE_TM, MOE_TM, TOK_TM)
    prev_expert = jnp.concatenate([jnp.full((1,), -1, I32), tile_expert[:-1]])
    tile_first = jnp.logical_and(tile_expert != prev_expert, tile_valid > 0).astype(I32)
    y2d = _experts_call(xs.reshape(n_tiles * MOE_TM * ROW_TILE, LANES), tile_expert + e0, tile_valid, tile_first,
                        w1, w3, w2, MOE_TM, MOE_TF)
    return _combine_call(h, gates, pos, y2d.reshape(n_tiles * MOE_TM, ROW_TILE, LANES), TOK_TM, final_g)


def _block_diag(w):
    n, k, a, b = w.shape
    eye = jnp.eye(k, dtype=w.dtype)
    return jnp.einsum('lkab,kj->lkajb', w, eye).reshape(n, k * a, k * b)


def kernel(x_prompt, x_sample, state_conv, state_lru, state_s5_re, state_s5_im, state_pool, norm_mix_g, w_in, b_in, gmlp_ln_g, gmlp_ln_b, gmlp_w_s, gmlp_b_s, conv_w, conv_b, lru_w_a, lru_b_a, lru_w_x, lru_b_x, lru_lam, s5_a_re, s5_a_im, s5_log_dt, s5_b_re, s5_b_im, s5_c_re, s5_c_im, s5_d, s5_w_glu, s5_b_glu, pool_w, pool_scale, w_branch, w_out, norm_ffn_g, ffn_w1, ffn_w3, ffn_w2, router_w, moe_w1, moe_w3, moe_w2, norm_final_g):
    depth = w_in.shape[0]
    bp, tp, _ = x_prompt.shape
    bs, ts, _ = x_sample.shape
    n_p, n_s = bp * tp, bs * ts
    assert n_p % (CHUNK * bp) == 0 and n_p % n_s == 0

    h = _to_tm_call(x_prompt, jnp.transpose(x_sample, (1, 0, 2)).reshape(n_s, D_MODEL), CHUNK)

    lbr, lbi, bbr, bbi = _s5prep(s5_a_re, s5_a_im, s5_log_dt, s5_b_re, s5_b_im)
    lane_head = jnp.arange(D_BR) // HEAD_DIM

    rows = lambda x: x.reshape(depth, 1, -1)
    bf = lambda x: x.astype(BF16)
    common_tail = (
        conv_w, rows(conv_b),
        bf(_block_diag(lru_w_a)), rows(lru_b_a), bf(_block_diag(lru_w_x)), rows(lru_b_x), rows(lru_lam),
        rows(lbr), rows(lbi),
        bf(jnp.concatenate([_block_diag(bbr), _block_diag(bbi)], axis=2)),
        bf(_block_diag(jnp.transpose(s5_c_re, (0, 1, 3, 2)))),
        bf(_block_diag(jnp.transpose(s5_c_im, (0, 1, 3, 2)))), rows(s5_d),
        bf(s5_w_glu), rows(s5_b_glu),
        bf(_block_diag(pool_w)), rows(pool_scale),
        bf(w_branch.reshape(depth, N_BRANCH * D_BR, D_MODEL)), bf(w_out))
    common_head = (rows(norm_mix_g), bf(w_in), rows(b_in), rows(gmlp_ln_g), rows(gmlp_ln_b))
    bsm_p = jnp.transpose(gmlp_b_s[:, lane_head, :], (0, 2, 1))
    lp_p = common_head + (gmlp_w_s, bsm_p) + common_tail
    ws_small = jnp.transpose(gmlp_w_s[:, :, :ts, :ts], (0, 2, 3, 1))[..., lane_head].reshape(depth, ts * ts, D_BR)
    bsm_s = jnp.transpose(gmlp_b_s[:, lane_head, :ts], (0, 2, 1))
    lp_s = common_head + (ws_small, bsm_s) + common_tail
    st_s = (jnp.transpose(state_conv, (0, 2, 1, 3)).reshape(depth, (CONV_W - 1) * bs, D_BR),
            state_lru, state_s5_re.reshape(depth, bs, S5_W), state_s5_im.reshape(depth, bs, S5_W),
            jnp.transpose(state_pool, (0, 2, 1, 3)).reshape(depth, POOL_BUF * bs, D_BR))
    row = lambda x: x.reshape(1, -1)

    conv_p, lru_p, sre_p, sim_p, pool_p = [], [], [], [], []
    conv_s, lru_s, sre_s, sim_s, pool_s, v_s = [], [], [], [], [], []
    zeros_p = (jnp.zeros(((CONV_W - 1) * bp, D_BR), F32), jnp.zeros((bp, D_BR), F32),
               jnp.zeros((bp, S5_W), F32), jnp.zeros((bp, S5_W), F32),
               jnp.zeros((POOL_BUF * bp, D_BR), F32))
    for l in range(depth):
        router = (row(norm_ffn_g[l]), jnp.transpose(router_w[l // 2])) if l % 2 == 1 else None
        outs_p = _mixer_call(h, 0, tp // CHUNK, bp, CHUNK, 0, True, False, zeros_p, None, lp_p, l, router)
        outs_s = _mixer_call(outs_p[0], n_p // n_s, 1, bs, ts, PAST_LEN, False, True, st_s, l, lp_s, l, router)
        h = outs_s[0]
        for lst, o in zip((conv_p, lru_p, sre_p, sim_p, pool_p), outs_p[1:6]):
            lst.append(o)
        for lst, o in zip((conv_s, lru_s, sre_s, sim_s, pool_s, v_s), outs_s[1:7]):
            lst.append(o)

        if l % 2 == 0:
            k = l // 2
            h = _ffn_call(h, row(norm_ffn_g[l]), ffn_w1[k].astype(BF16), ffn_w3[k].astype(BF16),
                          ffn_w2[k].astype(BF16), TOK_TM, FFN_TF)
            if l == depth - 1:
                h = _rms_call(h, row(norm_final_g), TOK_TM)
        else:
            k = l // 2
            idx = jnp.concatenate([outs_p[-2], outs_s[-2]], axis=0)
            gates = jnp.concatenate([outs_p[-1], outs_s[-1]], axis=0)
            stk = lambda w: w.reshape((-1,) + w.shape[2:])
            h = _moe_layer(h, row(norm_ffn_g[l]), idx, gates, stk(moe_w1), stk(moe_w3), stk(moe_w2),
                           k * N_EXPERTS, row(norm_final_g) if l == depth - 1 else None)

    def tm_to_bm(x, t, b):
        return jnp.transpose(x.reshape(t, b, x.shape[-1]), (1, 0, 2))

    y_prompt = _from_tm_call(h, bp, tp, CHUNK)
    y_sample = tm_to_bm(h[n_p:], ts, bs)
    stack = lambda lst, f: jnp.stack([f(o) for o in lst])
    return (
        y_prompt, y_sample,
        stack(conv_p, lambda o: tm_to_bm(o, CONV_W - 1, bp)),
        stack(lru_p, lambda o: o),
        stack(sre_p, lambda o: o.reshape(bp, N_GROUPS_C, P_STATE)),
        stack(sim_p, lambda o: o.reshape(bp, N_GROUPS_C, P_STATE)),
        stack(pool_p, lambda o: tm_to_bm(o, POOL_BUF, bp)),
        stack(conv_s, lambda o: tm_to_bm(o, CONV_W - 1, bs)),
        stack(lru_s, lambda o: o),
        stack(sre_s, lambda o: o.reshape(bs, N_GROUPS_C, P_STATE)),
        stack(sim_s, lambda o: o.reshape(bs, N_GROUPS_C, P_STATE)),
        stack(pool_s, lambda o: tm_to_bm(o, POOL_BUF, bs)),
        stack(v_s, lambda o: tm_to_bm(o, ts, bs)),
    )
```

```python
import functools
import math

import jax
import jax.numpy as jnp
from jax import lax
from jax.experimental import pallas as pl
from jax.experimental.pallas import tpu as pltpu

F32 = jnp.float32
BF16 = jnp.bfloat16
I32 = jnp.int32

SUBLANES = 8
LANES = 128

D_MODEL = 1024
D_BR = 256
N_BRANCH = 4
N_HEADS = 4
HEAD_DIM = D_BR // N_HEADS
CHUNK = 128
CONV_W = 4
LRU_C = 8.0
N_GROUPS_C = 16
GROUP_C = 16
P_STATE = 64
S5_W = N_GROUPS_C * P_STATE
POOL_WINDOWS = (2, 4, 8, 16)
POOL_BUF = max(POOL_WINDOWS) - 1
GROUP_D = D_BR // len(POOL_WINDOWS)
COLS_A = 5 * D_BR
IN_COLS = COLS_A + N_BRANCH * D_MODEL
N_EXPERTS = 8
EPS = 1e-6
PAST_LEN = 16384
SQRT_2_OVER_PI = math.sqrt(2.0 / math.pi)
ROW_TILE = D_MODEL // LANES


def _gelu(x):
    return x * (0.5 * (1.0 + jnp.tanh(SQRT_2_OVER_PI * (x + 0.044715 * (x * x * x)))))


def _sigmoid(x):
    return 1.0 / (1.0 + jnp.exp(-x))


def _rms(x, g):
    return x * lax.rsqrt(jnp.mean(x * x, axis=-1, keepdims=True) + EPS) * g


def _dot(a, b):
    return jnp.dot(a, b, preferred_element_type=F32)


def _top2(cols, out_shape):
    m1, i1 = cols[0], jnp.zeros_like(cols[0], I32)
    m2, i2 = jnp.full_like(cols[0], -jnp.inf), jnp.zeros_like(cols[0], I32)
    for e in range(1, len(cols)):
        v = cols[e]
        gt1 = v > m1
        gt2 = v > m2
        m2 = jnp.where(gt1, m1, jnp.where(gt2, v, m2))
        i2 = jnp.where(gt1, i1, jnp.where(gt2, e, i2))
        m1 = jnp.where(gt1, v, m1)
        i1 = jnp.where(gt1, e, i1)
    ex = jnp.exp(m2 - m1)
    g1 = 1.0 / (1.0 + ex)
    g2 = ex / (1.0 + ex)
    lane = lax.broadcasted_iota(I32, out_shape, 1)
    idx = jnp.where(lane == 0, i1, jnp.where(lane == 1, i2, 0))
    gate = jnp.where(lane == 0, g1, jnp.where(lane == 1, g2, 0.0))
    return idx, gate


def _const_spec(shape, single=False):
    nd = len(shape)
    if single:
        return pl.BlockSpec(shape, lambda *_: (0,) * nd, pipeline_mode=pl.Buffered(1))
    return pl.BlockSpec(shape, lambda *_: (0,) * nd)


def _layer_spec(shape, layer, single=False):
    idx = (layer,) + (0,) * (len(shape) - 1)
    blk = (None,) + tuple(shape[1:])
    if single:
        return pl.BlockSpec(blk, lambda *_: idx, pipeline_mode=pl.Buffered(1))
    return pl.BlockSpec(blk, lambda *_: idx)


def _mixer_body(cfg, *refs):
    B, Tt, start_pos, mm_gmlp, emit_v, route = cfg
    R = Tt * B
    refs = list(refs)
    h_ref = refs.pop(0)
    (conv0_ref, lru0_ref, sre0_ref, sim0_ref, pool0_ref,
     ng_ref, win_ref, bin_ref, lng_ref, lnb_ref, ws_ref, bsm_ref,
     cw_ref, cb_ref, wa_ref, ba_ref, wx_ref, bx_ref, lam_ref,
     lbr_ref, lbi_ref, bblk_ref, cre_ref, cim_ref, d_ref, wglu_ref, bglu_ref,
     pw_ref, ps_ref, wb_ref, wo_ref) = refs[:31]
    n_in = 33 if route else 31
    if route:
        nfg_ref, rwt_ref = refs[31:33]
    n_out = 6 + int(emit_v) + 2 * int(route)
    outs = refs[n_in:n_in + n_out]
    out_ref, convo_ref, lruo_ref, sreo_ref, simo_ref, poolo_ref = outs[:6]
    if route:
        idx_ref, gate_ref = outs[-2:]
    (xn_s, cext_s, pext_s, lruh_s, sre_s, sim_s, u_s, v_s, mix_s, xc_s,
     a_s, b_s, yc_s, yd_s, s5_s) = refs[n_in + n_out:]

    i = pl.program_id(0)
    SB = min(R, 256)
    SB5 = max(B, min(R, 256))
    lane256 = lax.broadcasted_iota(I32, (1, D_BR), 1)
    ydt = yc_s.dtype

    def mm(a, w):
        return _dot(a.astype(BF16), w)

    def init_carries():
        cext_s[0:(CONV_W - 1) * B, :] = conv0_ref[...]
        pext_s[0:POOL_BUF * B, :] = pool0_ref[...]
        lruh_s[...] = lru0_ref[...]
        sre_s[...] = sre0_ref[...]
        sim_s[...] = sim0_ref[...]

    def blocks(n, size, fn):
        for k in range(n):
            fn(k * size)

    def stage_in(r0):
        rows = pl.ds(r0, SB)
        xn = _rms(h_ref[rows, :], ng_ref[...]).astype(BF16)
        xn_s[rows, :] = xn
        pa = _dot(xn, win_ref[:, 0:COLS_A]) + bin_ref[:, 0:COLS_A]
        u_s[rows, :] = _gelu(pa[:, 0:D_BR])
        gv = _gelu(pa[:, D_BR:2 * D_BR])
        mu = jnp.mean(gv, axis=-1, keepdims=True)
        var = jnp.mean(jnp.square(gv - mu), axis=-1, keepdims=True)
        v = (gv - mu) * lax.rsqrt(var + EPS) * lng_ref[...] + lnb_ref[...]
        v_s[0, rows, :] = v[:, 0:LANES]
        v_s[1, rows, :] = v[:, LANES:2 * LANES]
        cext_s[pl.ds(r0 + (CONV_W - 1) * B, SB), :] = pa[:, 2 * D_BR:3 * D_BR]
        xc_s[rows, :] = pa[:, 3 * D_BR:4 * D_BR]
        pext_s[pl.ds(r0 + POOL_BUF * B, SB), :] = pa[:, 4 * D_BR:5 * D_BR]

    def stage_gmlp():
        if mm_gmlp:
            tril = (lax.broadcasted_iota(I32, (CHUNK, CHUNK), 0)
                    >= lax.broadcasted_iota(I32, (CHUNK, CHUNK), 1))
            wm = [jnp.where(tril, ws_ref[hd], 0.0).astype(BF16) for hd in range(N_HEADS)]
            head = lane256 // HEAD_DIM
            for b in range(B):
                vb = jnp.concatenate([v_s[0, pl.ds(b, Tt, stride=B), :],
                                      v_s[1, pl.ds(b, Tt, stride=B), :]], axis=1).astype(BF16)
                mixed = bsm_ref[...]
                for hd in range(N_HEADS):
                    mixed = mixed + jnp.where(head == hd, mm(wm[hd], vb), 0.0)
                mix_s[0, pl.ds(b, Tt, stride=B), :] = mixed[:, 0:LANES]
                mix_s[1, pl.ds(b, Tt, stride=B), :] = mixed[:, LANES:2 * LANES]
        else:
            for t in range(Tt):
                for half in range(2):
                    lo = half * LANES
                    acc = jnp.broadcast_to(bsm_ref[t:t + 1, lo:lo + LANES], (B, LANES))
                    for s in range(t + 1):
                        w = ws_ref[t * Tt + s:t * Tt + s + 1, lo:lo + LANES]
                        acc = acc + w * v_s[half, s * B:(s + 1) * B, :]
                    mix_s[half, t * B:(t + 1) * B, :] = acc
        if emit_v:
            vo_ref = outs[6]
            vo_ref[:, 0:LANES] = v_s[0]
            vo_ref[:, LANES:2 * LANES] = v_s[1]

    log_sig_lam = (jnp.minimum(lam_ref[...], 0.0)
                   - jnp.log1p(jnp.exp(-jnp.abs(lam_ref[...]))))

    def stage_lru(r0):
        rows = pl.ds(r0, SB)
        conv = cb_ref[...] + cext_s[pl.ds(r0, SB), :] * cw_ref[0:1, :]
        for k in range(1, CONV_W):
            conv = conv + cext_s[pl.ds(r0 + k * B, SB), :] * cw_ref[k:k + 1, :]
        cbf = conv.astype(BF16)
        r = _sigmoid(mm(cbf, wa_ref[...]) + ba_ref[...])
        ig = _sigmoid(mm(cbf, wx_ref[...]) + bx_ref[...])
        a = jnp.exp(LRU_C * r * log_sig_lam)
        mult = jnp.sqrt(1.0 - a * a)
        if start_pos == 0:
            row = lax.broadcasted_iota(I32, (SB, D_BR), 0) + r0
            mult = jnp.where(jnp.logical_and(i == 0, row < B), 1.0, mult)
        a_s[rows, :] = a
        b_s[rows, :] = mult * ig * conv

    def lru_step(t, hprev):
        rows = pl.ds(t * B, B) if isinstance(t, int) else pl.ds(pl.multiple_of(t * B, B), B)
        hnew = a_s[rows, :] * hprev + b_s[rows, :]
        b_s[rows, :] = hnew
        return hnew

    def lru_scan():
        if Tt <= 8:
            hl = lruh_s[...]
            for t in range(Tt):
                hl = lru_step(t, hl)
        else:
            hl = lax.fori_loop(0, Tt, lru_step, lruh_s[...], unroll=8)
        lruh_s[...] = hl
        lruo_ref[...] = hl
        conv_tail = cext_s[R:R + (CONV_W - 1) * B, :]
        convo_ref[...] = conv_tail
        cext_s[0:(CONV_W - 1) * B, :] = conv_tail

    lbr = jnp.broadcast_to(lbr_ref[...], (B, S5_W))
    lbi = jnp.broadcast_to(lbi_ref[...], (B, S5_W))
    steps5 = SB5 // B

    def stage_s5(r0):
        rows = pl.ds(r0, SB5)
        xc = xc_s[rows, :]
        s5_s[...] = mm(xc, bblk_ref[...])

        def s5_step(t, carry):
            sre, sim = carry
            rr = pl.ds(t * B, B) if isinstance(t, int) else pl.ds(pl.multiple_of(t * B, B), B)
            nre = lbr * sre - lbi * sim + s5_s[rr, 0:S5_W]
            nim = lbr * sim + lbi * sre + s5_s[rr, S5_W:2 * S5_W]
            s5_s[rr, 0:S5_W] = nre
            s5_s[rr, S5_W:2 * S5_W] = nim
            return nre, nim

        carry = (sre_s[...], sim_s[...])
        for t in range(steps5):
            carry = s5_step(t, carry)
        sre_s[...] = carry[0]
        sim_s[...] = carry[1]
        y = (mm(s5_s[:, 0:S5_W], cre_ref[...])
             - mm(s5_s[:, S5_W:2 * S5_W], cim_ref[...])
             + d_ref[...] * xc)
        y = _gelu(y)
        y = y * _sigmoid(mm(y, wglu_ref[...]) + bglu_ref[...])
        yc_s[rows, :] = y.astype(ydt)

    wlane = jnp.where(lane256 < GROUP_D, POOL_WINDOWS[0],
                      jnp.where(lane256 < 2 * GROUP_D, POOL_WINDOWS[1],
                                jnp.where(lane256 < 3 * GROUP_D, POOL_WINDOWS[2], POOL_WINDOWS[3])))

    def stage_pool(r0):
        base = r0 + POOL_BUF * B
        tok = pext_s[pl.ds(base, SB), :]
        acc = tok
        sums = {}
        for j in range(1, max(POOL_WINDOWS)):
            acc = acc + pext_s[pl.ds(base - j * B, SB), :]
            if j + 1 in POOL_WINDOWS:
                sums[j + 1] = acc
        sel = jnp.where(lane256 < GROUP_D, sums[2],
                        jnp.where(lane256 < 2 * GROUP_D, sums[4],
                                  jnp.where(lane256 < 3 * GROUP_D, sums[8], sums[16])))
        if start_pos >= POOL_BUF:
            cnt = wlane.astype(F32)
        else:
            row = lax.broadcasted_iota(I32, (SB, D_BR), 0) + r0
            tpos = start_pos + i * Tt + lax.shift_right_logical(row, jnp.full_like(row, int(math.log2(B))))
            cnt = jnp.minimum(wlane, tpos + 1).astype(F32)
        diff = sel / cnt - tok
        yd_s[pl.ds(r0, SB), :] = (mm(diff, pw_ref[...]) * ps_ref[...]).astype(ydt)

    def pool_tail():
        tail = pext_s[R:R + POOL_BUF * B, :]
        poolo_ref[...] = tail
        pext_s[0:POOL_BUF * B, :] = tail

    SBM = min(R, 128)

    def stage_merge(r0):
        rows = pl.ds(r0, SBM)
        xn = xn_s[rows, :]
        ya = u_s[rows, :] * jnp.concatenate([mix_s[0, rows, :], mix_s[1, rows, :]], axis=1)
        ys = (ya, b_s[rows, :], yc_s[rows, :], yd_s[rows, :])
        merged = None
        for n in range(N_BRANCH):
            c0 = COLS_A + n * D_MODEL
            logits = _dot(xn, win_ref[:, c0:c0 + D_MODEL]) + bin_ref[:, c0:c0 + D_MODEL]
            term = _sigmoid(logits) * mm(ys[n], wb_ref[n * D_BR:(n + 1) * D_BR, :])
            merged = term if merged is None else merged + term
        out = h_ref[rows, :] + mm(merged, wo_ref[...])
        out_ref[rows, :] = out
        if route:
            xr = _rms(out, nfg_ref[...])
            cols = [jnp.sum(xr * rwt_ref[e:e + 1, :], axis=-1, keepdims=True) for e in range(N_EXPERTS)]
            idx, gate = _top2(cols, (SBM, LANES))
            idx_ref[rows, :] = idx
            gate_ref[rows, :] = gate

    def stage_front(r0):
        stage_in(r0)
        stage_lru(r0)
        stage_pool(r0)

    def sequence_stages():
        blocks(R // SB, SB, stage_front)
        stage_gmlp()
        lru_scan()
        pool_tail()

    def merge_rows(r0):
        for j in range(SB5 // SBM):
            rj = r0 + j * SBM
            stage_merge(rj if isinstance(rj, int) else pl.multiple_of(rj, SBM))

    nb5 = R // SB5
    pl.when(i == 0)(init_carries)
    sequence_stages()
    stage_s5(0)
    for k in range(1, nb5):
        stage_s5(k * SB5)
        merge_rows((k - 1) * SB5)
    merge_rows((nb5 - 1) * SB5)
    sreo_ref[...] = sre_s[...]
    simo_ref[...] = sim_s[...]


def _mixer_call(h_all, blk0, nsteps, B, Tt, start_pos, mm_gmlp, emit_v, states, state_layer, lp, layer,
                router=None):
    R = Tt * B
    route = router is not None
    cfg = (B, Tt, start_pos, mm_gmlp, emit_v, route)
    small = list(states)
    params = list(lp)
    big = {1, 16, 24, 25}
    h_spec = pl.BlockSpec((R, D_MODEL), lambda i: (blk0 + i, 0))
    in_specs = [h_spec]
    if state_layer is None:
        in_specs += [_const_spec(a.shape) for a in small]
        st_shapes = [a.shape for a in small]
    else:
        in_specs += [_layer_spec(a.shape, state_layer) for a in small]
        st_shapes = [a.shape[1:] for a in small]
    in_specs += [_layer_spec(a.shape, layer, single=(k in big)) for k, a in enumerate(params)]
    out_shape = [jax.ShapeDtypeStruct(h_all.shape, F32)] + [jax.ShapeDtypeStruct(s, F32) for s in st_shapes]
    out_specs = [h_spec] + [_const_spec(s.shape) for s in out_shape[1:]]
    if emit_v:
        out_shape.append(jax.ShapeDtypeStruct((R, D_BR), F32))
        out_specs.append(_const_spec((R, D_BR)))
    extra = []
    if route:
        extra = list(router)
        in_specs += [_const_spec(a.shape) for a in extra]
        out_shape += [jax.ShapeDtypeStruct((nsteps * R, LANES), I32),
                      jax.ShapeDtypeStruct((nsteps * R, LANES), F32)]
        out_specs += [pl.BlockSpec((R, LANES), lambda i: (i, 0))] * 2
    SB5 = max(B, min(R, 256))
    scratch = [
        pltpu.VMEM((R, D_MODEL), BF16),
        pltpu.VMEM((R + (CONV_W - 1) * B, D_BR), F32),
        pltpu.VMEM((R + POOL_BUF * B, D_BR), F32),
        pltpu.VMEM((B, D_BR), F32),
        pltpu.VMEM((B, S5_W), F32), pltpu.VMEM((B, S5_W), F32),
        pltpu.VMEM((R, D_BR), F32),
        pltpu.VMEM((2, R, LANES), F32),
        pltpu.VMEM((2, R, LANES), F32),
        pltpu.VMEM((R, D_BR), F32),
        pltpu.VMEM((R, D_BR), F32), pltpu.VMEM((R, D_BR), F32),
        pltpu.VMEM((R, D_BR), BF16), pltpu.VMEM((R, D_BR), BF16),
        pltpu.VMEM((SB5, 2 * S5_W), F32),
    ]
    return pl.pallas_call(
        functools.partial(_mixer_body, cfg),
        grid=(nsteps,),
        in_specs=in_specs,
        out_specs=out_specs,
        out_shape=out_shape,
        scratch_shapes=scratch,
        input_output_aliases={0: 0},
        compiler_params=pltpu.CompilerParams(dimension_semantics=("arbitrary",),
                                             vmem_limit_bytes=60 * 1024 * 1024),
        name="mixer_b%d" % B,
    )(h_all, *small, *params, *extra)


def _s5prep_body(are_ref, aim_ref, ldt_ref, bre_ref, bim_ref, lbr_ref, lbi_ref, bbr_ref, bbi_ref):
    a_re, a_im = are_ref[...], aim_ref[...]
    dt = jnp.exp(ldt_ref[...])
    mag = jnp.exp(a_re * dt)
    lb_re = mag * jnp.cos(a_im * dt)
    lb_im = mag * jnp.sin(a_im * dt)
    den = a_re * a_re + a_im * a_im
    n_re = lb_re - 1.0
    q_re = (n_re * a_re + lb_im * a_im) / den
    q_im = (lb_im * a_re - n_re * a_im) / den
    lbr_ref[...] = lb_re
    lbi_ref[...] = lb_im
    bbr_ref[...] = q_re * bre_ref[...] - q_im * bim_ref[...]
    bbi_ref[...] = q_re * bim_ref[...] + q_im * bre_ref[...]


def _s5prep(a_re, a_im, log_dt, b_re, b_im):
    depth = a_re.shape[0]
    rows = depth * N_GROUPS_C * GROUP_C

    def rep(x):
        return jnp.broadcast_to(x[:, :, None, :], (depth, N_GROUPS_C, GROUP_C, P_STATE)).reshape(rows, P_STATE)

    ldt = jnp.broadcast_to(log_dt[:, :, None, None], (depth, N_GROUPS_C, GROUP_C, P_STATE)).reshape(rows, P_STATE)
    b_re_t = jnp.transpose(b_re, (0, 1, 3, 2)).reshape(rows, P_STATE)
    b_im_t = jnp.transpose(b_im, (0, 1, 3, 2)).reshape(rows, P_STATE)
    shp = jax.ShapeDtypeStruct((rows, P_STATE), F32)
    lbr, lbi, bbr, bbi = pl.pallas_call(_s5prep_body, out_shape=[shp] * 4, name="s5prep")(
        rep(a_re), rep(a_im), ldt, b_re_t, b_im_t)
    r4 = lambda x: x.reshape(depth, N_GROUPS_C, GROUP_C, P_STATE)
    return r4(lbr)[:, :, 0, :], r4(lbi)[:, :, 0, :], r4(bbr), r4(bbi)


def _ffn_body(chunks, x_ref, g_ref, w1_ref, w3_ref, w2_ref, o_ref):
    x = x_ref[...]
    xn = _rms(x, g_ref[...]).astype(BF16)
    acc = None
    for c0, cw in chunks:
        a = _dot(xn, w1_ref[:, c0:c0 + cw])
        b = _dot(xn, w3_ref[:, c0:c0 + cw])
        t = _dot((a * _sigmoid(a) * b).astype(BF16), w2_ref[c0:c0 + cw, :])
        acc = t if acc is None else acc + t
    o_ref[...] = x + acc


def _ffn_call(h, g, w1, w3, w2, tm, tf):
    dff = w1.shape[1]
    chunks = tuple((c0, min(tf, dff - c0)) for c0 in range(0, dff, tf))
    return pl.pallas_call(
        functools.partial(_ffn_body, chunks),
        grid=(h.shape[0] // tm,),
        in_specs=[pl.BlockSpec((tm, D_MODEL), lambda i: (i, 0)),
                  _const_spec((1, D_MODEL)),
                  _const_spec((D_MODEL, dff), single=True),
                  _const_spec((D_MODEL, dff), single=True),
                  _const_spec((dff, D_MODEL), single=True)],
        out_specs=pl.BlockSpec((tm, D_MODEL), lambda i: (i, 0)),
        out_shape=jax.ShapeDtypeStruct(h.shape, F32),
        input_output_aliases={0: 0},
        compiler_params=pltpu.CompilerParams(dimension_semantics=("arbitrary",),
                                             vmem_limit_bytes=56 * 1024 * 1024),
        name="ffn",
    )(h, g, w1, w3, w2)


def _row_copy_wait(hbm_ref, sem, nrows):
    pltpu.make_async_copy(hbm_ref.at[pl.ds(0, nrows)], hbm_ref.at[pl.ds(0, nrows)], sem).wait()


def _dispatch_body(tm, pad_rows, pos_ref, ends_ref, x_ref, g_ref, xs_ref, buf, zbuf, sem):
    @pl.when(pl.program_id(0) == 0)
    def _():
        zbuf[...] = jnp.zeros_like(zbuf)
        n_rows = xs_ref.shape[0]
        for e in range(N_EXPERTS):
            for start in (jnp.maximum(ends_ref[e] - pad_rows, 0), n_rows - (e + 1) * pad_rows):
                cp = pltpu.make_async_copy(zbuf, xs_ref.at[pl.ds(start, pad_rows)], sem.at[1])
                cp.start()
                cp.wait()

    xn = _rms(x_ref[...], g_ref[...])
    for k in range(ROW_TILE):
        buf[pl.ds(k, tm, stride=ROW_TILE), :] = xn[:, k * LANES:(k + 1) * LANES]

    def issue(r, c):
        src = buf.at[pl.ds(pl.multiple_of(r * ROW_TILE, ROW_TILE), ROW_TILE), :]
        pltpu.make_async_copy(src, xs_ref.at[pos_ref[0, 0, 2 * r]], sem.at[0]).start(priority=0)
        pltpu.make_async_copy(src, xs_ref.at[pos_ref[0, 0, 2 * r + 1]], sem.at[0]).start(priority=1)
        return c

    lax.fori_loop(0, tm, issue, 0)
    _row_copy_wait(xs_ref, sem.at[0], 2 * tm)


def _dispatch_call(h, g, pos, ends, n_rows, pad_rows, tm):
    n = h.shape[0]
    nt = n // tm
    return pl.pallas_call(
        functools.partial(_dispatch_body, tm, pad_rows),
        grid=(nt,),
        in_specs=[pl.BlockSpec((1, 1, 2 * tm), lambda i: (i, 0, 0), memory_space=pltpu.SMEM),
                  pl.BlockSpec(memory_space=pltpu.SMEM),
                  pl.BlockSpec((tm, D_MODEL), lambda i: (i, 0)),
                  _const_spec((1, D_MODEL))],
        out_specs=pl.BlockSpec(memory_space=pl.ANY),
        out_shape=jax.ShapeDtypeStruct((n_rows, ROW_TILE, LANES), F32),
        scratch_shapes=[pltpu.VMEM((tm * ROW_TILE, LANES), F32),
                        pltpu.VMEM((pad_rows, ROW_TILE, LANES), F32),
                        pltpu.SemaphoreType.DMA((2,))],
        compiler_params=pltpu.CompilerParams(dimension_semantics=("arbitrary",)),
        name="dispatch",
    )(pos.reshape(nt, 1, 2 * tm), ends, h, g)


def _experts_body(tm, nf, tf, te_ref, tv_ref, tfirst_ref, xs_ref, w1_hbm, w3_hbm, w2_hbm, y_ref,
                  x_s, c1_s, c3_s, c2_s, s1, s3, s2, sem):
    t = pl.program_id(0)
    e = te_ref[t]

    def chunk_copies(f, slot):
        cols = pl.ds(f * tf, tf)
        return (pltpu.make_async_copy(w1_hbm.at[e, :, cols], s1.at[slot], sem.at[slot, 0]),
                pltpu.make_async_copy(w3_hbm.at[e, :, cols], s3.at[slot], sem.at[slot, 1]),
                pltpu.make_async_copy(w2_hbm.at[e, cols, :], s2.at[slot], sem.at[slot, 2]))

    def load_rows():
        for k in range(ROW_TILE):
            x_s[:, k * LANES:(k + 1) * LANES] = xs_ref[pl.ds(k, tm, stride=ROW_TILE), :].astype(BF16)
        return x_s[...]

    def chunk_out(x, f):
        a = _dot(x, c1_s[f])
        b = _dot(x, c3_s[f])
        return _dot((a * _sigmoid(a) * b).astype(BF16), c2_s[f])

    def store_rows(acc):
        for k in range(ROW_TILE):
            y_ref[pl.ds(k, tm, stride=ROW_TILE), :] = acc[:, k * LANES:(k + 1) * LANES]

    @pl.when(tfirst_ref[t] > 0)
    def _():
        for cp in chunk_copies(0, 0):
            cp.start()
        x = load_rows()
        acc = None
        for f in range(nf):
            slot = f % 2
            if f + 1 < nf:
                for cp in chunk_copies(f + 1, 1 - slot):
                    cp.start()
            for cp in chunk_copies(f, slot):
                cp.wait()
            c1_s[f] = s1[slot].astype(BF16)
            c3_s[f] = s3[slot].astype(BF16)
            c2_s[f] = s2[slot].astype(BF16)
            term = chunk_out(x, f)
            acc = term if acc is None else acc + term
        store_rows(acc)

    @pl.when(jnp.logical_and(tv_ref[t] > 0, tfirst_ref[t] == 0))
    def _():
        x = load_rows()
        acc = None
        for f in range(nf):
            term = chunk_out(x, f)
            acc = term if acc is None else acc + term
        store_rows(acc)

    @pl.when(tv_ref[t] == 0)
    def _():
        y_ref[...] = jnp.zeros_like(y_ref)


def _experts_call(xs2d, tile_expert, tile_valid, tile_first, w1, w3, w2, tm, tf):
    rows = xs2d.shape[0] // ROW_TILE
    nt = rows // tm
    dff = w1.shape[2]
    nf = dff // tf
    grid_spec = pltpu.PrefetchScalarGridSpec(
        num_scalar_prefetch=3,
        grid=(nt,),
        in_specs=[pl.BlockSpec((tm * ROW_TILE, LANES), lambda t, te, tv, t1: (t, 0)),
                  pl.BlockSpec(memory_space=pl.ANY), pl.BlockSpec(memory_space=pl.ANY),
                  pl.BlockSpec(memory_space=pl.ANY)],
        out_specs=pl.BlockSpec((tm * ROW_TILE, LANES), lambda t, te, tv, t1: (t, 0)),
        scratch_shapes=[pltpu.VMEM((tm, D_MODEL), BF16),
                        pltpu.VMEM((nf, D_MODEL, tf), BF16), pltpu.VMEM((nf, D_MODEL, tf), BF16),
                        pltpu.VMEM((nf, tf, D_MODEL), BF16),
                        pltpu.VMEM((2, D_MODEL, tf), F32), pltpu.VMEM((2, D_MODEL, tf), F32),
                        pltpu.VMEM((2, tf, D_MODEL), F32),
                        pltpu.SemaphoreType.DMA((2, 3))])
    return pl.pallas_call(
        functools.partial(_experts_body, tm, nf, tf),
        grid_spec=grid_spec,
        out_shape=jax.ShapeDtypeStruct(xs2d.shape, F32),
        compiler_params=pltpu.CompilerParams(dimension_semantics=("arbitrary",),
                                             vmem_limit_bytes=56 * 1024 * 1024),
        name="experts",
    )(tile_expert, tile_valid, tile_first, xs2d, w1, w3, w2)


def _combine_body(tm, final, pos_ref, h_ref, gate_ref, *rest):
    if final:
        ng_ref, y_ref, o_ref, buf, sem = rest
    else:
        y_ref, o_ref, buf, sem = rest

    def issue(r, c):
        for j in range(2):
            dst = buf.at[j, pl.ds(pl.multiple_of(r * ROW_TILE, ROW_TILE), ROW_TILE), :]
            pltpu.make_async_copy(y_ref.at[pos_ref[0, 0, 2 * r + j]], dst, sem.at[0]).start(priority=j)
        return c

    lax.fori_loop(0, tm, issue, 0)
    _row_copy_wait(y_ref, sem.at[0], 2 * tm)
    g0 = gate_ref[:, 0:1]
    g1 = gate_ref[:, 1:2]
    for k in range(ROW_TILE):
        cols = slice(k * LANES, (k + 1) * LANES)
        moe = (g0 * buf[0, pl.ds(k, tm, stride=ROW_TILE), :]
               + g1 * buf[1, pl.ds(k, tm, stride=ROW_TILE), :])
        o_ref[:, cols] = h_ref[:, cols] + moe
    if final:
        o_ref[...] = _rms(o_ref[...], ng_ref[...])


def _combine_call(h, gates, pos, y3, tm, final_g=None):
    n = h.shape[0]
    nt = n // tm
    final = final_g is not None
    in_specs = [pl.BlockSpec((1, 1, 2 * tm), lambda i: (i, 0, 0), memory_space=pltpu.SMEM),
                pl.BlockSpec((tm, D_MODEL), lambda i: (i, 0)),
                pl.BlockSpec((tm, LANES), lambda i: (i, 0))]
    args = [pos.reshape(nt, 1, 2 * tm), h, gates]
    if final:
        in_specs.append(_const_spec((1, D_MODEL)))
        args.append(final_g)
    in_specs.append(pl.BlockSpec(memory_space=pl.ANY))
    args.append(y3)
    return pl.pallas_call(
        functools.partial(_combine_body, tm, final),
        grid=(nt,),
        in_specs=in_specs,
        out_specs=pl.BlockSpec((tm, D_MODEL), lambda i: (i, 0)),
        out_shape=jax.ShapeDtypeStruct((n, D_MODEL), F32),
        scratch_shapes=[pltpu.VMEM((2, tm * ROW_TILE, LANES), F32), pltpu.SemaphoreType.DMA((1,))],
        compiler_params=pltpu.CompilerParams(dimension_semantics=("arbitrary",)),
        name="combine",
    )(*args)


def _to_tm_body(nb, tt, x_ref, tail_ref, o_ref, slab):
    last = pl.program_id(0) == pl.num_programs(0) - 1

    @pl.when(jnp.logical_not(last))
    def _():
        for b in range(nb):
            for k in range(ROW_TILE):
                slab[k, pl.ds(b, tt, stride=nb), :] = x_ref[b, :, k * LANES:(k + 1) * LANES]
        for k in range(ROW_TILE):
            o_ref[:, k * LANES:(k + 1) * LANES] = slab[k]

    @pl.when(last)
    def _():
        o_ref[0:tail_ref.shape[0], :] = tail_ref[...]


def _from_tm_body(nb, tt, x_ref, o_ref, slab):
    for k in range(ROW_TILE):
        slab[k] = x_ref[:, k * LANES:(k + 1) * LANES]
    for b in range(nb):
        for k in range(ROW_TILE):
            o_ref[b, :, k * LANES:(k + 1) * LANES] = slab[k, pl.ds(b, tt, stride=nb), :]


def _to_tm_call(x, tail, tt):
    nb, t, d = x.shape
    ns = tail.shape[0]
    assert ns <= tt * nb
    nsteps = t // tt
    return pl.pallas_call(
        functools.partial(_to_tm_body, nb, tt),
        grid=(nsteps + 1,),
        in_specs=[pl.BlockSpec((nb, tt, d), lambda i: (0, jnp.minimum(i, nsteps - 1), 0)),
                  _const_spec((ns, d))],
        out_specs=pl.BlockSpec((tt * nb, d), lambda i: (i, 0)),
        out_shape=jax.ShapeDtypeStruct((nb * t + ns, d), F32),
        scratch_shapes=[pltpu.VMEM((ROW_TILE, tt * nb, LANES), F32)],
        compiler_params=pltpu.CompilerParams(dimension_semantics=("arbitrary",)),
        name="to_tm",
    )(x, tail)


def _from_tm_call(h, nb, t, tt):
    d = h.shape[1]
    return pl.pallas_call(
        functools.partial(_from_tm_body, nb, tt),
        grid=(t // tt,),
        in_specs=[pl.BlockSpec((tt * nb, d), lambda i: (i, 0))],
        out_specs=pl.BlockSpec((nb, tt, d), lambda i: (0, i, 0)),
        out_shape=jax.ShapeDtypeStruct((nb, t, d), F32),
        scratch_shapes=[pltpu.VMEM((ROW_TILE, tt * nb, LANES), F32)],
        compiler_params=pltpu.CompilerParams(dimension_semantics=("arbitrary",)),
        name="from_tm",
    )(h)


def _rms_body(x_ref, g_ref, o_ref):
    o_ref[...] = _rms(x_ref[...], g_ref[...])


def _rms_call(h, g, tm):
    n = h.shape[0]
    return pl.pallas_call(
        _rms_body, grid=(n // tm,),
        in_specs=[pl.BlockSpec((tm, D_MODEL), lambda i: (i, 0)), _const_spec((1, D_MODEL))],
        out_specs=pl.BlockSpec((tm, D_MODEL), lambda i: (i, 0)),
        out_shape=jax.ShapeDtypeStruct((n, D_MODEL), F32),
        name="rms",
    )(h, g)


MOE_TM = 512
MOE_TF = 512
TOK_TM = 768
FFN_TF = 512


def _moe_layer(h, norm_g, idx, gates, w1, w3, w2, e0, final_g):
    n = h.shape[0]
    e_flat = idx[:, 0:2].reshape(2 * n)
    onehot = (e_flat[:, None] == jnp.arange(N_EXPERTS, dtype=I32)[None, :]).astype(I32)
    csum = jnp.cumsum(onehot, axis=0)
    rank = jnp.sum((csum - onehot) * onehot, axis=1)
    counts = csum[-1]
    padded = ((counts + MOE_TM - 1) // MOE_TM) * MOE_TM
    ends = jnp.cumsum(padded)
    starts = ends - padded
    pos = (jnp.sum(onehot * starts[None, :], axis=1) + rank).astype(I32)
    n_tiles = (2 * n) // MOE_TM + N_EXPERTS
    tile_start = jnp.arange(n_tiles, dtype=I32) * MOE_TM
    tile_valid = (tile_start < ends[-1]).astype(I32)
    tile_expert = jnp.minimum(jnp.sum((tile_start[:, None] >= ends[None, :]).astype(I32), axis=1),
                              N_EXPERTS - 1).astype(I32)
    last_valid = jnp.max(jnp.where(tile_valid > 0, tile_expert, 0))
    tile_expert = jnp.where(tile_valid > 0, tile_expert, last_valid)
    xs = _dispatch_call(h, norm_g, pos, ends.astype(I32), n_tiles * MOE_TM, MOE_TM, TOK_TM)
    prev_expert = jnp.concatenate([jnp.full((1,), -1, I32), tile_expert[:-1]])
    tile_first = jnp.logical_and(tile_expert != prev_expert, tile_valid > 0).astype(I32)
    y2d = _experts_call(xs.reshape(n_tiles * MOE_TM * ROW_TILE, LANES), tile_expert + e0, tile_valid, tile_first,
                        w1, w3, w2, MOE_TM, MOE_TF)
    return _combine_call(h, gates, pos, y2d.reshape(n_tiles * MOE_TM, ROW_TILE, LANES), TOK_TM, final_g)


def _block_diag(w):
    n, k, a, b = w.shape
    eye = jnp.eye(k, dtype=w.dtype)
    return jnp.einsum('lkab,kj->lkajb', w, eye).reshape(n, k * a, k * b)


def kernel(x_prompt, x_sample, state_conv, state_lru, state_s5_re, state_s5_im, state_pool, norm_mix_g, w_in, b_in, gmlp_ln_g, gmlp_ln_b, gmlp_w_s, gmlp_b_s, conv_w, conv_b, lru_w_a, lru_b_a, lru_w_x, lru_b_x, lru_lam, s5_a_re, s5_a_im, s5_log_dt, s5_b_re, s5_b_im, s5_c_re, s5_c_im, s5_d, s5_w_glu, s5_b_glu, pool_w, pool_scale, w_branch, w_out, norm_ffn_g, ffn_w1, ffn_w3, ffn_w2, router_w, moe_w1, moe_w3, moe_w2, norm_final_g):
    depth = w_in.shape[0]
    bp, tp, _ = x_prompt.shape
    bs, ts, _ = x_sample.shape
    n_p, n_s = bp * tp, bs * ts
    assert n_p % (CHUNK * bp) == 0 and n_p % n_s == 0

    h = _to_tm_call(x_prompt, jnp.transpose(x_sample, (1, 0, 2)).reshape(n_s, D_MODEL), CHUNK)

    lbr, lbi, bbr, bbi = _s5prep(s5_a_re, s5_a_im, s5_log_dt, s5_b_re, s5_b_im)
    lane_head = jnp.arange(D_BR) // HEAD_DIM

    rows = lambda x: x.reshape(depth, 1, -1)
    bf = lambda x: x.astype(BF16)
    common_tail = (
        conv_w, rows(conv_b),
        bf(_block_diag(lru_w_a)), rows(lru_b_a), bf(_block_diag(lru_w_x)), rows(lru_b_x), rows(lru_lam),
        rows(lbr), rows(lbi),
        bf(jnp.concatenate([_block_diag(bbr), _block_diag(bbi)], axis=2)),
        bf(_block_diag(jnp.transpose(s5_c_re, (0, 1, 3, 2)))),
        bf(_block_diag(jnp.transpose(s5_c_im, (0, 1, 3, 2)))), rows(s5_d),
        bf(s5_w_glu), rows(s5_b_glu),
        bf(_block_diag(pool_w)), rows(pool_scale),
        bf(w_branch.reshape(depth, N_BRANCH * D_BR, D_MODEL)), bf(w_out))
    common_head = (rows(norm_mix_g), bf(w_in), rows(b_in), rows(gmlp_ln_g), rows(gmlp_ln_b))
    bsm_p = jnp.transpose(gmlp_b_s[:, lane_head, :], (0, 2, 1))
    lp_p = common_head + (gmlp_w_s, bsm_p) + common_tail
    ws_small = jnp.transpose(gmlp_w_s[:, :, :ts, :ts], (0, 2, 3, 1))[..., lane_head].reshape(depth, ts * ts, D_BR)
    bsm_s = jnp.transpose(gmlp_b_s[:, lane_head, :ts], (0, 2, 1))
    lp_s = common_head + (ws_small, bsm_s) + common_tail
    st_s = (jnp.transpose(state_conv, (0, 2, 1, 3)).reshape(depth, (CONV_W - 1) * bs, D_BR),
            state_lru, state_s5_re.reshape(depth, bs, S5_W), state_s5_im.reshape(depth, bs, S5_W),
            jnp.transpose(state_pool, (0, 2, 1, 3)).reshape(depth, POOL_BUF * bs, D_BR))
    row = lambda x: x.reshape(1, -1)

    conv_p, lru_p, sre_p, sim_p, pool_p = [], [], [], [], []
    conv_s, lru_s, sre_s, sim_s, pool_s, v_s = [], [], [], [], [], []
    zeros_p = (jnp.zeros(((CONV_W - 1) * bp, D_BR), F32), jnp.zeros((bp, D_BR), F32),
               jnp.zeros((bp, S5_W), F32), jnp.zeros((bp, S5_W), F32),
               jnp.zeros((POOL_BUF * bp, D_BR), F32))
    for l in range(depth):
        router = (row(norm_ffn_g[l]), jnp.transpose(router_w[l // 2])) if l % 2 == 1 else None
        outs_p = _mixer_call(h, 0, tp // CHUNK, bp, CHUNK, 0, True, False, zeros_p, None, lp_p, l, router)
        outs_s = _mixer_call(outs_p[0], n_p // n_s, 1, bs, ts, PAST_LEN, False, True, st_s, l, lp_s, l, router)
        h = outs_s[0]
        for lst, o in zip((conv_p, lru_p, sre_p, sim_p, pool_p), outs_p[1:6]):
            lst.append(o)
        for lst, o in zip((conv_s, lru_s, sre_s, sim_s, pool_s, v_s), outs_s[1:7]):
            lst.append(o)

        if l % 2 == 0:
            k = l // 2
            h = _ffn_call(h, row(norm_ffn_g[l]), ffn_w1[k].astype(BF16), ffn_w3[k].astype(BF16),
                          ffn_w2[k].astype(BF16), TOK_TM, FFN_TF)
            if l == depth - 1:
                h = _rms_call(h, row(norm_final_g), TOK_TM)
        else:
            k = l // 2
            idx = jnp.concatenate([outs_p[-2], outs_s[-2]], axis=0)
            gates = jnp.concatenate([outs_p[-1], outs_s[-1]], axis=0)
            stk = lambda w: w.reshape((-1,) + w.shape[2:])
            h = _moe_layer(h, row(norm_ffn_g[l]), idx, gates, stk(moe_w1), stk(moe_w3), stk(moe_w2),
                           k * N_EXPERTS, row(norm_final_g) if l == depth - 1 else None)

    def tm_to_bm(x, t, b):
        return jnp.transpose(x.reshape(t, b, x.shape[-1]), (1, 0, 2))

    y_prompt = _from_tm_call(h, bp, tp, CHUNK)
    y_sample = tm_to_bm(h[n_p:], ts, bs)
    stack = lambda lst, f: jnp.stack([f(o) for o in lst])
    return (
        y_prompt, y_sample,
        stack(conv_p, lambda o: tm_to_bm(o, CONV_W - 1, bp)),
        stack(lru_p, lambda o: o),
        stack(sre_p, lambda o: o.reshape(bp, N_GROUPS_C, P_STATE)),
        stack(sim_p, lambda o: o.reshape(bp, N_GROUPS_C, P_STATE)),
        stack(pool_p, lambda o: tm_to_bm(o, POOL_BUF, bp)),
        stack(conv_s, lambda o: tm_to_bm(o, CONV_W - 1, bs)),
        stack(lru_s, lambda o: o),
        stack(sre_s, lambda o: o.reshape(bs, N_GROUPS_C, P_STATE)),
        stack(sim_s, lambda o: o.reshape(bs, N_GROUPS_C, P_STATE)),
        stack(pool_s, lambda o: tm_to_bm(o, POOL_BUF, bs)),
        stack(v_s, lambda o: tm_to_bm(o, ts, bs)),
    )
```

```python
import functools
import math

import jax
import jax.numpy as jnp
from jax import lax
from jax.experimental import pallas as pl
from jax.experimental.pallas import tpu as pltpu

F32 = jnp.float32
BF16 = jnp.bfloat16
I32 = jnp.int32

SUBLANES = 8
LANES = 128

D_MODEL = 1024
D_BR = 256
N_BRANCH = 4
N_HEADS = 4
HEAD_DIM = D_BR // N_HEADS
CHUNK = 128
CONV_W = 4
LRU_C = 8.0
N_GROUPS_C = 16
GROUP_C = 16
P_STATE = 64
S5_W = N_GROUPS_C * P_STATE
POOL_WINDOWS = (2, 4, 8, 16)
POOL_BUF = max(POOL_WINDOWS) - 1
GROUP_D = D_BR // len(POOL_WINDOWS)
COLS_A = 5 * D_BR
IN_COLS = COLS_A + N_BRANCH * D_MODEL
N_EXPERTS = 8
EPS = 1e-6
PAST_LEN = 16384
SQRT_2_OVER_PI = math.sqrt(2.0 / math.pi)
ROW_TILE = D_MODEL // LANES


def _gelu(x):
    return x * (0.5 * (1.0 + jnp.tanh(SQRT_2_OVER_PI * (x + 0.044715 * (x * x * x)))))


def _sigmoid(x):
    return 1.0 / (1.0 + jnp.exp(-x))


def _rms(x, g):
    return x * lax.rsqrt(jnp.mean(x * x, axis=-1, keepdims=True) + EPS) * g


def _dot(a, b):
    return jnp.dot(a, b, preferred_element_type=F32)


def _top2(logits):
    lane = lax.broadcasted_iota(I32, logits.shape, 1)
    neg = jnp.float32(-jnp.inf)
    logits = jnp.where(lane < N_EXPERTS, logits, neg)
    m1 = jnp.max(logits, axis=-1, keepdims=True)
    i1 = jnp.min(jnp.where(logits == m1, lane, LANES), axis=-1, keepdims=True)
    rest = jnp.where(lane == i1, neg, logits)
    m2 = jnp.max(rest, axis=-1, keepdims=True)
    i2 = jnp.min(jnp.where(rest == m2, lane, LANES), axis=-1, keepdims=True)
    ex = jnp.exp(m2 - m1)
    g1 = 1.0 / (1.0 + ex)
    g2 = ex / (1.0 + ex)
    idx = jnp.where(lane == 0, i1, jnp.where(lane == 1, i2, 0))
    gate = jnp.where(lane == 0, g1, jnp.where(lane == 1, g2, 0.0))
    return idx, gate


def _const_spec(shape, single=False):
    nd = len(shape)
    if single:
        return pl.BlockSpec(shape, lambda *_: (0,) * nd, pipeline_mode=pl.Buffered(1))
    return pl.BlockSpec(shape, lambda *_: (0,) * nd)


def _layer_spec(shape, layer, single=False):
    idx = (layer,) + (0,) * (len(shape) - 1)
    blk = (None,) + tuple(shape[1:])
    if single:
        return pl.BlockSpec(blk, lambda *_: idx, pipeline_mode=pl.Buffered(1))
    return pl.BlockSpec(blk, lambda *_: idx)


def _mixer_body(cfg, *refs):
    B, Tt, start_pos, mm_gmlp, emit_v, route = cfg
    R = Tt * B
    refs = list(refs)
    h_ref = refs.pop(0)
    (conv0_ref, lru0_ref, sre0_ref, sim0_ref, pool0_ref,
     ng_ref, win_ref, bin_ref, lng_ref, lnb_ref, ws_ref, bsm_ref,
     cw_ref, cb_ref, wa_ref, ba_ref, wx_ref, bx_ref, lam_ref,
     lbr_ref, lbi_ref, bblk_ref, cre_ref, cim_ref, d_ref, wglu_ref, bglu_ref,
     pw_ref, ps_ref, wb_ref, wo_ref) = refs[:31]
    n_in = 33 if route else 31
    if route:
        rwh_ref, rwl_ref = refs[31:33]
    n_out = 6 + int(emit_v) + 2 * int(route)
    outs = refs[n_in:n_in + n_out]
    out_ref, convo_ref, lruo_ref, sreo_ref, simo_ref, poolo_ref = outs[:6]
    if route:
        idx_ref, gate_ref = outs[-2:]
    (xn_s, cext_s, pext_s, lruh_s, sre_s, sim_s, u_s, v_s, mix_s, xc_s,
     a_s, b_s, yc_s, yd_s, s5_s) = refs[n_in + n_out:]

    i = pl.program_id(0)
    SB = min(R, 256)
    SB5 = max(B, min(R, 256))
    lane256 = lax.broadcasted_iota(I32, (1, D_BR), 1)
    ydt = yc_s.dtype

    def mm(a, w):
        return _dot(a.astype(BF16), w)

    def init_carries():
        cext_s[0:(CONV_W - 1) * B, :] = conv0_ref[...]
        pext_s[0:POOL_BUF * B, :] = pool0_ref[...]
        lruh_s[...] = lru0_ref[...]
        sre_s[...] = sre0_ref[...]
        sim_s[...] = sim0_ref[...]

    def blocks(n, size, fn):
        for k in range(n):
            fn(k * size)

    def stage_in(r0):
        rows = pl.ds(r0, SB)
        xn = _rms(h_ref[rows, :], ng_ref[...]).astype(BF16)
        xn_s[rows, :] = xn
        pa = _dot(xn, win_ref[:, 0:COLS_A]) + bin_ref[:, 0:COLS_A]
        u_s[rows, :] = _gelu(pa[:, 0:D_BR])
        gv = _gelu(pa[:, D_BR:2 * D_BR])
        mu = jnp.mean(gv, axis=-1, keepdims=True)
        var = jnp.mean(jnp.square(gv - mu), axis=-1, keepdims=True)
        v = (gv - mu) * lax.rsqrt(var + EPS) * lng_ref[...] + lnb_ref[...]
        v_s[0, rows, :] = v[:, 0:LANES]
        v_s[1, rows, :] = v[:, LANES:2 * LANES]
        cext_s[pl.ds(r0 + (CONV_W - 1) * B, SB), :] = pa[:, 2 * D_BR:3 * D_BR]
        xc_s[rows, :] = pa[:, 3 * D_BR:4 * D_BR]
        pext_s[pl.ds(r0 + POOL_BUF * B, SB), :] = pa[:, 4 * D_BR:5 * D_BR]

    def stage_gmlp():
        if mm_gmlp:
            tril = (lax.broadcasted_iota(I32, (CHUNK, CHUNK), 0)
                    >= lax.broadcasted_iota(I32, (CHUNK, CHUNK), 1))
            wm = [jnp.where(tril, ws_ref[hd], 0.0).astype(BF16) for hd in range(N_HEADS)]
            head = lane256 // HEAD_DIM
            for b in range(B):
                vb = jnp.concatenate([v_s[0, pl.ds(b, Tt, stride=B), :],
                                      v_s[1, pl.ds(b, Tt, stride=B), :]], axis=1).astype(BF16)
                mixed = bsm_ref[...]
                for hd in range(N_HEADS):
                    mixed = mixed + jnp.where(head == hd, mm(wm[hd], vb), 0.0)
                mix_s[0, pl.ds(b, Tt, stride=B), :] = mixed[:, 0:LANES]
                mix_s[1, pl.ds(b, Tt, stride=B), :] = mixed[:, LANES:2 * LANES]
        else:
            for t in range(Tt):
                for half in range(2):
                    lo = half * LANES
                    acc = jnp.broadcast_to(bsm_ref[t:t + 1, lo:lo + LANES], (B, LANES))
                    for s in range(t + 1):
                        w = ws_ref[t * Tt + s:t * Tt + s + 1, lo:lo + LANES]
                        acc = acc + w * v_s[half, s * B:(s + 1) * B, :]
                    mix_s[half, t * B:(t + 1) * B, :] = acc
        if emit_v:
            vo_ref = outs[6]
            vo_ref[:, 0:LANES] = v_s[0]
            vo_ref[:, LANES:2 * LANES] = v_s[1]

    log_sig_lam = (jnp.minimum(lam_ref[...], 0.0)
                   - jnp.log1p(jnp.exp(-jnp.abs(lam_ref[...]))))

    def stage_lru(r0):
        rows = pl.ds(r0, SB)
        conv = cb_ref[...] + cext_s[pl.ds(r0, SB), :] * cw_ref[0:1, :]
        for k in range(1, CONV_W):
            conv = conv + cext_s[pl.ds(r0 + k * B, SB), :] * cw_ref[k:k + 1, :]
        cbf = conv.astype(BF16)
        r = _sigmoid(mm(cbf, wa_ref[...]) + ba_ref[...])
        ig = _sigmoid(mm(cbf, wx_ref[...]) + bx_ref[...])
        a = jnp.exp(LRU_C * r * log_sig_lam)
        mult = jnp.sqrt(1.0 - a * a)
        if start_pos == 0:
            row = lax.broadcasted_iota(I32, (SB, D_BR), 0) + r0
            mult = jnp.where(jnp.logical_and(i == 0, row < B), 1.0, mult)
        a_s[rows, :] = a
        b_s[rows, :] = mult * ig * conv

    def lru_step(t, hprev):
        rows = pl.ds(t * B, B) if isinstance(t, int) else pl.ds(pl.multiple_of(t * B, B), B)
        hnew = a_s[rows, :] * hprev + b_s[rows, :]
        b_s[rows, :] = hnew
        return hnew

    def lru_scan():
        hl = lruh_s[...]
        for t in range(Tt):
            hl = lru_step(t, hl)
        lruh_s[...] = hl
        lruo_ref[...] = hl
        conv_tail = cext_s[R:R + (CONV_W - 1) * B, :]
        convo_ref[...] = conv_tail
        cext_s[0:(CONV_W - 1) * B, :] = conv_tail

    lbr = jnp.broadcast_to(lbr_ref[...], (B, S5_W))
    lbi = jnp.broadcast_to(lbi_ref[...], (B, S5_W))
    steps5 = SB5 // B

    def stage_s5(r0):
        rows = pl.ds(r0, SB5)
        xc = xc_s[rows, :]
        s5_s[...] = mm(xc, bblk_ref[...])

        def s5_step(t, carry):
            sre, sim = carry
            rr = pl.ds(t * B, B) if isinstance(t, int) else pl.ds(pl.multiple_of(t * B, B), B)
            nre = lbr * sre - lbi * sim + s5_s[rr, 0:S5_W]
            nim = lbr * sim + lbi * sre + s5_s[rr, S5_W:2 * S5_W]
            s5_s[rr, 0:S5_W] = nre
            s5_s[rr, S5_W:2 * S5_W] = nim
            return nre, nim

        carry = (sre_s[...], sim_s[...])
        for t in range(steps5):
            carry = s5_step(t, carry)
        sre_s[...] = carry[0]
        sim_s[...] = carry[1]
        y = (mm(s5_s[:, 0:S5_W], cre_ref[...])
             - mm(s5_s[:, S5_W:2 * S5_W], cim_ref[...])
             + d_ref[...] * xc)
        y = _gelu(y)
        y = y * _sigmoid(mm(y, wglu_ref[...]) + bglu_ref[...])
        yc_s[rows, :] = y.astype(ydt)

    wlane = jnp.where(lane256 < GROUP_D, POOL_WINDOWS[0],
                      jnp.where(lane256 < 2 * GROUP_D, POOL_WINDOWS[1],
                                jnp.where(lane256 < 3 * GROUP_D, POOL_WINDOWS[2], POOL_WINDOWS[3])))

    def stage_pool(r0):
        base = r0 + POOL_BUF * B
        tok = pext_s[pl.ds(base, SB), :]
        acc = tok
        sums = {}
        for j in range(1, max(POOL_WINDOWS)):
            acc = acc + pext_s[pl.ds(base - j * B, SB), :]
            if j + 1 in POOL_WINDOWS:
                sums[j + 1] = acc
        sel = jnp.where(lane256 < GROUP_D, sums[2],
                        jnp.where(lane256 < 2 * GROUP_D, sums[4],
                                  jnp.where(lane256 < 3 * GROUP_D, sums[8], sums[16])))
        if start_pos >= POOL_BUF:
            cnt = wlane.astype(F32)
        else:
            row = lax.broadcasted_iota(I32, (SB, D_BR), 0) + r0
            tpos = start_pos + i * Tt + lax.shift_right_logical(row, jnp.full_like(row, int(math.log2(B))))
            cnt = jnp.minimum(wlane, tpos + 1).astype(F32)
        diff = sel / cnt - tok
        yd_s[pl.ds(r0, SB), :] = (mm(diff, pw_ref[...]) * ps_ref[...]).astype(ydt)

    def pool_tail():
        tail = pext_s[R:R + POOL_BUF * B, :]
        poolo_ref[...] = tail
        pext_s[0:POOL_BUF * B, :] = tail

    SBM = min(R, 128)

    def stage_merge(r0):
        rows = pl.ds(r0, SBM)
        xn = xn_s[rows, :]
        ya = u_s[rows, :] * jnp.concatenate([mix_s[0, rows, :], mix_s[1, rows, :]], axis=1)
        ys = (ya, b_s[rows, :], yc_s[rows, :], yd_s[rows, :])
        merged = None
        for n in range(N_BRANCH):
            c0 = COLS_A + n * D_MODEL
            logits = _dot(xn, win_ref[:, c0:c0 + D_MODEL]) + bin_ref[:, c0:c0 + D_MODEL]
            term = _sigmoid(logits) * mm(ys[n], wb_ref[n * D_BR:(n + 1) * D_BR, :])
            merged = term if merged is None else merged + term
        out = h_ref[rows, :] + mm(merged, wo_ref[...])
        out_ref[rows, :] = out
        if route:
            ohi = out.astype(BF16)
            olo = (out - ohi.astype(F32)).astype(BF16)
            z = _dot(ohi, rwh_ref[...]) + (_dot(olo, rwh_ref[...]) + _dot(ohi, rwl_ref[...]))
            logits = z * lax.rsqrt(jnp.mean(out * out, axis=-1, keepdims=True) + EPS)
            idx, gate = _top2(logits)
            idx_ref[rows, :] = idx
            gate_ref[rows, :] = gate

    def stage_front(r0):
        stage_in(r0)
        stage_lru(r0)
        stage_pool(r0)

    def sequence_stages():
        blocks(R // SB, SB, stage_front)
        stage_gmlp()
        lru_scan()
        pool_tail()

    def merge_rows(r0):
        for j in range(SB5 // SBM):
            rj = r0 + j * SBM
            stage_merge(rj if isinstance(rj, int) else pl.multiple_of(rj, SBM))

    nb5 = R // SB5
    pl.when(i == 0)(init_carries)
    sequence_stages()
    stage_s5(0)
    for k in range(1, nb5):
        stage_s5(k * SB5)
        merge_rows((k - 1) * SB5)
    merge_rows((nb5 - 1) * SB5)
    sreo_ref[...] = sre_s[...]
    simo_ref[...] = sim_s[...]


def _mixer_call(h_all, blk0, nsteps, B, Tt, start_pos, mm_gmlp, emit_v, states, state_layer, lp, layer,
                router=None):
    R = Tt * B
    route = router is not None
    cfg = (B, Tt, start_pos, mm_gmlp, emit_v, route)
    small = list(states)
    params = list(lp)
    big = {1, 16, 24, 25}
    h_spec = pl.BlockSpec((R, D_MODEL), lambda i: (blk0 + i, 0))
    in_specs = [h_spec]
    if state_layer is None:
        in_specs += [_const_spec(a.shape) for a in small]
        st_shapes = [a.shape for a in small]
    else:
        in_specs += [_layer_spec(a.shape, state_layer) for a in small]
        st_shapes = [a.shape[1:] for a in small]
    in_specs += [_layer_spec(a.shape, layer, single=(k in big)) for k, a in enumerate(params)]
    out_shape = [jax.ShapeDtypeStruct(h_all.shape, F32)] + [jax.ShapeDtypeStruct(s, F32) for s in st_shapes]
    out_specs = [h_spec] + [_const_spec(s.shape) for s in out_shape[1:]]
    if emit_v:
        out_shape.append(jax.ShapeDtypeStruct((R, D_BR), F32))
        out_specs.append(_const_spec((R, D_BR)))
    extra = []
    if route:
        extra = list(router)
        in_specs += [_const_spec(a.shape) for a in extra]
        out_shape += [jax.ShapeDtypeStruct((nsteps * R, LANES), I32),
                      jax.ShapeDtypeStruct((nsteps * R, LANES), F32)]
        out_specs += [pl.BlockSpec((R, LANES), lambda i: (i, 0))] * 2
    SB5 = max(B, min(R, 256))
    scratch = [
        pltpu.VMEM((R, D_MODEL), BF16),
        pltpu.VMEM((R + (CONV_W - 1) * B, D_BR), F32),
        pltpu.VMEM((R + POOL_BUF * B, D_BR), F32),
        pltpu.VMEM((B, D_BR), F32),
        pltpu.VMEM((B, S5_W), F32), pltpu.VMEM((B, S5_W), F32),
        pltpu.VMEM((R, D_BR), F32),
        pltpu.VMEM((2, R, LANES), F32),
        pltpu.VMEM((2, R, LANES), F32),
        pltpu.VMEM((R, D_BR), F32),
        pltpu.VMEM((R, D_BR), F32), pltpu.VMEM((R, D_BR), F32),
        pltpu.VMEM((R, D_BR), BF16), pltpu.VMEM((R, D_BR), BF16),
        pltpu.VMEM((SB5, 2 * S5_W), F32),
    ]
    return pl.pallas_call(
        functools.partial(_mixer_body, cfg),
        grid=(nsteps,),
        in_specs=in_specs,
        out_specs=out_specs,
        out_shape=out_shape,
        scratch_shapes=scratch,
        input_output_aliases={0: 0},
        compiler_params=pltpu.CompilerParams(dimension_semantics=("arbitrary",),
                                             vmem_limit_bytes=60 * 1024 * 1024),
        name="mixer_b%d" % B,
    )(h_all, *small, *params, *extra)


def _s5prep_body(are_ref, aim_ref, ldt_ref, bre_ref, bim_ref, lbr_ref, lbi_ref, bbr_ref, bbi_ref):
    a_re, a_im = are_ref[...], aim_ref[...]
    dt = jnp.exp(ldt_ref[...])
    mag = jnp.exp(a_re * dt)
    lb_re = mag * jnp.cos(a_im * dt)
    lb_im = mag * jnp.sin(a_im * dt)
    den = a_re * a_re + a_im * a_im
    n_re = lb_re - 1.0
    q_re = (n_re * a_re + lb_im * a_im) / den
    q_im = (lb_im * a_re - n_re * a_im) / den
    lbr_ref[...] = lb_re
    lbi_ref[...] = lb_im
    bbr_ref[...] = q_re * bre_ref[...] - q_im * bim_ref[...]
    bbi_ref[...] = q_re * bim_ref[...] + q_im * bre_ref[...]


def _s5prep(a_re, a_im, log_dt, b_re, b_im):
    depth = a_re.shape[0]
    rows = depth * N_GROUPS_C * GROUP_C

    def rep(x):
        return jnp.broadcast_to(x[:, :, None, :], (depth, N_GROUPS_C, GROUP_C, P_STATE)).reshape(rows, P_STATE)

    ldt = jnp.broadcast_to(log_dt[:, :, None, None], (depth, N_GROUPS_C, GROUP_C, P_STATE)).reshape(rows, P_STATE)
    b_re_t = jnp.transpose(b_re, (0, 1, 3, 2)).reshape(rows, P_STATE)
    b_im_t = jnp.transpose(b_im, (0, 1, 3, 2)).reshape(rows, P_STATE)
    shp = jax.ShapeDtypeStruct((rows, P_STATE), F32)
    lbr, lbi, bbr, bbi = pl.pallas_call(_s5prep_body, out_shape=[shp] * 4, name="s5prep")(
        rep(a_re), rep(a_im), ldt, b_re_t, b_im_t)
    r4 = lambda x: x.reshape(depth, N_GROUPS_C, GROUP_C, P_STATE)
    return r4(lbr)[:, :, 0, :], r4(lbi)[:, :, 0, :], r4(bbr), r4(bbi)


def _ffn_body(chunks, x_ref, g_ref, w1_ref, w3_ref, w2_ref, o_ref):
    x = x_ref[...]
    xn = _rms(x, g_ref[...]).astype(BF16)
    acc = None
    for c0, cw in chunks:
        a = _dot(xn, w1_ref[:, c0:c0 + cw])
        b = _dot(xn, w3_ref[:, c0:c0 + cw])
        t = _dot((a * _sigmoid(a) * b).astype(BF16), w2_ref[c0:c0 + cw, :])
        acc = t if acc is None else acc + t
    o_ref[...] = x + acc


def _ffn_call(h, g, w1, w3, w2, tm, tf):
    dff = w1.shape[1]
    chunks = tuple((c0, min(tf, dff - c0)) for c0 in range(0, dff, tf))
    return pl.pallas_call(
        functools.partial(_ffn_body, chunks),
        grid=(h.shape[0] // tm,),
        in_specs=[pl.BlockSpec((tm, D_MODEL), lambda i: (i, 0)),
                  _const_spec((1, D_MODEL)),
                  _const_spec((D_MODEL, dff), single=True),
                  _const_spec((D_MODEL, dff), single=True),
                  _const_spec((dff, D_MODEL), single=True)],
        out_specs=pl.BlockSpec((tm, D_MODEL), lambda i: (i, 0)),
        out_shape=jax.ShapeDtypeStruct(h.shape, F32),
        input_output_aliases={0: 0},
        compiler_params=pltpu.CompilerParams(dimension_semantics=("arbitrary",),
                                             vmem_limit_bytes=56 * 1024 * 1024),
        name="ffn",
    )(h, g, w1, w3, w2)


def _row_copy_wait(hbm_ref, sem, nrows):
    pltpu.make_async_copy(hbm_ref.at[pl.ds(0, nrows)], hbm_ref.at[pl.ds(0, nrows)], sem).wait()


def _dispatch_body(tm, pad_rows, pos_ref, ends_ref, x_ref, g_ref, xs_ref, buf, zbuf, sem):
    @pl.when(pl.program_id(0) == 0)
    def _():
        zbuf[...] = jnp.zeros_like(zbuf)
        n_rows = xs_ref.shape[0]
        for e in range(N_EXPERTS):
            for start in (jnp.maximum(ends_ref[e] - pad_rows, 0), n_rows - (e + 1) * pad_rows):
                cp = pltpu.make_async_copy(zbuf, xs_ref.at[pl.ds(start, pad_rows)], sem.at[1])
                cp.start()
                cp.wait()

    xn = _rms(x_ref[...], g_ref[...])
    for k in range(ROW_TILE):
        buf[pl.ds(k, tm, stride=ROW_TILE), :] = xn[:, k * LANES:(k + 1) * LANES]

    def issue(r, c):
        src = buf.at[pl.ds(pl.multiple_of(r * ROW_TILE, ROW_TILE), ROW_TILE), :]
        pltpu.make_async_copy(src, xs_ref.at[pos_ref[0, 0, 2 * r]], sem.at[0]).start(priority=0)
        pltpu.make_async_copy(src, xs_ref.at[pos_ref[0, 0, 2 * r + 1]], sem.at[0]).start(priority=1)
        return c

    lax.fori_loop(0, tm, issue, 0)
    _row_copy_wait(xs_ref, sem.at[0], 2 * tm)


def _dispatch_call(h, g, pos, ends, n_rows, pad_rows, tm):
    n = h.shape[0]
    nt = n // tm
    return pl.pallas_call(
        functools.partial(_dispatch_body, tm, pad_rows),
        grid=(nt,),
        in_specs=[pl.BlockSpec((1, 1, 2 * tm), lambda i: (i, 0, 0), memory_space=pltpu.SMEM),
                  pl.BlockSpec(memory_space=pltpu.SMEM),
                  pl.BlockSpec((tm, D_MODEL), lambda i: (i, 0)),
                  _const_spec((1, D_MODEL))],
        out_specs=pl.BlockSpec(memory_space=pl.ANY),
        out_shape=jax.ShapeDtypeStruct((n_rows, ROW_TILE, LANES), F32),
        scratch_shapes=[pltpu.VMEM((tm * ROW_TILE, LANES), F32),
                        pltpu.VMEM((pad_rows, ROW_TILE, LANES), F32),
                        pltpu.SemaphoreType.DMA((2,))],
        compiler_params=pltpu.CompilerParams(dimension_semantics=("arbitrary",)),
        name="dispatch",
    )(pos.reshape(nt, 1, 2 * tm), ends, h, g)


def _experts_body(tm, nf, tf, te_ref, tv_ref, tfirst_ref, xs_ref, w1_hbm, w3_hbm, w2_hbm, y_ref,
                  x_s, c1_s, c3_s, c2_s, s1, s3, s2, sem):
    t = pl.program_id(0)
    e = te_ref[t]

    def chunk_copies(f, slot):
        cols = pl.ds(f * tf, tf)
        return (pltpu.make_async_copy(w1_hbm.at[e, :, cols], s1.at[slot], sem.at[slot, 0]),
                pltpu.make_async_copy(w3_hbm.at[e, :, cols], s3.at[slot], sem.at[slot, 1]),
                pltpu.make_async_copy(w2_hbm.at[e, cols, :], s2.at[slot], sem.at[slot, 2]))

    def load_rows():
        for k in range(ROW_TILE):
            x_s[:, k * LANES:(k + 1) * LANES] = xs_ref[pl.ds(k, tm, stride=ROW_TILE), :].astype(BF16)
        return x_s[...]

    def chunk_out(x, f):
        a = _dot(x, c1_s[f])
        b = _dot(x, c3_s[f])
        return _dot((a * _sigmoid(a) * b).astype(BF16), c2_s[f])

    def store_rows(acc):
        for k in range(ROW_TILE):
            y_ref[pl.ds(k, tm, stride=ROW_TILE), :] = acc[:, k * LANES:(k + 1) * LANES]

    @pl.when(tfirst_ref[t] > 0)
    def _():
        for cp in chunk_copies(0, 0):
            cp.start()
        x = load_rows()
        acc = None
        for f in range(nf):
            slot = f % 2
            if f + 1 < nf:
                for cp in chunk_copies(f + 1, 1 - slot):
                    cp.start()
            for cp in chunk_copies(f, slot):
                cp.wait()
            c1_s[f] = s1[slot].astype(BF16)
            c3_s[f] = s3[slot].astype(BF16)
            c2_s[f] = s2[slot].astype(BF16)
            term = chunk_out(x, f)
            acc = term if acc is None else acc + term
        store_rows(acc)

    @pl.when(jnp.logical_and(tv_ref[t] > 0, tfirst_ref[t] == 0))
    def _():
        x = load_rows()
        acc = None
        for f in range(nf):
            term = chunk_out(x, f)
            acc = term if acc is None else acc + term
        store_rows(acc)

    @pl.when(tv_ref[t] == 0)
    def _():
        y_ref[...] = jnp.zeros_like(y_ref)


def _experts_call(xs2d, tile_expert, tile_valid, tile_first, w1, w3, w2, tm, tf):
    rows = xs2d.shape[0] // ROW_TILE
    nt = rows // tm
    dff = w1.shape[2]
    nf = dff // tf
    grid_spec = pltpu.PrefetchScalarGridSpec(
        num_scalar_prefetch=3,
        grid=(nt,),
        in_specs=[pl.BlockSpec((tm * ROW_TILE, LANES), lambda t, te, tv, t1: (t, 0)),
                  pl.BlockSpec(memory_space=pl.ANY), pl.BlockSpec(memory_space=pl.ANY),
                  pl.BlockSpec(memory_space=pl.ANY)],
        out_specs=pl.BlockSpec((tm * ROW_TILE, LANES), lambda t, te, tv, t1: (t, 0)),
        scratch_shapes=[pltpu.VMEM((tm, D_MODEL), BF16),
                        pltpu.VMEM((nf, D_MODEL, tf), BF16), pltpu.VMEM((nf, D_MODEL, tf), BF16),
                        pltpu.VMEM((nf, tf, D_MODEL), BF16),
                        pltpu.VMEM((2, D_MODEL, tf), F32), pltpu.VMEM((2, D_MODEL, tf), F32),
                        pltpu.VMEM((2, tf, D_MODEL), F32),
                        pltpu.SemaphoreType.DMA((2, 3))])
    return pl.pallas_call(
        functools.partial(_experts_body, tm, nf, tf),
        grid_spec=grid_spec,
        out_shape=jax.ShapeDtypeStruct(xs2d.shape, F32),
        compiler_params=pltpu.CompilerParams(dimension_semantics=("arbitrary",),
                                             vmem_limit_bytes=56 * 1024 * 1024),
        name="experts",
    )(tile_expert, tile_valid, tile_first, xs2d, w1, w3, w2)


def _combine_body(tm, final, pos_ref, h_ref, gate_ref, *rest):
    if final:
        ng_ref, y_ref, o_ref, buf, sem = rest
    else:
        y_ref, o_ref, buf, sem = rest

    def issue(r, c):
        for j in range(2):
            dst = buf.at[j, pl.ds(pl.multiple_of(r * ROW_TILE, ROW_TILE), ROW_TILE), :]
            pltpu.make_async_copy(y_ref.at[pos_ref[0, 0, 2 * r + j]], dst, sem.at[0]).start(priority=j)
        return c

    lax.fori_loop(0, tm, issue, 0)
    _row_copy_wait(y_ref, sem.at[0], 2 * tm)
    g0 = gate_ref[:, 0:1]
    g1 = gate_ref[:, 1:2]
    for k in range(ROW_TILE):
        cols = slice(k * LANES, (k + 1) * LANES)
        moe = (g0 * buf[0, pl.ds(k, tm, stride=ROW_TILE), :]
               + g1 * buf[1, pl.ds(k, tm, stride=ROW_TILE), :])
        o_ref[:, cols] = h_ref[:, cols] + moe
    if final:
        o_ref[...] = _rms(o_ref[...], ng_ref[...])


def _combine_call(h, gates, pos, y3, tm, final_g=None):
    n = h.shape[0]
    nt = n // tm
    final = final_g is not None
    in_specs = [pl.BlockSpec((1, 1, 2 * tm), lambda i: (i, 0, 0), memory_space=pltpu.SMEM),
                pl.BlockSpec((tm, D_MODEL), lambda i: (i, 0)),
                pl.BlockSpec((tm, LANES), lambda i: (i, 0))]
    args = [pos.reshape(nt, 1, 2 * tm), h, gates]
    if final:
        in_specs.append(_const_spec((1, D_MODEL)))
        args.append(final_g)
    in_specs.append(pl.BlockSpec(memory_space=pl.ANY))
    args.append(y3)
    return pl.pallas_call(
        functools.partial(_combine_body, tm, final),
        grid=(nt,),
        in_specs=in_specs,
        out_specs=pl.BlockSpec((tm, D_MODEL), lambda i: (i, 0)),
        out_shape=jax.ShapeDtypeStruct((n, D_MODEL), F32),
        scratch_shapes=[pltpu.VMEM((2, tm * ROW_TILE, LANES), F32), pltpu.SemaphoreType.DMA((1,))],
        compiler_params=pltpu.CompilerParams(dimension_semantics=("arbitrary",)),
        name="combine",
    )(*args)


def _to_tm_body(nb, tt, x_ref, tail_ref, o_ref, slab):
    last = pl.program_id(0) == pl.num_programs(0) - 1

    @pl.when(jnp.logical_not(last))
    def _():
        for b in range(nb):
            for k in range(ROW_TILE):
                slab[k, pl.ds(b, tt, stride=nb), :] = x_ref[b, :, k * LANES:(k + 1) * LANES]
        for k in range(ROW_TILE):
            o_ref[:, k * LANES:(k + 1) * LANES] = slab[k]

    @pl.when(last)
    def _():
        o_ref[0:tail_ref.shape[0], :] = tail_ref[...]


def _from_tm_body(nb, tt, x_ref, o_ref, slab):
    for k in range(ROW_TILE):
        slab[k] = x_ref[:, k * LANES:(k + 1) * LANES]
    for b in range(nb):
        for k in range(ROW_TILE):
            o_ref[b, :, k * LANES:(k + 1) * LANES] = slab[k, pl.ds(b, tt, stride=nb), :]


def _to_tm_call(x, tail, tt):
    nb, t, d = x.shape
    ns = tail.shape[0]
    assert ns <= tt * nb
    nsteps = t // tt
    return pl.pallas_call(
        functools.partial(_to_tm_body, nb, tt),
        grid=(nsteps + 1,),
        in_specs=[pl.BlockSpec((nb, tt, d), lambda i: (0, jnp.minimum(i, nsteps - 1), 0)),
                  _const_spec((ns, d))],
        out_specs=pl.BlockSpec((tt * nb, d), lambda i: (i, 0)),
        out_shape=jax.ShapeDtypeStruct((nb * t + ns, d), F32),
        scratch_shapes=[pltpu.VMEM((ROW_TILE, tt * nb, LANES), F32)],
        compiler_params=pltpu.CompilerParams(dimension_semantics=("arbitrary",)),
        name="to_tm",
    )(x, tail)


def _from_tm_call(h, nb, t, tt):
    d = h.shape[1]
    return pl.pallas_call(
        functools.partial(_from_tm_body, nb, tt),
        grid=(t // tt,),
        in_specs=[pl.BlockSpec((tt * nb, d), lambda i: (i, 0))],
        out_specs=pl.BlockSpec((nb, tt, d), lambda i: (0, i, 0)),
        out_shape=jax.ShapeDtypeStruct((nb, t, d), F32),
        scratch_shapes=[pltpu.VMEM((ROW_TILE, tt * nb, LANES), F32)],
        compiler_params=pltpu.CompilerParams(dimension_semantics=("arbitrary",)),
        name="from_tm",
    )(h)


def _rms_body(x_ref, g_ref, o_ref):
    o_ref[...] = _rms(x_ref[...], g_ref[...])


def _rms_call(h, g, tm):
    n = h.shape[0]
    return pl.pallas_call(
        _rms_body, grid=(n // tm,),
        in_specs=[pl.BlockSpec((tm, D_MODEL), lambda i: (i, 0)), _const_spec((1, D_MODEL))],
        out_specs=pl.BlockSpec((tm, D_MODEL), lambda i: (i, 0)),
        out_shape=jax.ShapeDtypeStruct((n, D_MODEL), F32),
        name="rms",
    )(h, g)


MOE_TM = 512
MOE_TF = 512
TOK_TM = 768
FFN_TF = 512


def _moe_layer(h, norm_g, idx, gates, w1, w3, w2, e0, final_g):
    n = h.shape[0]
    e_flat = idx[:, 0:2].reshape(2 * n)
    onehot = (e_flat[:, None] == jnp.arange(N_EXPERTS, dtype=I32)[None, :]).astype(I32)
    csum = jnp.cumsum(onehot, axis=0)
    rank = jnp.sum((csum - onehot) * onehot, axis=1)
    counts = csum[-1]
    padded = ((counts + MOE_TM - 1) // MOE_TM) * MOE_TM
    ends = jnp.cumsum(padded)
    starts = ends - padded
    pos = (jnp.sum(onehot * starts[None, :], axis=1) + rank).astype(I32)
    n_tiles = (2 * n) // MOE_TM + N_EXPERTS
    tile_start = jnp.arange(n_tiles, dtype=I32) * MOE_TM
    tile_valid = (tile_start < ends[-1]).astype(I32)
    tile_expert = jnp.minimum(jnp.sum((tile_start[:, None] >= ends[None, :]).astype(I32), axis=1),
                              N_EXPERTS - 1).astype(I32)
    last_valid = jnp.max(jnp.where(tile_valid > 0, tile_expert, 0))
    tile_expert = jnp.where(tile_valid > 0, tile_expert, last_valid)
    xs = _dispatch_call(h, norm_g, pos, ends.astype(I32), n_tiles * MOE_TM, MOE_TM, TOK_TM)
    prev_expert = jnp.concatenate([jnp.full((1,), -1, I32), tile_expert[:-1]])
    tile_first = jnp.logical_and(tile_expert != prev_expert, tile_valid > 0).astype(I32)
    y2d = _experts_call(xs.reshape(n_tiles * MOE_TM * ROW_TILE, LANES), tile_expert + e0, tile_valid, tile_first,
                        w1, w3, w2, MOE_TM, MOE_TF)
    return _combine_call(h, gates, pos, y2d.reshape(n_tiles * MOE_TM, ROW_TILE, LANES), TOK_TM, final_g)


def _block_diag(w):
    n, k, a, b = w.shape
    eye = jnp.eye(k, dtype=w.dtype)
    return jnp.einsum('lkab,kj->lkajb', w, eye).reshape(n, k * a, k * b)


def kernel(x_prompt, x_sample, state_conv, state_lru, state_s5_re, state_s5_im, state_pool, norm_mix_g, w_in, b_in, gmlp_ln_g, gmlp_ln_b, gmlp_w_s, gmlp_b_s, conv_w, conv_b, lru_w_a, lru_b_a, lru_w_x, lru_b_x, lru_lam, s5_a_re, s5_a_im, s5_log_dt, s5_b_re, s5_b_im, s5_c_re, s5_c_im, s5_d, s5_w_glu, s5_b_glu, pool_w, pool_scale, w_branch, w_out, norm_ffn_g, ffn_w1, ffn_w3, ffn_w2, router_w, moe_w1, moe_w3, moe_w2, norm_final_g):
    depth = w_in.shape[0]
    bp, tp, _ = x_prompt.shape
    bs, ts, _ = x_sample.shape
    n_p, n_s = bp * tp, bs * ts
    assert n_p % (CHUNK * bp) == 0 and n_p % n_s == 0

    h = _to_tm_call(x_prompt, jnp.transpose(x_sample, (1, 0, 2)).reshape(n_s, D_MODEL), CHUNK)

    lbr, lbi, bbr, bbi = _s5prep(s5_a_re, s5_a_im, s5_log_dt, s5_b_re, s5_b_im)
    lane_head = jnp.arange(D_BR) // HEAD_DIM

    rows = lambda x: x.reshape(depth, 1, -1)
    bf = lambda x: x.astype(BF16)
    common_tail = (
        conv_w, rows(conv_b),
        bf(_block_diag(lru_w_a)), rows(lru_b_a), bf(_block_diag(lru_w_x)), rows(lru_b_x), rows(lru_lam),
        rows(lbr), rows(lbi),
        bf(jnp.concatenate([_block_diag(bbr), _block_diag(bbi)], axis=2)),
        bf(_block_diag(jnp.transpose(s5_c_re, (0, 1, 3, 2)))),
        bf(_block_diag(jnp.transpose(s5_c_im, (0, 1, 3, 2)))), rows(s5_d),
        bf(s5_w_glu), rows(s5_b_glu),
        bf(_block_diag(pool_w)), rows(pool_scale),
        bf(w_branch.reshape(depth, N_BRANCH * D_BR, D_MODEL)), bf(w_out))
    common_head = (rows(norm_mix_g), bf(w_in), rows(b_in), rows(gmlp_ln_g), rows(gmlp_ln_b))
    bsm_p = jnp.transpose(gmlp_b_s[:, lane_head, :], (0, 2, 1))
    lp_p = common_head + (gmlp_w_s, bsm_p) + common_tail
    ws_small = jnp.transpose(gmlp_w_s[:, :, :ts, :ts], (0, 2, 3, 1))[..., lane_head].reshape(depth, ts * ts, D_BR)
    bsm_s = jnp.transpose(gmlp_b_s[:, lane_head, :ts], (0, 2, 1))
    lp_s = common_head + (ws_small, bsm_s) + common_tail
    st_s = (jnp.transpose(state_conv, (0, 2, 1, 3)).reshape(depth, (CONV_W - 1) * bs, D_BR),
            state_lru, state_s5_re.reshape(depth, bs, S5_W), state_s5_im.reshape(depth, bs, S5_W),
            jnp.transpose(state_pool, (0, 2, 1, 3)).reshape(depth, POOL_BUF * bs, D_BR))
    row = lambda x: x.reshape(1, -1)

    conv_p, lru_p, sre_p, sim_p, pool_p = [], [], [], [], []
    conv_s, lru_s, sre_s, sim_s, pool_s, v_s = [], [], [], [], [], []
    zeros_p = (jnp.zeros(((CONV_W - 1) * bp, D_BR), F32), jnp.zeros((bp, D_BR), F32),
               jnp.zeros((bp, S5_W), F32), jnp.zeros((bp, S5_W), F32),
               jnp.zeros((POOL_BUF * bp, D_BR), F32))
    for l in range(depth):
        router = None
        if l % 2 == 1:
            rw = jnp.pad(norm_ffn_g[l][:, None] * router_w[l // 2], ((0, 0), (0, LANES - N_EXPERTS)))
            rw_hi = rw.astype(BF16)
            router = (rw_hi, (rw - rw_hi.astype(F32)).astype(BF16))
        outs_p = _mixer_call(h, 0, tp // CHUNK, bp, CHUNK, 0, True, False, zeros_p, None, lp_p, l, router)
        outs_s = _mixer_call(outs_p[0], n_p // n_s, 1, bs, ts, PAST_LEN, False, True, st_s, l, lp_s, l, router)
        h = outs_s[0]
        for lst, o in zip((conv_p, lru_p, sre_p, sim_p, pool_p), outs_p[1:6]):
            lst.append(o)
        for lst, o in zip((conv_s, lru_s, sre_s, sim_s, pool_s, v_s), outs_s[1:7]):
            lst.append(o)

        if l % 2 == 0:
            k = l // 2
            h = _ffn_call(h, row(norm_ffn_g[l]), ffn_w1[k].astype(BF16), ffn_w3[k].astype(BF16),
                          ffn_w2[k].astype(BF16), TOK_TM, FFN_TF)
            if l == depth - 1:
                h = _rms_call(h, row(norm_final_g), TOK_TM)
        else:
            k = l // 2
            idx = jnp.concatenate([outs_p[-2], outs_s[-2]], axis=0)
            gates = jnp.concatenate([outs_p[-1], outs_s[-1]], axis=0)
            stk = lambda w: w.reshape((-1,) + w.shape[2:])
            h = _moe_layer(h, row(norm_ffn_g[l]), idx, gates, stk(moe_w1), stk(moe_w3), stk(moe_w2),
                           k * N_EXPERTS, row(norm_final_g) if l == depth - 1 else None)

    def tm_to_bm(x, t, b):
        return jnp.transpose(x.reshape(t, b, x.shape[-1]), (1, 0, 2))

    y_prompt = _from_tm_call(h, bp, tp, CHUNK)
    y_sample = tm_to_bm(h[n_p:], ts, bs)
    stack = lambda lst, f: jnp.stack([f(o) for o in lst])
    return (
        y_prompt, y_sample,
        stack(conv_p, lambda o: tm_to_bm(o, CONV_W - 1, bp)),
        stack(lru_p, lambda o: o),
        stack(sre_p, lambda o: o.reshape(bp, N_GROUPS_C, P_STATE)),
        stack(sim_p, lambda o: o.reshape(bp, N_GROUPS_C, P_STATE)),
        stack(pool_p, lambda o: tm_to_bm(o, POOL_BUF, bp)),
        stack(conv_s, lambda o: tm_to_bm(o, CONV_W - 1, bs)),
        stack(lru_s, lambda o: o),
        stack(sre_s, lambda o: o.reshape(bs, N_GROUPS_C, P_STATE)),
        stack(sim_s, lambda o: o.reshape(bs, N_GROUPS_C, P_STATE)),
        stack(pool_s, lambda o: tm_to_bm(o, POOL_BUF, bs)),
        stack(v_s, lambda o: tm_to_bm(o, ts, bs)),
    )
```

```python
import functools
import math

import jax
import jax.numpy as jnp
from jax import lax
from jax.experimental import pallas as pl
from jax.experimental.pallas import tpu as pltpu

F32 = jnp.float32
BF16 = jnp.bfloat16
I32 = jnp.int32

SUBLANES = 8
LANES = 128

D_MODEL = 1024
D_BR = 256
N_BRANCH = 4
N_HEADS = 4
HEAD_DIM = D_BR // N_HEADS
CHUNK = 128
CONV_W = 4
LRU_C = 8.0
N_GROUPS_C = 16
GROUP_C = 16
P_STATE = 64
S5_W = N_GROUPS_C * P_STATE
POOL_WINDOWS = (2, 4, 8, 16)
POOL_BUF = max(POOL_WINDOWS) - 1
GROUP_D = D_BR // len(POOL_WINDOWS)
COLS_A = 5 * D_BR
IN_COLS = COLS_A + N_BRANCH * D_MODEL
N_EXPERTS = 8
EPS = 1e-6
PAST_LEN = 16384
SQRT_2_OVER_PI = math.sqrt(2.0 / math.pi)
ROW_TILE = D_MODEL // LANES


def _gelu(x):
    return x * (0.5 * (1.0 + jnp.tanh(SQRT_2_OVER_PI * (x + 0.044715 * (x * x * x)))))


def _sigmoid(x):
    return 1.0 / (1.0 + jnp.exp(-x))


def _rms(x, g):
    return x * lax.rsqrt(jnp.mean(x * x, axis=-1, keepdims=True) + EPS) * g


def _dot(a, b):
    return jnp.dot(a, b, preferred_element_type=F32)


def _top2(logits):
    lane = lax.broadcasted_iota(I32, logits.shape, 1)
    neg = jnp.float32(-jnp.inf)
    logits = jnp.where(lane < N_EXPERTS, logits, neg)
    m1 = jnp.max(logits, axis=-1, keepdims=True)
    i1 = jnp.min(jnp.where(logits == m1, lane, LANES), axis=-1, keepdims=True)
    rest = jnp.where(lane == i1, neg, logits)
    m2 = jnp.max(rest, axis=-1, keepdims=True)
    i2 = jnp.min(jnp.where(rest == m2, lane, LANES), axis=-1, keepdims=True)
    ex = jnp.exp(m2 - m1)
    g1 = 1.0 / (1.0 + ex)
    g2 = ex / (1.0 + ex)
    idx = jnp.where(lane == 0, i1, jnp.where(lane == 1, i2, 0))
    gate = jnp.where(lane == 0, g1, jnp.where(lane == 1, g2, 0.0))
    return idx, gate


def _const_spec(shape, single=False):
    nd = len(shape)
    if single:
        return pl.BlockSpec(shape, lambda *_: (0,) * nd, pipeline_mode=pl.Buffered(1))
    return pl.BlockSpec(shape, lambda *_: (0,) * nd)


def _layer_spec(shape, layer, single=False):
    idx = (layer,) + (0,) * (len(shape) - 1)
    blk = (None,) + tuple(shape[1:])
    if single:
        return pl.BlockSpec(blk, lambda *_: idx, pipeline_mode=pl.Buffered(1))
    return pl.BlockSpec(blk, lambda *_: idx)


def _mixer_body(cfg, *refs):
    B, Tt, start_pos, mm_gmlp, emit_v, route = cfg
    R = Tt * B
    refs = list(refs)
    h_ref = refs.pop(0)
    (conv0_ref, lru0_ref, sre0_ref, sim0_ref, pool0_ref,
     ng_ref, win_ref, bin_ref, lng_ref, lnb_ref, ws_ref, bsm_ref,
     cw_ref, cb_ref, wa_ref, ba_ref, wx_ref, bx_ref, lam_ref,
     lbr_ref, lbi_ref, bblk_ref, cre_ref, cim_ref, d_ref, wglu_ref, bglu_ref,
     pw_ref, ps_ref, wb_ref, wo_ref) = refs[:31]
    n_in = 33 if route else 31
    if route:
        rwh_ref, rwl_ref = refs[31:33]
    n_out = 6 + int(emit_v) + 2 * int(route)
    outs = refs[n_in:n_in + n_out]
    out_ref, convo_ref, lruo_ref, sreo_ref, simo_ref, poolo_ref = outs[:6]
    if route:
        idx_ref, gate_ref = outs[-2:]
    (xn_s, cext_s, pext_s, lruh_s, sre_s, sim_s, u_s, v_s, mix_s, xc_s,
     a_s, b_s, yc_s, yd_s, s5_s) = refs[n_in + n_out:]

    i = pl.program_id(0)
    SB = min(R, 256)
    SB5 = max(B, min(R, 256))
    lane256 = lax.broadcasted_iota(I32, (1, D_BR), 1)
    ydt = yc_s.dtype

    def mm(a, w):
        return _dot(a.astype(BF16), w)

    def init_carries():
        cext_s[0:(CONV_W - 1) * B, :] = conv0_ref[...]
        pext_s[0:POOL_BUF * B, :] = pool0_ref[...]
        lruh_s[...] = lru0_ref[...]
        sre_s[...] = sre0_ref[...]
        sim_s[...] = sim0_ref[...]

    def blocks(n, size, fn):
        for k in range(n):
            fn(k * size)

    def stage_in(r0):
        rows = pl.ds(r0, SB)
        xn = _rms(h_ref[rows, :], ng_ref[...]).astype(BF16)
        xn_s[rows, :] = xn
        pa = _dot(xn, win_ref[:, 0:COLS_A]) + bin_ref[:, 0:COLS_A]
        u_s[rows, :] = _gelu(pa[:, 0:D_BR])
        gv = _gelu(pa[:, D_BR:2 * D_BR])
        mu = jnp.mean(gv, axis=-1, keepdims=True)
        var = jnp.mean(jnp.square(gv - mu), axis=-1, keepdims=True)
        v = (gv - mu) * lax.rsqrt(var + EPS) * lng_ref[...] + lnb_ref[...]
        v_s[0, rows, :] = v[:, 0:LANES]
        v_s[1, rows, :] = v[:, LANES:2 * LANES]
        cext_s[pl.ds(r0 + (CONV_W - 1) * B, SB), :] = pa[:, 2 * D_BR:3 * D_BR]
        xc_s[rows, :] = pa[:, 3 * D_BR:4 * D_BR]
        pext_s[pl.ds(r0 + POOL_BUF * B, SB), :] = pa[:, 4 * D_BR:5 * D_BR]

    def stage_gmlp():
        if mm_gmlp:
            tril = (lax.broadcasted_iota(I32, (CHUNK, CHUNK), 0)
                    >= lax.broadcasted_iota(I32, (CHUNK, CHUNK), 1))
            wm = [jnp.where(tril, ws_ref[hd], 0.0).astype(BF16) for hd in range(N_HEADS)]
            head = lane256 // HEAD_DIM
            for b in range(B):
                vb = jnp.concatenate([v_s[0, pl.ds(b, Tt, stride=B), :],
                                      v_s[1, pl.ds(b, Tt, stride=B), :]], axis=1).astype(BF16)
                mixed = bsm_ref[...]
                for hd in range(N_HEADS):
                    mixed = mixed + jnp.where(head == hd, mm(wm[hd], vb), 0.0)
                mix_s[0, pl.ds(b, Tt, stride=B), :] = mixed[:, 0:LANES]
                mix_s[1, pl.ds(b, Tt, stride=B), :] = mixed[:, LANES:2 * LANES]
        else:
            for t in range(Tt):
                for half in range(2):
                    lo = half * LANES
                    acc = jnp.broadcast_to(bsm_ref[t:t + 1, lo:lo + LANES], (B, LANES))
                    for s in range(t + 1):
                        w = ws_ref[t * Tt + s:t * Tt + s + 1, lo:lo + LANES]
                        acc = acc + w * v_s[half, s * B:(s + 1) * B, :]
                    mix_s[half, t * B:(t + 1) * B, :] = acc
        if emit_v:
            vo_ref = outs[6]
            vo_ref[:, 0:LANES] = v_s[0]
            vo_ref[:, LANES:2 * LANES] = v_s[1]

    log_sig_lam = (jnp.minimum(lam_ref[...], 0.0)
                   - jnp.log1p(jnp.exp(-jnp.abs(lam_ref[...]))))

    def stage_lru(r0):
        rows = pl.ds(r0, SB)
        conv = cb_ref[...] + cext_s[pl.ds(r0, SB), :] * cw_ref[0:1, :]
        for k in range(1, CONV_W):
            conv = conv + cext_s[pl.ds(r0 + k * B, SB), :] * cw_ref[k:k + 1, :]
        cbf = conv.astype(BF16)
        r = _sigmoid(mm(cbf, wa_ref[...]) + ba_ref[...])
        ig = _sigmoid(mm(cbf, wx_ref[...]) + bx_ref[...])
        a = jnp.exp(LRU_C * r * log_sig_lam)
        mult = jnp.sqrt(1.0 - a * a)
        if start_pos == 0:
            row = lax.broadcasted_iota(I32, (SB, D_BR), 0) + r0
            mult = jnp.where(jnp.logical_and(i == 0, row < B), 1.0, mult)
        a_s[rows, :] = a
        b_s[rows, :] = mult * ig * conv

    def lru_step(t, hprev):
        rows = pl.ds(t * B, B) if isinstance(t, int) else pl.ds(pl.multiple_of(t * B, B), B)
        hnew = a_s[rows, :] * hprev + b_s[rows, :]
        b_s[rows, :] = hnew
        return hnew

    def lru_scan():
        hl = lruh_s[...]
        for t in range(Tt):
            hl = lru_step(t, hl)
        lruh_s[...] = hl
        lruo_ref[...] = hl
        conv_tail = cext_s[R:R + (CONV_W - 1) * B, :]
        convo_ref[...] = conv_tail
        cext_s[0:(CONV_W - 1) * B, :] = conv_tail

    lbr = jnp.broadcast_to(lbr_ref[...], (B, S5_W))
    lbi = jnp.broadcast_to(lbi_ref[...], (B, S5_W))
    steps5 = SB5 // B

    def stage_s5(r0):
        rows = pl.ds(r0, SB5)
        xc = xc_s[rows, :]
        s5_s[...] = mm(xc, bblk_ref[...])

        def s5_step(t, carry):
            sre, sim = carry
            rr = pl.ds(t * B, B) if isinstance(t, int) else pl.ds(pl.multiple_of(t * B, B), B)
            nre = lbr * sre - lbi * sim + s5_s[rr, 0:S5_W]
            nim = lbr * sim + lbi * sre + s5_s[rr, S5_W:2 * S5_W]
            s5_s[rr, 0:S5_W] = nre
            s5_s[rr, S5_W:2 * S5_W] = nim
            return nre, nim

        carry = (sre_s[...], sim_s[...])
        for t in range(steps5):
            carry = s5_step(t, carry)
        sre_s[...] = carry[0]
        sim_s[...] = carry[1]
        y = (mm(s5_s[:, 0:S5_W], cre_ref[...])
             - mm(s5_s[:, S5_W:2 * S5_W], cim_ref[...])
             + d_ref[...] * xc)
        y = _gelu(y)
        y = y * _sigmoid(mm(y, wglu_ref[...]) + bglu_ref[...])
        yc_s[rows, :] = y.astype(ydt)

    wlane = jnp.where(lane256 < GROUP_D, POOL_WINDOWS[0],
                      jnp.where(lane256 < 2 * GROUP_D, POOL_WINDOWS[1],
                                jnp.where(lane256 < 3 * GROUP_D, POOL_WINDOWS[2], POOL_WINDOWS[3])))

    def stage_pool(r0):
        base = r0 + POOL_BUF * B
        tok = pext_s[pl.ds(base, SB), :]
        acc = tok
        sums = {}
        for j in range(1, max(POOL_WINDOWS)):
            acc = acc + pext_s[pl.ds(base - j * B, SB), :]
            if j + 1 in POOL_WINDOWS:
                sums[j + 1] = acc
        sel = jnp.where(lane256 < GROUP_D, sums[2],
                        jnp.where(lane256 < 2 * GROUP_D, sums[4],
                                  jnp.where(lane256 < 3 * GROUP_D, sums[8], sums[16])))
        if start_pos >= POOL_BUF:
            cnt = wlane.astype(F32)
        else:
            row = lax.broadcasted_iota(I32, (SB, D_BR), 0) + r0
            tpos = start_pos + i * Tt + lax.shift_right_logical(row, jnp.full_like(row, int(math.log2(B))))
            cnt = jnp.minimum(wlane, tpos + 1).astype(F32)
        diff = sel / cnt - tok
        yd_s[pl.ds(r0, SB), :] = (mm(diff, pw_ref[...]) * ps_ref[...]).astype(ydt)

    def pool_tail():
        tail = pext_s[R:R + POOL_BUF * B, :]
        poolo_ref[...] = tail
        pext_s[0:POOL_BUF * B, :] = tail

    SBM = min(R, 128)

    def stage_merge(r0):
        rows = pl.ds(r0, SBM)
        xn = xn_s[rows, :]
        ya = u_s[rows, :] * jnp.concatenate([mix_s[0, rows, :], mix_s[1, rows, :]], axis=1)
        ys = (ya, b_s[rows, :], yc_s[rows, :], yd_s[rows, :])
        merged = None
        for n in range(N_BRANCH):
            c0 = COLS_A + n * D_MODEL
            logits = _dot(xn, win_ref[:, c0:c0 + D_MODEL]) + bin_ref[:, c0:c0 + D_MODEL]
            term = _sigmoid(logits) * mm(ys[n], wb_ref[n * D_BR:(n + 1) * D_BR, :])
            merged = term if merged is None else merged + term
        out = h_ref[rows, :] + mm(merged, wo_ref[...])
        out_ref[rows, :] = out
        if route:
            ohi = out.astype(BF16)
            olo = (out - ohi.astype(F32)).astype(BF16)
            z = _dot(ohi, rwh_ref[...]) + (_dot(olo, rwh_ref[...]) + _dot(ohi, rwl_ref[...]))
            logits = z * lax.rsqrt(jnp.mean(out * out, axis=-1, keepdims=True) + EPS)
            idx, gate = _top2(logits)
            idx_ref[rows, :] = idx
            gate_ref[rows, :] = gate

    def stage_front(r0):
        stage_in(r0)
        stage_lru(r0)
        stage_pool(r0)

    def sequence_stages():
        blocks(R // SB, SB, stage_front)
        stage_gmlp()
        lru_scan()
        pool_tail()

    def merge_rows(r0):
        for j in range(SB5 // SBM):
            rj = r0 + j * SBM
            stage_merge(rj if isinstance(rj, int) else pl.multiple_of(rj, SBM))

    nb5 = R // SB5
    pl.when(i == 0)(init_carries)
    sequence_stages()
    stage_s5(0)
    for k in range(1, nb5):
        stage_s5(k * SB5)
        merge_rows((k - 1) * SB5)
    merge_rows((nb5 - 1) * SB5)
    sreo_ref[...] = sre_s[...]
    simo_ref[...] = sim_s[...]


def _mixer_call(h_all, blk0, nsteps, B, Tt, start_pos, mm_gmlp, emit_v, states, state_layer, lp, layer,
                router=None):
    R = Tt * B
    route = router is not None
    cfg = (B, Tt, start_pos, mm_gmlp, emit_v, route)
    small = list(states)
    params = list(lp)
    big = {1, 16, 24, 25}
    h_spec = pl.BlockSpec((R, D_MODEL), lambda i: (blk0 + i, 0))
    in_specs = [h_spec]
    if state_layer is None:
        in_specs += [_const_spec(a.shape) for a in small]
        st_shapes = [a.shape for a in small]
    else:
        in_specs += [_layer_spec(a.shape, state_layer) for a in small]
        st_shapes = [a.shape[1:] for a in small]
    in_specs += [_layer_spec(a.shape, layer, single=(k in big)) for k, a in enumerate(params)]
    out_shape = [jax.ShapeDtypeStruct(h_all.shape, F32)] + [jax.ShapeDtypeStruct(s, F32) for s in st_shapes]
    out_specs = [h_spec] + [_const_spec(s.shape) for s in out_shape[1:]]
    if emit_v:
        out_shape.append(jax.ShapeDtypeStruct((R, D_BR), F32))
        out_specs.append(_const_spec((R, D_BR)))
    extra = []
    if route:
        extra = list(router)
        in_specs += [_const_spec(a.shape) for a in extra]
        out_shape += [jax.ShapeDtypeStruct((nsteps * R, LANES), I32),
                      jax.ShapeDtypeStruct((nsteps * R, LANES), F32)]
        out_specs += [pl.BlockSpec((R, LANES), lambda i: (i, 0))] * 2
    SB5 = max(B, min(R, 256))
    scratch = [
        pltpu.VMEM((R, D_MODEL), BF16),
        pltpu.VMEM((R + (CONV_W - 1) * B, D_BR), F32),
        pltpu.VMEM((R + POOL_BUF * B, D_BR), F32),
        pltpu.VMEM((B, D_BR), F32),
        pltpu.VMEM((B, S5_W), F32), pltpu.VMEM((B, S5_W), F32),
        pltpu.VMEM((R, D_BR), F32),
        pltpu.VMEM((2, R, LANES), F32),
        pltpu.VMEM((2, R, LANES), F32),
        pltpu.VMEM((R, D_BR), F32),
        pltpu.VMEM((R, D_BR), F32), pltpu.VMEM((R, D_BR), F32),
        pltpu.VMEM((R, D_BR), BF16), pltpu.VMEM((R, D_BR), BF16),
        pltpu.VMEM((SB5, 2 * S5_W), F32),
    ]
    return pl.pallas_call(
        functools.partial(_mixer_body, cfg),
        grid=(nsteps,),
        in_specs=in_specs,
        out_specs=out_specs,
        out_shape=out_shape,
        scratch_shapes=scratch,
        input_output_aliases={0: 0},
        compiler_params=pltpu.CompilerParams(dimension_semantics=("arbitrary",),
                                             vmem_limit_bytes=60 * 1024 * 1024),
        name="mixer_b%d" % B,
    )(h_all, *small, *params, *extra)


def _s5prep_body(are_ref, aim_ref, ldt_ref, bre_ref, bim_ref, lbr_ref, lbi_ref, bbr_ref, bbi_ref):
    a_re, a_im = are_ref[...], aim_ref[...]
    dt = jnp.exp(ldt_ref[...])
    mag = jnp.exp(a_re * dt)
    lb_re = mag * jnp.cos(a_im * dt)
    lb_im = mag * jnp.sin(a_im * dt)
    den = a_re * a_re + a_im * a_im
    n_re = lb_re - 1.0
    q_re = (n_re * a_re + lb_im * a_im) / den
    q_im = (lb_im * a_re - n_re * a_im) / den
    lbr_ref[...] = lb_re
    lbi_ref[...] = lb_im
    bbr_ref[...] = q_re * bre_ref[...] - q_im * bim_ref[...]
    bbi_ref[...] = q_re * bim_ref[...] + q_im * bre_ref[...]


def _s5prep(a_re, a_im, log_dt, b_re, b_im):
    depth = a_re.shape[0]
    rows = depth * N_GROUPS_C * GROUP_C

    def rep(x):
        return jnp.broadcast_to(x[:, :, None, :], (depth, N_GROUPS_C, GROUP_C, P_STATE)).reshape(rows, P_STATE)

    ldt = jnp.broadcast_to(log_dt[:, :, None, None], (depth, N_GROUPS_C, GROUP_C, P_STATE)).reshape(rows, P_STATE)
    b_re_t = jnp.transpose(b_re, (0, 1, 3, 2)).reshape(rows, P_STATE)
    b_im_t = jnp.transpose(b_im, (0, 1, 3, 2)).reshape(rows, P_STATE)
    shp = jax.ShapeDtypeStruct((rows, P_STATE), F32)
    lbr, lbi, bbr, bbi = pl.pallas_call(_s5prep_body, out_shape=[shp] * 4, name="s5prep")(
        rep(a_re), rep(a_im), ldt, b_re_t, b_im_t)
    r4 = lambda x: x.reshape(depth, N_GROUPS_C, GROUP_C, P_STATE)
    return r4(lbr)[:, :, 0, :], r4(lbi)[:, :, 0, :], r4(bbr), r4(bbi)


def _ffn_body(chunks, x_ref, g_ref, w1_ref, w3_ref, w2_ref, o_ref):
    x = x_ref[...]
    xn = _rms(x, g_ref[...]).astype(BF16)
    acc = None
    for c0, cw in chunks:
        a = _dot(xn, w1_ref[:, c0:c0 + cw])
        b = _dot(xn, w3_ref[:, c0:c0 + cw])
        t = _dot((a * _sigmoid(a) * b).astype(BF16), w2_ref[c0:c0 + cw, :])
        acc = t if acc is None else acc + t
    o_ref[...] = x + acc


def _ffn_call(h, g, w1, w3, w2, tm, tf):
    dff = w1.shape[1]
    chunks = tuple((c0, min(tf, dff - c0)) for c0 in range(0, dff, tf))
    return pl.pallas_call(
        functools.partial(_ffn_body, chunks),
        grid=(h.shape[0] // tm,),
        in_specs=[pl.BlockSpec((tm, D_MODEL), lambda i: (i, 0)),
                  _const_spec((1, D_MODEL)),
                  _const_spec((D_MODEL, dff), single=True),
                  _const_spec((D_MODEL, dff), single=True),
                  _const_spec((dff, D_MODEL), single=True)],
        out_specs=pl.BlockSpec((tm, D_MODEL), lambda i: (i, 0)),
        out_shape=jax.ShapeDtypeStruct(h.shape, F32),
        input_output_aliases={0: 0},
        compiler_params=pltpu.CompilerParams(dimension_semantics=("arbitrary",),
                                             vmem_limit_bytes=56 * 1024 * 1024),
        name="ffn",
    )(h, g, w1, w3, w2)


def _row_copy_wait(hbm_ref, sem, nrows):
    pltpu.make_async_copy(hbm_ref.at[pl.ds(0, nrows)], hbm_ref.at[pl.ds(0, nrows)], sem).wait()


def _dispatch_body(tm, pad_rows, pos_ref, ends_ref, x_ref, g_ref, xs_ref, buf, zbuf, sem):
    @pl.when(pl.program_id(0) == 0)
    def _():
        zbuf[...] = jnp.zeros_like(zbuf)
        n_rows = xs_ref.shape[0]
        for e in range(N_EXPERTS):
            for start in (jnp.maximum(ends_ref[e] - pad_rows, 0), n_rows - (e + 1) * pad_rows):
                cp = pltpu.make_async_copy(zbuf, xs_ref.at[pl.ds(start, pad_rows)], sem.at[1])
                cp.start()
                cp.wait()

    xn = _rms(x_ref[...], g_ref[...])
    for k in range(ROW_TILE):
        buf[pl.ds(k, tm, stride=ROW_TILE), :] = xn[:, k * LANES:(k + 1) * LANES]

    def issue(r, c):
        src = buf.at[pl.ds(pl.multiple_of(r * ROW_TILE, ROW_TILE), ROW_TILE), :]
        pltpu.make_async_copy(src, xs_ref.at[pos_ref[0, 0, 2 * r]], sem.at[0]).start(priority=0)
        pltpu.make_async_copy(src, xs_ref.at[pos_ref[0, 0, 2 * r + 1]], sem.at[0]).start(priority=1)
        return c

    lax.fori_loop(0, tm, issue, 0)
    _row_copy_wait(xs_ref, sem.at[0], 2 * tm)


def _dispatch_call(h, g, pos, ends, n_rows, pad_rows, tm):
    n = h.shape[0]
    nt = n // tm
    return pl.pallas_call(
        functools.partial(_dispatch_body, tm, pad_rows),
        grid=(nt,),
        in_specs=[pl.BlockSpec((1, 1, 2 * tm), lambda i: (i, 0, 0), memory_space=pltpu.SMEM),
                  pl.BlockSpec(memory_space=pltpu.SMEM),
                  pl.BlockSpec((tm, D_MODEL), lambda i: (i, 0)),
                  _const_spec((1, D_MODEL))],
        out_specs=pl.BlockSpec(memory_space=pl.ANY),
        out_shape=jax.ShapeDtypeStruct((n_rows, ROW_TILE, LANES), F32),
        scratch_shapes=[pltpu.VMEM((tm * ROW_TILE, LANES), F32),
                        pltpu.VMEM((pad_rows, ROW_TILE, LANES), F32),
                        pltpu.SemaphoreType.DMA((2,))],
        compiler_params=pltpu.CompilerParams(dimension_semantics=("arbitrary",)),
        name="dispatch",
    )(pos.reshape(nt, 1, 2 * tm), ends, h, g)


def _experts_body(tm, nf, tf, te_ref, tv_ref, tfirst_ref, xs_ref, w1_hbm, w3_hbm, w2_hbm, y_ref,
                  x_s, c1_s, c3_s, c2_s, s1, s3, s2, sem):
    t = pl.program_id(0)
    e = te_ref[t]

    def chunk_copies(f, slot):
        cols = pl.ds(f * tf, tf)
        return (pltpu.make_async_copy(w1_hbm.at[e, :, cols], s1.at[slot], sem.at[slot, 0]),
                pltpu.make_async_copy(w3_hbm.at[e, :, cols], s3.at[slot], sem.at[slot, 1]),
                pltpu.make_async_copy(w2_hbm.at[e, cols, :], s2.at[slot], sem.at[slot, 2]))

    def load_rows():
        for k in range(ROW_TILE):
            x_s[:, k * LANES:(k + 1) * LANES] = xs_ref[pl.ds(k, tm, stride=ROW_TILE), :].astype(BF16)
        return x_s[...]

    def chunk_out(x, f):
        a = _dot(x, c1_s[f])
        b = _dot(x, c3_s[f])
        return _dot((a * _sigmoid(a) * b).astype(BF16), c2_s[f])

    def store_rows(acc):
        for k in range(ROW_TILE):
            y_ref[pl.ds(k, tm, stride=ROW_TILE), :] = acc[:, k * LANES:(k + 1) * LANES]

    @pl.when(tfirst_ref[t] > 0)
    def _():
        for cp in chunk_copies(0, 0):
            cp.start()
        x = load_rows()
        acc = None
        for f in range(nf):
            slot = f % 2
            if f + 1 < nf:
                for cp in chunk_copies(f + 1, 1 - slot):
                    cp.start()
            for cp in chunk_copies(f, slot):
                cp.wait()
            c1_s[f] = s1[slot].astype(BF16)
            c3_s[f] = s3[slot].astype(BF16)
            c2_s[f] = s2[slot].astype(BF16)
            term = chunk_out(x, f)
            acc = term if acc is None else acc + term
        store_rows(acc)

    @pl.when(jnp.logical_and(tv_ref[t] > 0, tfirst_ref[t] == 0))
    def _():
        x = load_rows()
        acc = None
        for f in range(nf):
            term = chunk_out(x, f)
            acc = term if acc is None else acc + term
        store_rows(acc)

    @pl.when(tv_ref[t] == 0)
    def _():
        y_ref[...] = jnp.zeros_like(y_ref)


def _experts_call(xs2d, tile_expert, tile_valid, tile_first, w1, w3, w2, tm, tf):
    rows = xs2d.shape[0] // ROW_TILE
    nt = rows // tm
    dff = w1.shape[2]
    nf = dff // tf
    grid_spec = pltpu.PrefetchScalarGridSpec(
        num_scalar_prefetch=3,
        grid=(nt,),
        in_specs=[pl.BlockSpec((tm * ROW_TILE, LANES), lambda t, te, tv, t1: (t, 0)),
                  pl.BlockSpec(memory_space=pl.ANY), pl.BlockSpec(memory_space=pl.ANY),
                  pl.BlockSpec(memory_space=pl.ANY)],
        out_specs=pl.BlockSpec((tm * ROW_TILE, LANES), lambda t, te, tv, t1: (t, 0)),
        scratch_shapes=[pltpu.VMEM((tm, D_MODEL), BF16),
                        pltpu.VMEM((nf, D_MODEL, tf), BF16), pltpu.VMEM((nf, D_MODEL, tf), BF16),
                        pltpu.VMEM((nf, tf, D_MODEL), BF16),
                        pltpu.VMEM((2, D_MODEL, tf), F32), pltpu.VMEM((2, D_MODEL, tf), F32),
                        pltpu.VMEM((2, tf, D_MODEL), F32),
                        pltpu.SemaphoreType.DMA((2, 3))])
    return pl.pallas_call(
        functools.partial(_experts_body, tm, nf, tf),
        grid_spec=grid_spec,
        out_shape=jax.ShapeDtypeStruct(xs2d.shape, F32),
        compiler_params=pltpu.CompilerParams(dimension_semantics=("arbitrary",),
                                             vmem_limit_bytes=56 * 1024 * 1024),
        name="experts",
    )(tile_expert, tile_valid, tile_first, xs2d, w1, w3, w2)


def _combine_body(tm, final, nb, pos_ref, h_ref, gate_ref, *rest):
    rest = list(rest)
    ng_ref = rest.pop(0) if final else None
    y_ref, o_ref, buf = rest[:3]
    slab = rest[3] if nb is not None else None
    sem = rest[-1]

    def issue(r, c):
        for j in range(2):
            dst = buf.at[j, pl.ds(pl.multiple_of(r * ROW_TILE, ROW_TILE), ROW_TILE), :]
            pltpu.make_async_copy(y_ref.at[pos_ref[0, 0, 2 * r + j]], dst, sem.at[0]).start(priority=j)
        return c

    lax.fori_loop(0, tm, issue, 0)
    _row_copy_wait(y_ref, sem.at[0], 2 * tm)
    g0 = gate_ref[:, 0:1]
    g1 = gate_ref[:, 1:2]

    def row_slab(k):
        cols = slice(k * LANES, (k + 1) * LANES)
        moe = (g0 * buf[0, pl.ds(k, tm, stride=ROW_TILE), :]
               + g1 * buf[1, pl.ds(k, tm, stride=ROW_TILE), :])
        return h_ref[:, cols] + moe

    if nb is None:
        for k in range(ROW_TILE):
            o_ref[:, k * LANES:(k + 1) * LANES] = row_slab(k)
        if final:
            o_ref[...] = _rms(o_ref[...], ng_ref[...])
    else:
        tt = tm // nb
        ssq = None
        for k in range(ROW_TILE):
            s = row_slab(k)
            slab[k] = s
            part = jnp.sum(s * s, axis=-1, keepdims=True)
            ssq = part if ssq is None else ssq + part
        scale = lax.rsqrt(ssq * (1.0 / D_MODEL) + EPS)
        for k in range(ROW_TILE):
            slab[k] = slab[k] * scale * ng_ref[:, k * LANES:(k + 1) * LANES]
        for b in range(nb):
            for k in range(ROW_TILE):
                o_ref[b, :, k * LANES:(k + 1) * LANES] = slab[k, pl.ds(b, tt, stride=nb), :]


def _combine_call(h, gates, pos, y3, tm, tile0, n_tiles, final_g=None, nb=None):
    final = final_g is not None
    n_pos = pos.shape[0] // (2 * tm)
    in_specs = [pl.BlockSpec((1, 1, 2 * tm), lambda i: (tile0 + i, 0, 0), memory_space=pltpu.SMEM),
                pl.BlockSpec((tm, D_MODEL), lambda i: (tile0 + i, 0)),
                pl.BlockSpec((tm, LANES), lambda i: (tile0 + i, 0))]
    args = [pos.reshape(n_pos, 1, 2 * tm), h, gates]
    if final:
        in_specs.append(_const_spec((1, D_MODEL)))
        args.append(final_g)
    in_specs.append(pl.BlockSpec(memory_space=pl.ANY))
    args.append(y3)
    scratch = [pltpu.VMEM((2, tm * ROW_TILE, LANES), F32), pltpu.SemaphoreType.DMA((1,))]
    if nb is None:
        out_spec = pl.BlockSpec((tm, D_MODEL), lambda i: (i, 0))
        out_shape = jax.ShapeDtypeStruct((n_tiles * tm, D_MODEL), F32)
    else:
        tt = tm // nb
        out_spec = pl.BlockSpec((nb, tt, D_MODEL), lambda i: (0, i, 0))
        out_shape = jax.ShapeDtypeStruct((nb, n_tiles * tt, D_MODEL), F32)
        scratch.insert(1, pltpu.VMEM((ROW_TILE, tm, LANES), F32))
    return pl.pallas_call(
        functools.partial(_combine_body, tm, final, nb),
        grid=(n_tiles,),
        in_specs=in_specs,
        out_specs=out_spec,
        out_shape=out_shape,
        scratch_shapes=scratch,
        compiler_params=pltpu.CompilerParams(dimension_semantics=("arbitrary",)),
        name="combine",
    )(*args)


def _to_tm_body(nb, tt, x_ref, tail_ref, o_ref, slab):
    last = pl.program_id(0) == pl.num_programs(0) - 1

    @pl.when(jnp.logical_not(last))
    def _():
        for b in range(nb):
            for k in range(ROW_TILE):
                slab[k, pl.ds(b, tt, stride=nb), :] = x_ref[b, :, k * LANES:(k + 1) * LANES]
        for k in range(ROW_TILE):
            o_ref[:, k * LANES:(k + 1) * LANES] = slab[k]

    @pl.when(last)
    def _():
        o_ref[0:tail_ref.shape[0], :] = tail_ref[...]


def _from_tm_body(nb, tt, x_ref, o_ref, slab):
    for k in range(ROW_TILE):
        slab[k] = x_ref[:, k * LANES:(k + 1) * LANES]
    for b in range(nb):
        for k in range(ROW_TILE):
            o_ref[b, :, k * LANES:(k + 1) * LANES] = slab[k, pl.ds(b, tt, stride=nb), :]


def _to_tm_call(x, tail, tt):
    nb, t, d = x.shape
    ns = tail.shape[0]
    assert ns <= tt * nb
    nsteps = t // tt
    return pl.pallas_call(
        functools.partial(_to_tm_body, nb, tt),
        grid=(nsteps + 1,),
        in_specs=[pl.BlockSpec((nb, tt, d), lambda i: (0, jnp.minimum(i, nsteps - 1), 0)),
                  _const_spec((ns, d))],
        out_specs=pl.BlockSpec((tt * nb, d), lambda i: (i, 0)),
        out_shape=jax.ShapeDtypeStruct((nb * t + ns, d), F32),
        scratch_shapes=[pltpu.VMEM((ROW_TILE, tt * nb, LANES), F32)],
        compiler_params=pltpu.CompilerParams(dimension_semantics=("arbitrary",)),
        name="to_tm",
    )(x, tail)


def _from_tm_call(h, nb, t, tt):
    d = h.shape[1]
    return pl.pallas_call(
        functools.partial(_from_tm_body, nb, tt),
        grid=(t // tt,),
        in_specs=[pl.BlockSpec((tt * nb, d), lambda i: (i, 0))],
        out_specs=pl.BlockSpec((nb, tt, d), lambda i: (0, i, 0)),
        out_shape=jax.ShapeDtypeStruct((nb, t, d), F32),
        scratch_shapes=[pltpu.VMEM((ROW_TILE, tt * nb, LANES), F32)],
        compiler_params=pltpu.CompilerParams(dimension_semantics=("arbitrary",)),
        name="from_tm",
    )(h)


def _rms_body(x_ref, g_ref, o_ref):
    o_ref[...] = _rms(x_ref[...], g_ref[...])


def _rms_call(h, g, tm):
    n = h.shape[0]
    return pl.pallas_call(
        _rms_body, grid=(n // tm,),
        in_specs=[pl.BlockSpec((tm, D_MODEL), lambda i: (i, 0)), _const_spec((1, D_MODEL))],
        out_specs=pl.BlockSpec((tm, D_MODEL), lambda i: (i, 0)),
        out_shape=jax.ShapeDtypeStruct((n, D_MODEL), F32),
        name="rms",
    )(h, g)


MOE_TM = 512
MOE_TF = 512
TOK_TM = 768
FFN_TF = 512


def _moe_layer(h, norm_g, idx, gates, w1, w3, w2, e0, final_g=None, final_bm=None):
    n = h.shape[0]
    e_flat = idx[:, 0:2].reshape(2 * n)
    onehot = (e_flat[:, None] == jnp.arange(N_EXPERTS, dtype=I32)[None, :]).astype(I32)
    csum = jnp.cumsum(onehot, axis=0)
    rank = jnp.sum((csum - onehot) * onehot, axis=1)
    counts = csum[-1]
    padded = ((counts + MOE_TM - 1) // MOE_TM) * MOE_TM
    ends = jnp.cumsum(padded)
    starts = ends - padded
    pos = (jnp.sum(onehot * starts[None, :], axis=1) + rank).astype(I32)
    n_tiles = (2 * n) // MOE_TM + N_EXPERTS
    tile_start = jnp.arange(n_tiles, dtype=I32) * MOE_TM
    tile_valid = (tile_start < ends[-1]).astype(I32)
    tile_expert = jnp.minimum(jnp.sum((tile_start[:, None] >= ends[None, :]).astype(I32), axis=1),
                              N_EXPERTS - 1).astype(I32)
    last_valid = jnp.max(jnp.where(tile_valid > 0, tile_expert, 0))
    tile_expert = jnp.where(tile_valid > 0, tile_expert, last_valid)
    xs = _dispatch_call(h, norm_g, pos, ends.astype(I32), n_tiles * MOE_TM, MOE_TM, TOK_TM)
    prev_expert = jnp.concatenate([jnp.full((1,), -1, I32), tile_expert[:-1]])
    tile_first = jnp.logical_and(tile_expert != prev_expert, tile_valid > 0).astype(I32)
    y2d = _experts_call(xs.reshape(n_tiles * MOE_TM * ROW_TILE, LANES), tile_expert + e0, tile_valid, tile_first,
                        w1, w3, w2, MOE_TM, MOE_TF)
    y3 = y2d.reshape(n_tiles * MOE_TM, ROW_TILE, LANES)
    if final_g is None:
        return _combine_call(h, gates, pos, y3, TOK_TM, 0, n // TOK_TM)
    nb, n_first = final_bm
    tm = n - n_first
    assert n_first % tm == 0 and tm % nb == 0
    y_first = _combine_call(h, gates, pos, y3, tm, 0, n_first // tm, final_g, nb)
    y_rest = _combine_call(h, gates, pos, y3, tm, n_first // tm, 1, final_g)
    return y_first, y_rest


def _block_diag(w):
    n, k, a, b = w.shape
    eye = jnp.eye(k, dtype=w.dtype)
    return jnp.einsum('lkab,kj->lkajb', w, eye).reshape(n, k * a, k * b)


def kernel(x_prompt, x_sample, state_conv, state_lru, state_s5_re, state_s5_im, state_pool, norm_mix_g, w_in, b_in, gmlp_ln_g, gmlp_ln_b, gmlp_w_s, gmlp_b_s, conv_w, conv_b, lru_w_a, lru_b_a, lru_w_x, lru_b_x, lru_lam, s5_a_re, s5_a_im, s5_log_dt, s5_b_re, s5_b_im, s5_c_re, s5_c_im, s5_d, s5_w_glu, s5_b_glu, pool_w, pool_scale, w_branch, w_out, norm_ffn_g, ffn_w1, ffn_w3, ffn_w2, router_w, moe_w1, moe_w3, moe_w2, norm_final_g):
    depth = w_in.shape[0]
    bp, tp, _ = x_prompt.shape
    bs, ts, _ = x_sample.shape
    n_p, n_s = bp * tp, bs * ts
    assert n_p % (CHUNK * bp) == 0 and n_p % n_s == 0

    h = _to_tm_call(x_prompt, jnp.transpose(x_sample, (1, 0, 2)).reshape(n_s, D_MODEL), CHUNK)

    lbr, lbi, bbr, bbi = _s5prep(s5_a_re, s5_a_im, s5_log_dt, s5_b_re, s5_b_im)
    lane_head = jnp.arange(D_BR) // HEAD_DIM

    rows = lambda x: x.reshape(depth, 1, -1)
    bf = lambda x: x.astype(BF16)
    common_tail = (
        conv_w, rows(conv_b),
        bf(_block_diag(lru_w_a)), rows(lru_b_a), bf(_block_diag(lru_w_x)), rows(lru_b_x), rows(lru_lam),
        rows(lbr), rows(lbi),
        bf(jnp.concatenate([_block_diag(bbr), _block_diag(bbi)], axis=2)),
        bf(_block_diag(jnp.transpose(s5_c_re, (0, 1, 3, 2)))),
        bf(_block_diag(jnp.transpose(s5_c_im, (0, 1, 3, 2)))), rows(s5_d),
        bf(s5_w_glu), rows(s5_b_glu),
        bf(_block_diag(pool_w)), rows(pool_scale),
        bf(w_branch.reshape(depth, N_BRANCH * D_BR, D_MODEL)), bf(w_out))
    common_head = (rows(norm_mix_g), bf(w_in), rows(b_in), rows(gmlp_ln_g), rows(gmlp_ln_b))
    bsm_p = jnp.transpose(gmlp_b_s[:, lane_head, :], (0, 2, 1))
    lp_p = common_head + (gmlp_w_s, bsm_p) + common_tail
    ws_small = jnp.transpose(gmlp_w_s[:, :, :ts, :ts], (0, 2, 3, 1))[..., lane_head].reshape(depth, ts * ts, D_BR)
    bsm_s = jnp.transpose(gmlp_b_s[:, lane_head, :ts], (0, 2, 1))
    lp_s = common_head + (ws_small, bsm_s) + common_tail
    st_s = (jnp.transpose(state_conv, (0, 2, 1, 3)).reshape(depth, (CONV_W - 1) * bs, D_BR),
            state_lru, state_s5_re.reshape(depth, bs, S5_W), state_s5_im.reshape(depth, bs, S5_W),
            jnp.transpose(state_pool, (0, 2, 1, 3)).reshape(depth, POOL_BUF * bs, D_BR))
    row = lambda x: x.reshape(1, -1)

    conv_p, lru_p, sre_p, sim_p, pool_p = [], [], [], [], []
    conv_s, lru_s, sre_s, sim_s, pool_s, v_s = [], [], [], [], [], []
    zeros_p = (jnp.zeros(((CONV_W - 1) * bp, D_BR), F32), jnp.zeros((bp, D_BR), F32),
               jnp.zeros((bp, S5_W), F32), jnp.zeros((bp, S5_W), F32),
               jnp.zeros((POOL_BUF * bp, D_BR), F32))
    for l in range(depth):
        router = None
        if l % 2 == 1:
            rw = jnp.pad(norm_ffn_g[l][:, None] * router_w[l // 2], ((0, 0), (0, LANES - N_EXPERTS)))
            rw_hi = rw.astype(BF16)
            router = (rw_hi, (rw - rw_hi.astype(F32)).astype(BF16))
        outs_p = _mixer_call(h, 0, tp // CHUNK, bp, CHUNK, 0, True, False, zeros_p, None, lp_p, l, router)
        outs_s = _mixer_call(outs_p[0], n_p // n_s, 1, bs, ts, PAST_LEN, False, True, st_s, l, lp_s, l, router)
        h = outs_s[0]
        for lst, o in zip((conv_p, lru_p, sre_p, sim_p, pool_p), outs_p[1:6]):
            lst.append(o)
        for lst, o in zip((conv_s, lru_s, sre_s, sim_s, pool_s, v_s), outs_s[1:7]):
            lst.append(o)

        if l % 2 == 0:
            k = l // 2
            h = _ffn_call(h, row(norm_ffn_g[l]), ffn_w1[k].astype(BF16), ffn_w3[k].astype(BF16),
                          ffn_w2[k].astype(BF16), TOK_TM, FFN_TF)
            if l == depth - 1:
                h = _rms_call(h, row(norm_final_g), TOK_TM)
        else:
            k = l // 2
            idx = jnp.concatenate([outs_p[-2], outs_s[-2]], axis=0)
            gates = jnp.concatenate([outs_p[-1], outs_s[-1]], axis=0)
            stk = lambda w: w.reshape((-1,) + w.shape[2:])
            if l == depth - 1:
                y_prompt, y_sample_tm = _moe_layer(h, row(norm_ffn_g[l]), idx, gates, stk(moe_w1), stk(moe_w3),
                                                   stk(moe_w2), k * N_EXPERTS, row(norm_final_g), (bp, n_p))
            else:
                h = _moe_layer(h, row(norm_ffn_g[l]), idx, gates, stk(moe_w1), stk(moe_w3), stk(moe_w2),
                               k * N_EXPERTS)

    def tm_to_bm(x, t, b):
        return jnp.transpose(x.reshape(t, b, x.shape[-1]), (1, 0, 2))

    if depth % 2 == 1:
        y_prompt = _from_tm_call(h, bp, tp, CHUNK)
        y_sample_tm = h[n_p:]
    y_sample = tm_to_bm(y_sample_tm, ts, bs)
    stack = lambda lst, f: jnp.stack([f(o) for o in lst])
    return (
        y_prompt, y_sample,
        stack(conv_p, lambda o: tm_to_bm(o, CONV_W - 1, bp)),
        stack(lru_p, lambda o: o),
        stack(sre_p, lambda o: o.reshape(bp, N_GROUPS_C, P_STATE)),
        stack(sim_p, lambda o: o.reshape(bp, N_GROUPS_C, P_STATE)),
        stack(pool_p, lambda o: tm_to_bm(o, POOL_BUF, bp)),
        stack(conv_s, lambda o: tm_to_bm(o, CONV_W - 1, bs)),
        stack(lru_s, lambda o: o),
        stack(sre_s, lambda o: o.reshape(bs, N_GROUPS_C, P_STATE)),
        stack(sim_s, lambda o: o.reshape(bs, N_GROUPS_C, P_STATE)),
        stack(pool_s, lambda o: tm_to_bm(o, POOL_BUF, bs)),
        stack(v_s, lambda o: tm_to_bm(o, ts, bs)),
    )
```

```python
import functools
import math

import jax
import jax.numpy as jnp
from jax import lax
from jax.experimental import pallas as pl
from jax.experimental.pallas import tpu as pltpu

F32 = jnp.float32
BF16 = jnp.bfloat16
I32 = jnp.int32

SUBLANES = 8
LANES = 128

D_MODEL = 1024
D_BR = 256
N_BRANCH = 4
N_HEADS = 4
HEAD_DIM = D_BR // N_HEADS
CHUNK = 128
CONV_W = 4
LRU_C = 8.0
N_GROUPS_C = 16
GROUP_C = 16
P_STATE = 64
S5_W = N_GROUPS_C * P_STATE
POOL_WINDOWS = (2, 4, 8, 16)
POOL_BUF = max(POOL_WINDOWS) - 1
GROUP_D = D_BR // len(POOL_WINDOWS)
COLS_A = 5 * D_BR
IN_COLS = COLS_A + N_BRANCH * D_MODEL
N_EXPERTS = 8
EPS = 1e-6
PAST_LEN = 16384
SQRT_2_OVER_PI = math.sqrt(2.0 / math.pi)
ROW_TILE = D_MODEL // LANES


def _gelu(x):
    return x * (0.5 * (1.0 + jnp.tanh(SQRT_2_OVER_PI * (x + 0.044715 * (x * x * x)))))


def _sigmoid(x):
    return 1.0 / (1.0 + jnp.exp(-x))


def _rms(x, g):
    return x * lax.rsqrt(jnp.mean(x * x, axis=-1, keepdims=True) + EPS) * g


def _dot(a, b):
    return jnp.dot(a, b, preferred_element_type=F32)


def _top2(logits):
    lane = lax.broadcasted_iota(I32, logits.shape, 1)
    neg = jnp.float32(-jnp.inf)
    logits = jnp.where(lane < N_EXPERTS, logits, neg)
    m1 = jnp.max(logits, axis=-1, keepdims=True)
    i1 = jnp.min(jnp.where(logits == m1, lane, LANES), axis=-1, keepdims=True)
    rest = jnp.where(lane == i1, neg, logits)
    m2 = jnp.max(rest, axis=-1, keepdims=True)
    i2 = jnp.min(jnp.where(rest == m2, lane, LANES), axis=-1, keepdims=True)
    ex = jnp.exp(m2 - m1)
    g1 = 1.0 / (1.0 + ex)
    g2 = ex / (1.0 + ex)
    idx = jnp.where(lane == 0, i1, jnp.where(lane == 1, i2, 0))
    gate = jnp.where(lane == 0, g1, jnp.where(lane == 1, g2, 0.0))
    return idx, gate


def _const_spec(shape, single=False):
    nd = len(shape)
    if single:
        return pl.BlockSpec(shape, lambda *_: (0,) * nd, pipeline_mode=pl.Buffered(1))
    return pl.BlockSpec(shape, lambda *_: (0,) * nd)


def _layer_spec(shape, layer, single=False):
    idx = (layer,) + (0,) * (len(shape) - 1)
    blk = (None,) + tuple(shape[1:])
    if single:
        return pl.BlockSpec(blk, lambda *_: idx, pipeline_mode=pl.Buffered(1))
    return pl.BlockSpec(blk, lambda *_: idx)


def _mixer_body(cfg, *refs):
    B, Tt, start_pos, mm_gmlp, emit_v, route = cfg
    R = Tt * B
    refs = list(refs)
    h_ref = refs.pop(0)
    (conv0_ref, lru0_ref, sre0_ref, sim0_ref, pool0_ref,
     ng_ref, win_ref, bin_ref, lng_ref, lnb_ref, ws_ref, bsm_ref,
     cw_ref, cb_ref, wa_ref, ba_ref, wx_ref, bx_ref, lam_ref,
     lbr_ref, lbi_ref, bblk_ref, cre_ref, cim_ref, d_ref, wglu_ref, bglu_ref,
     pw_ref, ps_ref, wb_ref, wo_ref) = refs[:31]
    n_in = 33 if route else 31
    if route:
        rwh_ref, rwl_ref = refs[31:33]
    n_out = 6 + int(emit_v) + 2 * int(route)
    outs = refs[n_in:n_in + n_out]
    out_ref, convo_ref, lruo_ref, sreo_ref, simo_ref, poolo_ref = outs[:6]
    if route:
        idx_ref, gate_ref = outs[-2:]
    (xn_s, cext_s, pext_s, lruh_s, sre_s, sim_s, u_s, v_s, mix_s, xc_s,
     a_s, b_s, yc_s, yd_s, s5_s) = refs[n_in + n_out:]

    i = pl.program_id(0)
    SB = min(R, 256)
    SB5 = max(B, min(R, 256))
    lane256 = lax.broadcasted_iota(I32, (1, D_BR), 1)
    ydt = yc_s.dtype

    def mm(a, w):
        return _dot(a.astype(BF16), w)

    def init_carries():
        cext_s[0:(CONV_W - 1) * B, :] = conv0_ref[...]
        pext_s[0:POOL_BUF * B, :] = pool0_ref[...]
        lruh_s[...] = lru0_ref[...]
        sre_s[...] = sre0_ref[...]
        sim_s[...] = sim0_ref[...]

    def blocks(n, size, fn):
        for k in range(n):
            fn(k * size)

    def stage_in(r0):
        rows = pl.ds(r0, SB)
        xn = _rms(h_ref[rows, :], ng_ref[...]).astype(BF16)
        xn_s[rows, :] = xn
        pa = _dot(xn, win_ref[:, 0:COLS_A]) + bin_ref[:, 0:COLS_A]
        u_s[rows, :] = _gelu(pa[:, 0:D_BR])
        gv = _gelu(pa[:, D_BR:2 * D_BR])
        mu = jnp.mean(gv, axis=-1, keepdims=True)
        var = jnp.mean(jnp.square(gv - mu), axis=-1, keepdims=True)
        v = (gv - mu) * lax.rsqrt(var + EPS) * lng_ref[...] + lnb_ref[...]
        v_s[0, rows, :] = v[:, 0:LANES]
        v_s[1, rows, :] = v[:, LANES:2 * LANES]
        cext_s[pl.ds(r0 + (CONV_W - 1) * B, SB), :] = pa[:, 2 * D_BR:3 * D_BR]
        xc_s[rows, :] = pa[:, 3 * D_BR:4 * D_BR]
        pext_s[pl.ds(r0 + POOL_BUF * B, SB), :] = pa[:, 4 * D_BR:5 * D_BR]

    def stage_gmlp():
        if mm_gmlp:
            tril = (lax.broadcasted_iota(I32, (CHUNK, CHUNK), 0)
                    >= lax.broadcasted_iota(I32, (CHUNK, CHUNK), 1))
            wm = [jnp.where(tril, ws_ref[hd], 0.0).astype(BF16) for hd in range(N_HEADS)]
            head = lane256 // HEAD_DIM
            for b in range(B):
                vb = jnp.concatenate([v_s[0, pl.ds(b, Tt, stride=B), :],
                                      v_s[1, pl.ds(b, Tt, stride=B), :]], axis=1).astype(BF16)
                mixed = bsm_ref[...]
                for hd in range(N_HEADS):
                    mixed = mixed + jnp.where(head == hd, mm(wm[hd], vb), 0.0)
                mix_s[0, pl.ds(b, Tt, stride=B), :] = mixed[:, 0:LANES]
                mix_s[1, pl.ds(b, Tt, stride=B), :] = mixed[:, LANES:2 * LANES]
        else:
            for t in range(Tt):
                for half in range(2):
                    lo = half * LANES
                    acc = jnp.broadcast_to(bsm_ref[t:t + 1, lo:lo + LANES], (B, LANES))
                    for s in range(t + 1):
                        w = ws_ref[t * Tt + s:t * Tt + s + 1, lo:lo + LANES]
                        acc = acc + w * v_s[half, s * B:(s + 1) * B, :]
                    mix_s[half, t * B:(t + 1) * B, :] = acc
        if emit_v:
            vo_ref = outs[6]
            vo_ref[:, 0:LANES] = v_s[0]
            vo_ref[:, LANES:2 * LANES] = v_s[1]

    log_sig_lam = (jnp.minimum(lam_ref[...], 0.0)
                   - jnp.log1p(jnp.exp(-jnp.abs(lam_ref[...]))))

    def stage_lru(r0):
        rows = pl.ds(r0, SB)
        conv = cb_ref[...] + cext_s[pl.ds(r0, SB), :] * cw_ref[0:1, :]
        for k in range(1, CONV_W):
            conv = conv + cext_s[pl.ds(r0 + k * B, SB), :] * cw_ref[k:k + 1, :]
        cbf = conv.astype(BF16)
        r = _sigmoid(mm(cbf, wa_ref[...]) + ba_ref[...])
        ig = _sigmoid(mm(cbf, wx_ref[...]) + bx_ref[...])
        a = jnp.exp(LRU_C * r * log_sig_lam)
        mult = jnp.sqrt(1.0 - a * a)
        if start_pos == 0:
            row = lax.broadcasted_iota(I32, (SB, D_BR), 0) + r0
            mult = jnp.where(jnp.logical_and(i == 0, row < B), 1.0, mult)
        a_s[rows, :] = a
        b_s[rows, :] = mult * ig * conv

    def lru_step(t, hprev):
        rows = pl.ds(t * B, B) if isinstance(t, int) else pl.ds(pl.multiple_of(t * B, B), B)
        hnew = a_s[rows, :] * hprev + b_s[rows, :]
        b_s[rows, :] = hnew
        return hnew

    def lru_scan():
        hl = lruh_s[...]
        for t in range(Tt):
            hl = lru_step(t, hl)
        lruh_s[...] = hl
        lruo_ref[...] = hl
        conv_tail = cext_s[R:R + (CONV_W - 1) * B, :]
        convo_ref[...] = conv_tail
        cext_s[0:(CONV_W - 1) * B, :] = conv_tail

    lbr = jnp.broadcast_to(lbr_ref[...], (B, S5_W))
    lbi = jnp.broadcast_to(lbi_ref[...], (B, S5_W))
    steps5 = SB5 // B

    def stage_s5(r0):
        rows = pl.ds(r0, SB5)
        xc = xc_s[rows, :]
        s5_s[...] = mm(xc, bblk_ref[...])

        def s5_step(t, carry):
            sre, sim = carry
            rr = pl.ds(t * B, B) if isinstance(t, int) else pl.ds(pl.multiple_of(t * B, B), B)
            nre = lbr * sre - lbi * sim + s5_s[rr, 0:S5_W]
            nim = lbr * sim + lbi * sre + s5_s[rr, S5_W:2 * S5_W]
            s5_s[rr, 0:S5_W] = nre
            s5_s[rr, S5_W:2 * S5_W] = nim
            return nre, nim

        carry = (sre_s[...], sim_s[...])
        for t in range(steps5):
            carry = s5_step(t, carry)
        sre_s[...] = carry[0]
        sim_s[...] = carry[1]
        y = (mm(s5_s[:, 0:S5_W], cre_ref[...])
             - mm(s5_s[:, S5_W:2 * S5_W], cim_ref[...])
             + d_ref[...] * xc)
        y = _gelu(y)
        y = y * _sigmoid(mm(y, wglu_ref[...]) + bglu_ref[...])
        yc_s[rows, :] = y.astype(ydt)

    wlane = jnp.where(lane256 < GROUP_D, POOL_WINDOWS[0],
                      jnp.where(lane256 < 2 * GROUP_D, POOL_WINDOWS[1],
                                jnp.where(lane256 < 3 * GROUP_D, POOL_WINDOWS[2], POOL_WINDOWS[3])))

    def stage_pool(r0):
        base = r0 + POOL_BUF * B
        tok = pext_s[pl.ds(base, SB), :]
        acc = tok
        sums = {}
        for j in range(1, max(POOL_WINDOWS)):
            acc = acc + pext_s[pl.ds(base - j * B, SB), :]
            if j + 1 in POOL_WINDOWS:
                sums[j + 1] = acc
        sel = jnp.where(lane256 < GROUP_D, sums[2],
                        jnp.where(lane256 < 2 * GROUP_D, sums[4],
                                  jnp.where(lane256 < 3 * GROUP_D, sums[8], sums[16])))
        if start_pos >= POOL_BUF:
            cnt = wlane.astype(F32)
        else:
            row = lax.broadcasted_iota(I32, (SB, D_BR), 0) + r0
            tpos = start_pos + i * Tt + lax.shift_right_logical(row, jnp.full_like(row, int(math.log2(B))))
            cnt = jnp.minimum(wlane, tpos + 1).astype(F32)
        diff = sel / cnt - tok
        yd_s[pl.ds(r0, SB), :] = (mm(diff, pw_ref[...]) * ps_ref[...]).astype(ydt)

    def pool_tail():
        tail = pext_s[R:R + POOL_BUF * B, :]
        poolo_ref[...] = tail
        pext_s[0:POOL_BUF * B, :] = tail

    SBM = min(R, 256)

    def stage_merge(r0):
        rows = pl.ds(r0, SBM)
        xn = xn_s[rows, :]
        ya = u_s[rows, :] * jnp.concatenate([mix_s[0, rows, :], mix_s[1, rows, :]], axis=1)
        ys = (ya, b_s[rows, :], yc_s[rows, :], yd_s[rows, :])
        merged = None
        for n in range(N_BRANCH):
            c0 = COLS_A + n * D_MODEL
            logits = _dot(xn, win_ref[:, c0:c0 + D_MODEL]) + bin_ref[:, c0:c0 + D_MODEL]
            term = _sigmoid(logits) * mm(ys[n], wb_ref[n * D_BR:(n + 1) * D_BR, :])
            merged = term if merged is None else merged + term
        out = h_ref[rows, :] + mm(merged, wo_ref[...])
        out_ref[rows, :] = out
        if route:
            ohi = out.astype(BF16)
            olo = (out - ohi.astype(F32)).astype(BF16)
            z = _dot(ohi, rwh_ref[...]) + (_dot(olo, rwh_ref[...]) + _dot(ohi, rwl_ref[...]))
            logits = z * lax.rsqrt(jnp.mean(out * out, axis=-1, keepdims=True) + EPS)
            idx, gate = _top2(logits)
            idx_ref[rows, :] = idx
            gate_ref[rows, :] = gate

    def stage_front(r0):
        stage_in(r0)
        stage_lru(r0)
        stage_pool(r0)

    def sequence_stages():
        blocks(R // SB, SB, stage_front)
        stage_gmlp()
        lru_scan()
        pool_tail()

    def merge_rows(r0):
        for j in range(SB5 // SBM):
            rj = r0 + j * SBM
            stage_merge(rj if isinstance(rj, int) else pl.multiple_of(rj, SBM))

    nb5 = R // SB5
    pl.when(i == 0)(init_carries)
    sequence_stages()
    stage_s5(0)
    for k in range(1, nb5):
        stage_s5(k * SB5)
        merge_rows((k - 1) * SB5)
    merge_rows((nb5 - 1) * SB5)
    sreo_ref[...] = sre_s[...]
    simo_ref[...] = sim_s[...]


def _mixer_call(h_all, blk0, nsteps, B, Tt, start_pos, mm_gmlp, emit_v, states, state_layer, lp, layer,
                router=None):
    R = Tt * B
    route = router is not None
    cfg = (B, Tt, start_pos, mm_gmlp, emit_v, route)
    small = list(states)
    params = list(lp)
    big = {1, 16, 24, 25}
    h_spec = pl.BlockSpec((R, D_MODEL), lambda i: (blk0 + i, 0))
    in_specs = [h_spec]
    if state_layer is None:
        in_specs += [_const_spec(a.shape) for a in small]
        st_shapes = [a.shape for a in small]
    else:
        in_specs += [_layer_spec(a.shape, state_layer) for a in small]
        st_shapes = [a.shape[1:] for a in small]
    in_specs += [_layer_spec(a.shape, layer, single=(k in big)) for k, a in enumerate(params)]
    out_shape = [jax.ShapeDtypeStruct(h_all.shape, F32)] + [jax.ShapeDtypeStruct(s, F32) for s in st_shapes]
    out_specs = [h_spec] + [_const_spec(s.shape) for s in out_shape[1:]]
    if emit_v:
        out_shape.append(jax.ShapeDtypeStruct((R, D_BR), F32))
        out_specs.append(_const_spec((R, D_BR)))
    extra = []
    if route:
        extra = list(router)
        in_specs += [_const_spec(a.shape) for a in extra]
        out_shape += [jax.ShapeDtypeStruct((nsteps * R, LANES), I32),
                      jax.ShapeDtypeStruct((nsteps * R, LANES), F32)]
        out_specs += [pl.BlockSpec((R, LANES), lambda i: (i, 0))] * 2
    SB5 = max(B, min(R, 256))
    scratch = [
        pltpu.VMEM((R, D_MODEL), BF16),
        pltpu.VMEM((R + (CONV_W - 1) * B, D_BR), F32),
        pltpu.VMEM((R + POOL_BUF * B, D_BR), F32),
        pltpu.VMEM((B, D_BR), F32),
        pltpu.VMEM((B, S5_W), F32), pltpu.VMEM((B, S5_W), F32),
        pltpu.VMEM((R, D_BR), F32),
        pltpu.VMEM((2, R, LANES), F32),
        pltpu.VMEM((2, R, LANES), F32),
        pltpu.VMEM((R, D_BR), F32),
        pltpu.VMEM((R, D_BR), F32), pltpu.VMEM((R, D_BR), F32),
        pltpu.VMEM((R, D_BR), BF16), pltpu.VMEM((R, D_BR), BF16),
        pltpu.VMEM((SB5, 2 * S5_W), F32),
    ]
    return pl.pallas_call(
        functools.partial(_mixer_body, cfg),
        grid=(nsteps,),
        in_specs=in_specs,
        out_specs=out_specs,
        out_shape=out_shape,
        scratch_shapes=scratch,
        input_output_aliases={0: 0},
        compiler_params=pltpu.CompilerParams(dimension_semantics=("arbitrary",),
                                             vmem_limit_bytes=60 * 1024 * 1024),
        name="mixer_b%d" % B,
    )(h_all, *small, *params, *extra)


def _s5prep_body(are_ref, aim_ref, ldt_ref, bre_ref, bim_ref, lbr_ref, lbi_ref, bbr_ref, bbi_ref):
    a_re, a_im = are_ref[...], aim_ref[...]
    dt = jnp.exp(ldt_ref[...])
    mag = jnp.exp(a_re * dt)
    lb_re = mag * jnp.cos(a_im * dt)
    lb_im = mag * jnp.sin(a_im * dt)
    den = a_re * a_re + a_im * a_im
    n_re = lb_re - 1.0
    q_re = (n_re * a_re + lb_im * a_im) / den
    q_im = (lb_im * a_re - n_re * a_im) / den
    lbr_ref[...] = lb_re
    lbi_ref[...] = lb_im
    bbr_ref[...] = q_re * bre_ref[...] - q_im * bim_ref[...]
    bbi_ref[...] = q_re * bim_ref[...] + q_im * bre_ref[...]


def _s5prep(a_re, a_im, log_dt, b_re, b_im):
    depth = a_re.shape[0]
    rows = depth * N_GROUPS_C * GROUP_C

    def rep(x):
        return jnp.broadcast_to(x[:, :, None, :], (depth, N_GROUPS_C, GROUP_C, P_STATE)).reshape(rows, P_STATE)

    ldt = jnp.broadcast_to(log_dt[:, :, None, None], (depth, N_GROUPS_C, GROUP_C, P_STATE)).reshape(rows, P_STATE)
    b_re_t = jnp.transpose(b_re, (0, 1, 3, 2)).reshape(rows, P_STATE)
    b_im_t = jnp.transpose(b_im, (0, 1, 3, 2)).reshape(rows, P_STATE)
    shp = jax.ShapeDtypeStruct((rows, P_STATE), F32)
    lbr, lbi, bbr, bbi = pl.pallas_call(_s5prep_body, out_shape=[shp] * 4, name="s5prep")(
        rep(a_re), rep(a_im), ldt, b_re_t, b_im_t)
    r4 = lambda x: x.reshape(depth, N_GROUPS_C, GROUP_C, P_STATE)
    return r4(lbr)[:, :, 0, :], r4(lbi)[:, :, 0, :], r4(bbr), r4(bbi)


def _ffn_body(chunks, x_ref, g_ref, w1_ref, w3_ref, w2_ref, o_ref):
    x = x_ref[...]
    xn = _rms(x, g_ref[...]).astype(BF16)
    acc = None
    for c0, cw in chunks:
        a = _dot(xn, w1_ref[:, c0:c0 + cw])
        b = _dot(xn, w3_ref[:, c0:c0 + cw])
        t = _dot((a * _sigmoid(a) * b).astype(BF16), w2_ref[c0:c0 + cw, :])
        acc = t if acc is None else acc + t
    o_ref[...] = x + acc


def _ffn_call(h, g, w1, w3, w2, tm, tf):
    dff = w1.shape[1]
    chunks = tuple((c0, min(tf, dff - c0)) for c0 in range(0, dff, tf))
    return pl.pallas_call(
        functools.partial(_ffn_body, chunks),
        grid=(h.shape[0] // tm,),
        in_specs=[pl.BlockSpec((tm, D_MODEL), lambda i: (i, 0)),
                  _const_spec((1, D_MODEL)),
                  _const_spec((D_MODEL, dff), single=True),
                  _const_spec((D_MODEL, dff), single=True),
                  _const_spec((dff, D_MODEL), single=True)],
        out_specs=pl.BlockSpec((tm, D_MODEL), lambda i: (i, 0)),
        out_shape=jax.ShapeDtypeStruct(h.shape, F32),
        input_output_aliases={0: 0},
        compiler_params=pltpu.CompilerParams(dimension_semantics=("arbitrary",),
                                             vmem_limit_bytes=56 * 1024 * 1024),
        name="ffn",
    )(h, g, w1, w3, w2)


def _row_copy_wait(hbm_ref, sem, nrows):
    pltpu.make_async_copy(hbm_ref.at[pl.ds(0, nrows)], hbm_ref.at[pl.ds(0, nrows)], sem).wait()


def _dispatch_body(tm, pad_rows, pos_ref, ends_ref, x_ref, g_ref, xs_ref, buf, zbuf, sem):
    @pl.when(pl.program_id(0) == 0)
    def _():
        zbuf[...] = jnp.zeros_like(zbuf)
        n_rows = xs_ref.shape[0]
        for e in range(N_EXPERTS):
            for start in (jnp.maximum(ends_ref[e] - pad_rows, 0), n_rows - (e + 1) * pad_rows):
                cp = pltpu.make_async_copy(zbuf, xs_ref.at[pl.ds(start, pad_rows)], sem.at[1])
                cp.start()
                cp.wait()

    xn = _rms(x_ref[...], g_ref[...])
    for k in range(ROW_TILE):
        buf[pl.ds(k, tm, stride=ROW_TILE), :] = xn[:, k * LANES:(k + 1) * LANES]

    def issue(r, c):
        src = buf.at[pl.ds(pl.multiple_of(r * ROW_TILE, ROW_TILE), ROW_TILE), :]
        pltpu.make_async_copy(src, xs_ref.at[pos_ref[0, 0, 2 * r]], sem.at[0]).start(priority=0)
        pltpu.make_async_copy(src, xs_ref.at[pos_ref[0, 0, 2 * r + 1]], sem.at[0]).start(priority=1)
        return c

    lax.fori_loop(0, tm, issue, 0)
    _row_copy_wait(xs_ref, sem.at[0], 2 * tm)


def _dispatch_call(h, g, pos, ends, n_rows, pad_rows, tm):
    n = h.shape[0]
    nt = n // tm
    return pl.pallas_call(
        functools.partial(_dispatch_body, tm, pad_rows),
        grid=(nt,),
        in_specs=[pl.BlockSpec((1, 1, 2 * tm), lambda i: (i, 0, 0), memory_space=pltpu.SMEM),
                  pl.BlockSpec(memory_space=pltpu.SMEM),
                  pl.BlockSpec((tm, D_MODEL), lambda i: (i, 0)),
                  _const_spec((1, D_MODEL))],
        out_specs=pl.BlockSpec(memory_space=pl.ANY),
        out_shape=jax.ShapeDtypeStruct((n_rows, ROW_TILE, LANES), F32),
        scratch_shapes=[pltpu.VMEM((tm * ROW_TILE, LANES), F32),
                        pltpu.VMEM((pad_rows, ROW_TILE, LANES), F32),
                        pltpu.SemaphoreType.DMA((2,))],
        compiler_params=pltpu.CompilerParams(dimension_semantics=("arbitrary",)),
        name="dispatch",
    )(pos.reshape(nt, 1, 2 * tm), ends, h, g)


def _experts_body(tm, nf, tf, te_ref, tv_ref, tfirst_ref, xs_ref, w1_hbm, w3_hbm, w2_hbm, y_ref,
                  x_s, c1_s, c3_s, c2_s, s1, s3, s2, sem):
    t = pl.program_id(0)
    e = te_ref[t]

    def chunk_copies(f, slot):
        cols = pl.ds(f * tf, tf)
        return (pltpu.make_async_copy(w1_hbm.at[e, :, cols], s1.at[slot], sem.at[slot, 0]),
                pltpu.make_async_copy(w3_hbm.at[e, :, cols], s3.at[slot], sem.at[slot, 1]),
                pltpu.make_async_copy(w2_hbm.at[e, cols, :], s2.at[slot], sem.at[slot, 2]))

    def load_rows():
        for k in range(ROW_TILE):
            x_s[:, k * LANES:(k + 1) * LANES] = xs_ref[pl.ds(k, tm, stride=ROW_TILE), :].astype(BF16)
        return x_s[...]

    def chunk_out(x, f):
        a = _dot(x, c1_s[f])
        b = _dot(x, c3_s[f])
        return _dot((a * _sigmoid(a) * b).astype(BF16), c2_s[f])

    def store_rows(acc):
        for k in range(ROW_TILE):
            y_ref[pl.ds(k, tm, stride=ROW_TILE), :] = acc[:, k * LANES:(k + 1) * LANES]

    @pl.when(tfirst_ref[t] > 0)
    def _():
        for cp in chunk_copies(0, 0):
            cp.start()
        x = load_rows()
        acc = None
        for f in range(nf):
            slot = f % 2
            if f + 1 < nf:
                for cp in chunk_copies(f + 1, 1 - slot):
                    cp.start()
            for cp in chunk_copies(f, slot):
                cp.wait()
            c1_s[f] = s1[slot].astype(BF16)
            c3_s[f] = s3[slot].astype(BF16)
            c2_s[f] = s2[slot].astype(BF16)
            term = chunk_out(x, f)
            acc = term if acc is None else acc + term
        store_rows(acc)

    @pl.when(jnp.logical_and(tv_ref[t] > 0, tfirst_ref[t] == 0))
    def _():
        x = load_rows()
        acc = None
        for f in range(nf):
            term = chunk_out(x, f)
            acc = term if acc is None else acc + term
        store_rows(acc)

    @pl.when(tv_ref[t] == 0)
    def _():
        y_ref[...] = jnp.zeros_like(y_ref)


def _experts_call(xs2d, tile_expert, tile_valid, tile_first, w1, w3, w2, tm, tf):
    rows = xs2d.shape[0] // ROW_TILE
    nt = rows // tm
    dff = w1.shape[2]
    nf = dff // tf
    grid_spec = pltpu.PrefetchScalarGridSpec(
        num_scalar_prefetch=3,
        grid=(nt,),
        in_specs=[pl.BlockSpec((tm * ROW_TILE, LANES), lambda t, te, tv, t1: (t, 0)),
                  pl.BlockSpec(memory_space=pl.ANY), pl.BlockSpec(memory_space=pl.ANY),
                  pl.BlockSpec(memory_space=pl.ANY)],
        out_specs=pl.BlockSpec((tm * ROW_TILE, LANES), lambda t, te, tv, t1: (t, 0)),
        scratch_shapes=[pltpu.VMEM((tm, D_MODEL), BF16),
                        pltpu.VMEM((nf, D_MODEL, tf), BF16), pltpu.VMEM((nf, D_MODEL, tf), BF16),
                        pltpu.VMEM((nf, tf, D_MODEL), BF16),
                        pltpu.VMEM((2, D_MODEL, tf), F32), pltpu.VMEM((2, D_MODEL, tf), F32),
                        pltpu.VMEM((2, tf, D_MODEL), F32),
                        pltpu.SemaphoreType.DMA((2, 3))])
    return pl.pallas_call(
        functools.partial(_experts_body, tm, nf, tf),
        grid_spec=grid_spec,
        out_shape=jax.ShapeDtypeStruct(xs2d.shape, F32),
        compiler_params=pltpu.CompilerParams(dimension_semantics=("arbitrary",),
                                             vmem_limit_bytes=56 * 1024 * 1024),
        name="experts",
    )(tile_expert, tile_valid, tile_first, xs2d, w1, w3, w2)


def _combine_body(tm, final, nb, pos_ref, h_ref, gate_ref, *rest):
    rest = list(rest)
    ng_ref = rest.pop(0) if final else None
    y_ref, o_ref, buf = rest[:3]
    slab = rest[3] if nb is not None else None
    sem = rest[-1]

    def issue(r, c):
        for j in range(2):
            dst = buf.at[j, pl.ds(pl.multiple_of(r * ROW_TILE, ROW_TILE), ROW_TILE), :]
            pltpu.make_async_copy(y_ref.at[pos_ref[0, 0, 2 * r + j]], dst, sem.at[0]).start(priority=j)
        return c

    lax.fori_loop(0, tm, issue, 0)
    _row_copy_wait(y_ref, sem.at[0], 2 * tm)
    g0 = gate_ref[:, 0:1]
    g1 = gate_ref[:, 1:2]

    def row_slab(k):
        cols = slice(k * LANES, (k + 1) * LANES)
        moe = (g0 * buf[0, pl.ds(k, tm, stride=ROW_TILE), :]
               + g1 * buf[1, pl.ds(k, tm, stride=ROW_TILE), :])
        return h_ref[:, cols] + moe

    if nb is None:
        for k in range(ROW_TILE):
            o_ref[:, k * LANES:(k + 1) * LANES] = row_slab(k)
        if final:
            o_ref[...] = _rms(o_ref[...], ng_ref[...])
    else:
        tt = tm // nb
        ssq = None
        for k in range(ROW_TILE):
            s = row_slab(k)
            slab[k] = s
            part = jnp.sum(s * s, axis=-1, keepdims=True)
            ssq = part if ssq is None else ssq + part
        scale = lax.rsqrt(ssq * (1.0 / D_MODEL) + EPS)
        for k in range(ROW_TILE):
            slab[k] = slab[k] * scale * ng_ref[:, k * LANES:(k + 1) * LANES]
        for b in range(nb):
            for k in range(ROW_TILE):
                o_ref[b, :, k * LANES:(k + 1) * LANES] = slab[k, pl.ds(b, tt, stride=nb), :]


def _combine_call(h, gates, pos, y3, tm, tile0, n_tiles, final_g=None, nb=None):
    final = final_g is not None
    n_pos = pos.shape[0] // (2 * tm)
    in_specs = [pl.BlockSpec((1, 1, 2 * tm), lambda i: (tile0 + i, 0, 0), memory_space=pltpu.SMEM),
                pl.BlockSpec((tm, D_MODEL), lambda i: (tile0 + i, 0)),
                pl.BlockSpec((tm, LANES), lambda i: (tile0 + i, 0))]
    args = [pos.reshape(n_pos, 1, 2 * tm), h, gates]
    if final:
        in_specs.append(_const_spec((1, D_MODEL)))
        args.append(final_g)
    in_specs.append(pl.BlockSpec(memory_space=pl.ANY))
    args.append(y3)
    scratch = [pltpu.VMEM((2, tm * ROW_TILE, LANES), F32), pltpu.SemaphoreType.DMA((1,))]
    if nb is None:
        out_spec = pl.BlockSpec((tm, D_MODEL), lambda i: (i, 0))
        out_shape = jax.ShapeDtypeStruct((n_tiles * tm, D_MODEL), F32)
    else:
        tt = tm // nb
        out_spec = pl.BlockSpec((nb, tt, D_MODEL), lambda i: (0, i, 0))
        out_shape = jax.ShapeDtypeStruct((nb, n_tiles * tt, D_MODEL), F32)
        scratch.insert(1, pltpu.VMEM((ROW_TILE, tm, LANES), F32))
    return pl.pallas_call(
        functools.partial(_combine_body, tm, final, nb),
        grid=(n_tiles,),
        in_specs=in_specs,
        out_specs=out_spec,
        out_shape=out_shape,
        scratch_shapes=scratch,
        compiler_params=pltpu.CompilerParams(dimension_semantics=("arbitrary",)),
        name="combine",
    )(*args)


def _to_tm_body(nb, tt, x_ref, tail_ref, o_ref, slab):
    last = pl.program_id(0) == pl.num_programs(0) - 1

    @pl.when(jnp.logical_not(last))
    def _():
        for b in range(nb):
            for k in range(ROW_TILE):
                slab[k, pl.ds(b, tt, stride=nb), :] = x_ref[b, :, k * LANES:(k + 1) * LANES]
        for k in range(ROW_TILE):
            o_ref[:, k * LANES:(k + 1) * LANES] = slab[k]

    @pl.when(last)
    def _():
        o_ref[0:tail_ref.shape[0], :] = tail_ref[...]


def _from_tm_body(nb, tt, x_ref, o_ref, slab):
    for k in range(ROW_TILE):
        slab[k] = x_ref[:, k * LANES:(k + 1) * LANES]
    for b in range(nb):
        for k in range(ROW_TILE):
            o_ref[b, :, k * LANES:(k + 1) * LANES] = slab[k, pl.ds(b, tt, stride=nb), :]


def _to_tm_call(x, tail, tt):
    nb, t, d = x.shape
    ns = tail.shape[0]
    assert ns <= tt * nb
    nsteps = t // tt
    return pl.pallas_call(
        functools.partial(_to_tm_body, nb, tt),
        grid=(nsteps + 1,),
        in_specs=[pl.BlockSpec((nb, tt, d), lambda i: (0, jnp.minimum(i, nsteps - 1), 0)),
                  _const_spec((ns, d))],
        out_specs=pl.BlockSpec((tt * nb, d), lambda i: (i, 0)),
        out_shape=jax.ShapeDtypeStruct((nb * t + ns, d), F32),
        scratch_shapes=[pltpu.VMEM((ROW_TILE, tt * nb, LANES), F32)],
        compiler_params=pltpu.CompilerParams(dimension_semantics=("arbitrary",)),
        name="to_tm",
    )(x, tail)


def _from_tm_call(h, nb, t, tt):
    d = h.shape[1]
    return pl.pallas_call(
        functools.partial(_from_tm_body, nb, tt),
        grid=(t // tt,),
        in_specs=[pl.BlockSpec((tt * nb, d), lambda i: (i, 0))],
        out_specs=pl.BlockSpec((nb, tt, d), lambda i: (0, i, 0)),
        out_shape=jax.ShapeDtypeStruct((nb, t, d), F32),
        scratch_shapes=[pltpu.VMEM((ROW_TILE, tt * nb, LANES), F32)],
        compiler_params=pltpu.CompilerParams(dimension_semantics=("arbitrary",)),
        name="from_tm",
    )(h)


def _rms_body(x_ref, g_ref, o_ref):
    o_ref[...] = _rms(x_ref[...], g_ref[...])


def _rms_call(h, g, tm):
    n = h.shape[0]
    return pl.pallas_call(
        _rms_body, grid=(n // tm,),
        in_specs=[pl.BlockSpec((tm, D_MODEL), lambda i: (i, 0)), _const_spec((1, D_MODEL))],
        out_specs=pl.BlockSpec((tm, D_MODEL), lambda i: (i, 0)),
        out_shape=jax.ShapeDtypeStruct((n, D_MODEL), F32),
        name="rms",
    )(h, g)


MOE_TM = 512
MOE_TF = 512
TOK_TM = 768
FFN_TF = 512


def _moe_layer(h, norm_g, idx, gates, w1, w3, w2, e0, final_g=None, final_bm=None):
    n = h.shape[0]
    e_flat = idx[:, 0:2].reshape(2 * n)
    onehot = (e_flat[:, None] == jnp.arange(N_EXPERTS, dtype=I32)[None, :]).astype(I32)
    csum = jnp.cumsum(onehot, axis=0)
    rank = jnp.sum((csum - onehot) * onehot, axis=1)
    counts = csum[-1]
    padded = ((counts + MOE_TM - 1) // MOE_TM) * MOE_TM
    ends = jnp.cumsum(padded)
    starts = ends - padded
    pos = (jnp.sum(onehot * starts[None, :], axis=1) + rank).astype(I32)
    n_tiles = (2 * n) // MOE_TM + N_EXPERTS
    tile_start = jnp.arange(n_tiles, dtype=I32) * MOE_TM
    tile_valid = (tile_start < ends[-1]).astype(I32)
    tile_expert = jnp.minimum(jnp.sum((tile_start[:, None] >= ends[None, :]).astype(I32), axis=1),
                              N_EXPERTS - 1).astype(I32)
    last_valid = jnp.max(jnp.where(tile_valid > 0, tile_expert, 0))
    tile_expert = jnp.where(tile_valid > 0, tile_expert, last_valid)
    xs = _dispatch_call(h, norm_g, pos, ends.astype(I32), n_tiles * MOE_TM, MOE_TM, TOK_TM)
    prev_expert = jnp.concatenate([jnp.full((1,), -1, I32), tile_expert[:-1]])
    tile_first = jnp.logical_and(tile_expert != prev_expert, tile_valid > 0).astype(I32)
    y2d = _experts_call(xs.reshape(n_tiles * MOE_TM * ROW_TILE, LANES), tile_expert + e0, tile_valid, tile_first,
                        w1, w3, w2, MOE_TM, MOE_TF)
    y3 = y2d.reshape(n_tiles * MOE_TM, ROW_TILE, LANES)
    if final_g is None:
        return _combine_call(h, gates, pos, y3, TOK_TM, 0, n // TOK_TM)
    nb, n_first = final_bm
    tm = n - n_first
    assert n_first % tm == 0 and tm % nb == 0
    y_first = _combine_call(h, gates, pos, y3, tm, 0, n_first // tm, final_g, nb)
    y_rest = _combine_call(h, gates, pos, y3, tm, n_first // tm, 1, final_g)
    return y_first, y_rest


def _block_diag(w):
    n, k, a, b = w.shape
    eye = jnp.eye(k, dtype=w.dtype)
    return jnp.einsum('lkab,kj->lkajb', w, eye).reshape(n, k * a, k * b)


def kernel(x_prompt, x_sample, state_conv, state_lru, state_s5_re, state_s5_im, state_pool, norm_mix_g, w_in, b_in, gmlp_ln_g, gmlp_ln_b, gmlp_w_s, gmlp_b_s, conv_w, conv_b, lru_w_a, lru_b_a, lru_w_x, lru_b_x, lru_lam, s5_a_re, s5_a_im, s5_log_dt, s5_b_re, s5_b_im, s5_c_re, s5_c_im, s5_d, s5_w_glu, s5_b_glu, pool_w, pool_scale, w_branch, w_out, norm_ffn_g, ffn_w1, ffn_w3, ffn_w2, router_w, moe_w1, moe_w3, moe_w2, norm_final_g):
    depth = w_in.shape[0]
    bp, tp, _ = x_prompt.shape
    bs, ts, _ = x_sample.shape
    n_p, n_s = bp * tp, bs * ts
    assert n_p % (CHUNK * bp) == 0 and n_p % n_s == 0

    h = _to_tm_call(x_prompt, jnp.transpose(x_sample, (1, 0, 2)).reshape(n_s, D_MODEL), CHUNK)

    lbr, lbi, bbr, bbi = _s5prep(s5_a_re, s5_a_im, s5_log_dt, s5_b_re, s5_b_im)
    lane_head = jnp.arange(D_BR) // HEAD_DIM

    rows = lambda x: x.reshape(depth, 1, -1)
    bf = lambda x: x.astype(BF16)
    common_tail = (
        conv_w, rows(conv_b),
        bf(_block_diag(lru_w_a)), rows(lru_b_a), bf(_block_diag(lru_w_x)), rows(lru_b_x), rows(lru_lam),
        rows(lbr), rows(lbi),
        bf(jnp.concatenate([_block_diag(bbr), _block_diag(bbi)], axis=2)),
        bf(_block_diag(jnp.transpose(s5_c_re, (0, 1, 3, 2)))),
        bf(_block_diag(jnp.transpose(s5_c_im, (0, 1, 3, 2)))), rows(s5_d),
        bf(s5_w_glu), rows(s5_b_glu),
        bf(_block_diag(pool_w)), rows(pool_scale),
        bf(w_branch.reshape(depth, N_BRANCH * D_BR, D_MODEL)), bf(w_out))
    common_head = (rows(norm_mix_g), bf(w_in), rows(b_in), rows(gmlp_ln_g), rows(gmlp_ln_b))
    bsm_p = jnp.transpose(gmlp_b_s[:, lane_head, :], (0, 2, 1))
    lp_p = common_head + (gmlp_w_s, bsm_p) + common_tail
    ws_small = jnp.transpose(gmlp_w_s[:, :, :ts, :ts], (0, 2, 3, 1))[..., lane_head].reshape(depth, ts * ts, D_BR)
    bsm_s = jnp.transpose(gmlp_b_s[:, lane_head, :ts], (0, 2, 1))
    lp_s = common_head + (ws_small, bsm_s) + common_tail
    st_s = (jnp.transpose(state_conv, (0, 2, 1, 3)).reshape(depth, (CONV_W - 1) * bs, D_BR),
            state_lru, state_s5_re.reshape(depth, bs, S5_W), state_s5_im.reshape(depth, bs, S5_W),
            jnp.transpose(state_pool, (0, 2, 1, 3)).reshape(depth, POOL_BUF * bs, D_BR))
    row = lambda x: x.reshape(1, -1)

    conv_p, lru_p, sre_p, sim_p, pool_p = [], [], [], [], []
    conv_s, lru_s, sre_s, sim_s, pool_s, v_s = [], [], [], [], [], []
    zeros_p = (jnp.zeros(((CONV_W - 1) * bp, D_BR), F32), jnp.zeros((bp, D_BR), F32),
               jnp.zeros((bp, S5_W), F32), jnp.zeros((bp, S5_W), F32),
               jnp.zeros((POOL_BUF * bp, D_BR), F32))
    for l in range(depth):
        router = None
        if l % 2 == 1:
            rw = jnp.pad(norm_ffn_g[l][:, None] * router_w[l // 2], ((0, 0), (0, LANES - N_EXPERTS)))
            rw_hi = rw.astype(BF16)
            router = (rw_hi, (rw - rw_hi.astype(F32)).astype(BF16))
        outs_p = _mixer_call(h, 0, tp // CHUNK, bp, CHUNK, 0, True, False, zeros_p, None, lp_p, l, router)
        outs_s = _mixer_call(outs_p[0], n_p // n_s, 1, bs, ts, PAST_LEN, False, True, st_s, l, lp_s, l, router)
        h = outs_s[0]
        for lst, o in zip((conv_p, lru_p, sre_p, sim_p, pool_p), outs_p[1:6]):
            lst.append(o)
        for lst, o in zip((conv_s, lru_s, sre_s, sim_s, pool_s, v_s), outs_s[1:7]):
            lst.append(o)

        if l % 2 == 0:
            k = l // 2
            h = _ffn_call(h, row(norm_ffn_g[l]), ffn_w1[k].astype(BF16), ffn_w3[k].astype(BF16),
                          ffn_w2[k].astype(BF16), TOK_TM, FFN_TF)
            if l == depth - 1:
                h = _rms_call(h, row(norm_final_g), TOK_TM)
        else:
            k = l // 2
            idx = jnp.concatenate([outs_p[-2], outs_s[-2]], axis=0)
            gates = jnp.concatenate([outs_p[-1], outs_s[-1]], axis=0)
            stk = lambda w: w.reshape((-1,) + w.shape[2:])
            if l == depth - 1:
                y_prompt, y_sample_tm = _moe_layer(h, row(norm_ffn_g[l]), idx, gates, stk(moe_w1), stk(moe_w3),
                                                   stk(moe_w2), k * N_EXPERTS, row(norm_final_g), (bp, n_p))
            else:
                h = _moe_layer(h, row(norm_ffn_g[l]), idx, gates, stk(moe_w1), stk(moe_w3), stk(moe_w2),
                               k * N_EXPERTS)

    def tm_to_bm(x, t, b):
        return jnp.transpose(x.reshape(t, b, x.shape[-1]), (1, 0, 2))

    if depth % 2 == 1:
        y_prompt = _from_tm_call(h, bp, tp, CHUNK)
        y_sample_tm = h[n_p:]
    y_sample = tm_to_bm(y_sample_tm, ts, bs)
    stack = lambda lst, f: jnp.stack([f(o) for o in lst])
    return (
        y_prompt, y_sample,
        stack(conv_p, lambda o: tm_to_bm(o, CONV_W - 1, bp)),
        stack(lru_p, lambda o: o),
        stack(sre_p, lambda o: o.reshape(bp, N_GROUPS_C, P_STATE)),
        stack(sim_p, lambda o: o.reshape(bp, N_GROUPS_C, P_STATE)),
        stack(pool_p, lambda o: tm_to_bm(o, POOL_BUF, bp)),
        stack(conv_s, lambda o: tm_to_bm(o, CONV_W - 1, bs)),
        stack(lru_s, lambda o: o),
        stack(sre_s, lambda o: o.reshape(bs, N_GROUPS_C, P_STATE)),
        stack(sim_s, lambda o: o.reshape(bs, N_GROUPS_C, P_STATE)),
        stack(pool_s, lambda o: tm_to_bm(o, POOL_BUF, bs)),
        stack(v_s, lambda o: tm_to_bm(o, ts, bs)),
    )
```

```python
import functools
import math

import jax
import jax.numpy as jnp
from jax import lax
from jax.experimental import pallas as pl
from jax.experimental.pallas import tpu as pltpu

F32 = jnp.float32
BF16 = jnp.bfloat16
I32 = jnp.int32

SUBLANES = 8
LANES = 128

D_MODEL = 1024
D_BR = 256
N_BRANCH = 4
N_HEADS = 4
HEAD_DIM = D_BR // N_HEADS
CHUNK = 128
CONV_W = 4
LRU_C = 8.0
N_GROUPS_C = 16
GROUP_C = 16
P_STATE = 64
S5_W = N_GROUPS_C * P_STATE
POOL_WINDOWS = (2, 4, 8, 16)
POOL_BUF = max(POOL_WINDOWS) - 1
GROUP_D = D_BR // len(POOL_WINDOWS)
COLS_A = 5 * D_BR
IN_COLS = COLS_A + N_BRANCH * D_MODEL
N_EXPERTS = 8
EPS = 1e-6
PAST_LEN = 16384
SQRT_2_OVER_PI = math.sqrt(2.0 / math.pi)
ROW_TILE = D_MODEL // LANES


def _gelu(x):
    return x * (0.5 * (1.0 + jnp.tanh(SQRT_2_OVER_PI * (x + 0.044715 * (x * x * x)))))


def _sigmoid(x):
    return 1.0 / (1.0 + jnp.exp(-x))


def _rms(x, g):
    return x * lax.rsqrt(jnp.mean(x * x, axis=-1, keepdims=True) + EPS) * g


def _dot(a, b):
    return jnp.dot(a, b, preferred_element_type=F32)


def _top2(logits):
    lane = lax.broadcasted_iota(I32, logits.shape, 1)
    neg = jnp.float32(-jnp.inf)
    logits = jnp.where(lane < N_EXPERTS, logits, neg)
    m1 = jnp.max(logits, axis=-1, keepdims=True)
    i1 = jnp.min(jnp.where(logits == m1, lane, LANES), axis=-1, keepdims=True)
    rest = jnp.where(lane == i1, neg, logits)
    m2 = jnp.max(rest, axis=-1, keepdims=True)
    i2 = jnp.min(jnp.where(rest == m2, lane, LANES), axis=-1, keepdims=True)
    ex = jnp.exp(m2 - m1)
    g1 = 1.0 / (1.0 + ex)
    g2 = ex / (1.0 + ex)
    idx = jnp.where(lane == 0, i1, jnp.where(lane == 1, i2, 0))
    gate = jnp.where(lane == 0, g1, jnp.where(lane == 1, g2, 0.0))
    return idx, gate


def _const_spec(shape, single=False):
    nd = len(shape)
    if single:
        return pl.BlockSpec(shape, lambda *_: (0,) * nd, pipeline_mode=pl.Buffered(1))
    return pl.BlockSpec(shape, lambda *_: (0,) * nd)


def _layer_spec(shape, layer, single=False):
    idx = (layer,) + (0,) * (len(shape) - 1)
    blk = (None,) + tuple(shape[1:])
    if single:
        return pl.BlockSpec(blk, lambda *_: idx, pipeline_mode=pl.Buffered(1))
    return pl.BlockSpec(blk, lambda *_: idx)


def _mixer_body(cfg, *refs):
    B, Tt, start_pos, mm_gmlp, emit_v, route = cfg
    R = Tt * B
    refs = list(refs)
    h_ref = refs.pop(0)
    (conv0_ref, lru0_ref, sre0_ref, sim0_ref, pool0_ref,
     ng_ref, win_ref, bin_ref, lng_ref, lnb_ref, ws_ref, bsm_ref,
     cw_ref, cb_ref, wa_ref, ba_ref, wx_ref, bx_ref, lam_ref,
     lbr_ref, lbi_ref, bblk_ref, cre_ref, cim_ref, d_ref, wglu_ref, bglu_ref,
     pw_ref, ps_ref, wb_ref, wo_ref) = refs[:31]
    n_in = 33 if route else 31
    if route:
        rwh_ref, rwl_ref = refs[31:33]
    n_out = 6 + int(emit_v) + 2 * int(route)
    outs = refs[n_in:n_in + n_out]
    out_ref, convo_ref, lruo_ref, sreo_ref, simo_ref, poolo_ref = outs[:6]
    if route:
        idx_ref, gate_ref = outs[-2:]
    (xn_s, cext_s, pext_s, lruh_s, sre_s, sim_s, u_s, v_s, mix_s, xc_s,
     a_s, b_s, yc_s, yd_s, s5_s) = refs[n_in + n_out:]

    i = pl.program_id(0)
    SB = min(R, 256)
    SB5 = max(B, min(R, 512))
    lane256 = lax.broadcasted_iota(I32, (1, D_BR), 1)
    ydt = yc_s.dtype

    def mm(a, w):
        return _dot(a.astype(BF16), w)

    def init_carries():
        cext_s[0:(CONV_W - 1) * B, :] = conv0_ref[...]
        pext_s[0:POOL_BUF * B, :] = pool0_ref[...]
        lruh_s[...] = lru0_ref[...]
        sre_s[...] = sre0_ref[...]
        sim_s[...] = sim0_ref[...]

    def blocks(n, size, fn):
        for k in range(n):
            fn(k * size)

    def stage_in(r0):
        rows = pl.ds(r0, SB)
        xn = _rms(h_ref[rows, :], ng_ref[...]).astype(BF16)
        xn_s[rows, :] = xn
        pa = _dot(xn, win_ref[:, 0:COLS_A]) + bin_ref[:, 0:COLS_A]
        u_s[rows, :] = _gelu(pa[:, 0:D_BR])
        gv = _gelu(pa[:, D_BR:2 * D_BR])
        mu = jnp.mean(gv, axis=-1, keepdims=True)
        var = jnp.mean(jnp.square(gv - mu), axis=-1, keepdims=True)
        v = (gv - mu) * lax.rsqrt(var + EPS) * lng_ref[...] + lnb_ref[...]
        v_s[0, rows, :] = v[:, 0:LANES]
        v_s[1, rows, :] = v[:, LANES:2 * LANES]
        cext_s[pl.ds(r0 + (CONV_W - 1) * B, SB), :] = pa[:, 2 * D_BR:3 * D_BR]
        xc_s[rows, :] = pa[:, 3 * D_BR:4 * D_BR]
        pext_s[pl.ds(r0 + POOL_BUF * B, SB), :] = pa[:, 4 * D_BR:5 * D_BR]

    def stage_gmlp():
        if mm_gmlp:
            tril = (lax.broadcasted_iota(I32, (CHUNK, CHUNK), 0)
                    >= lax.broadcasted_iota(I32, (CHUNK, CHUNK), 1))
            wm = [jnp.where(tril, ws_ref[hd], 0.0).astype(BF16) for hd in range(N_HEADS)]
            head = lane256 // HEAD_DIM
            for b in range(B):
                vb = jnp.concatenate([v_s[0, pl.ds(b, Tt, stride=B), :],
                                      v_s[1, pl.ds(b, Tt, stride=B), :]], axis=1).astype(BF16)
                mixed = bsm_ref[...]
                for hd in range(N_HEADS):
                    mixed = mixed + jnp.where(head == hd, mm(wm[hd], vb), 0.0)
                mix_s[0, pl.ds(b, Tt, stride=B), :] = mixed[:, 0:LANES]
                mix_s[1, pl.ds(b, Tt, stride=B), :] = mixed[:, LANES:2 * LANES]
        else:
            for t in range(Tt):
                for half in range(2):
                    lo = half * LANES
                    acc = jnp.broadcast_to(bsm_ref[t:t + 1, lo:lo + LANES], (B, LANES))
                    for s in range(t + 1):
                        w = ws_ref[t * Tt + s:t * Tt + s + 1, lo:lo + LANES]
                        acc = acc + w * v_s[half, s * B:(s + 1) * B, :]
                    mix_s[half, t * B:(t + 1) * B, :] = acc
        if emit_v:
            vo_ref = outs[6]
            vo_ref[:, 0:LANES] = v_s[0]
            vo_ref[:, LANES:2 * LANES] = v_s[1]

    log_sig_lam = (jnp.minimum(lam_ref[...], 0.0)
                   - jnp.log1p(jnp.exp(-jnp.abs(lam_ref[...]))))

    def stage_lru(r0):
        rows = pl.ds(r0, SB)
        conv = cb_ref[...] + cext_s[pl.ds(r0, SB), :] * cw_ref[0:1, :]
        for k in range(1, CONV_W):
            conv = conv + cext_s[pl.ds(r0 + k * B, SB), :] * cw_ref[k:k + 1, :]
        cbf = conv.astype(BF16)
        r = _sigmoid(mm(cbf, wa_ref[...]) + ba_ref[...])
        ig = _sigmoid(mm(cbf, wx_ref[...]) + bx_ref[...])
        a = jnp.exp(LRU_C * r * log_sig_lam)
        mult = jnp.sqrt(1.0 - a * a)
        if start_pos == 0:
            row = lax.broadcasted_iota(I32, (SB, D_BR), 0) + r0
            mult = jnp.where(jnp.logical_and(i == 0, row < B), 1.0, mult)
        a_s[rows, :] = a
        b_s[rows, :] = mult * ig * conv

    def lru_step(t, hprev):
        rows = pl.ds(t * B, B) if isinstance(t, int) else pl.ds(pl.multiple_of(t * B, B), B)
        hnew = a_s[rows, :] * hprev + b_s[rows, :]
        b_s[rows, :] = hnew
        return hnew

    def lru_scan():
        hl = lruh_s[...]
        for t in range(Tt):
            hl = lru_step(t, hl)
        lruh_s[...] = hl
        lruo_ref[...] = hl
        conv_tail = cext_s[R:R + (CONV_W - 1) * B, :]
        convo_ref[...] = conv_tail
        cext_s[0:(CONV_W - 1) * B, :] = conv_tail

    lbr = jnp.broadcast_to(lbr_ref[...], (B, S5_W))
    lbi = jnp.broadcast_to(lbi_ref[...], (B, S5_W))
    steps5 = SB5 // B

    def stage_s5(r0):
        rows = pl.ds(r0, SB5)
        xc = xc_s[rows, :]
        s5_s[...] = mm(xc, bblk_ref[...])

        def s5_step(t, carry):
            sre, sim = carry
            rr = pl.ds(t * B, B) if isinstance(t, int) else pl.ds(pl.multiple_of(t * B, B), B)
            nre = lbr * sre - lbi * sim + s5_s[rr, 0:S5_W]
            nim = lbr * sim + lbi * sre + s5_s[rr, S5_W:2 * S5_W]
            s5_s[rr, 0:S5_W] = nre
            s5_s[rr, S5_W:2 * S5_W] = nim
            return nre, nim

        carry = (sre_s[...], sim_s[...])
        for t in range(steps5):
            carry = s5_step(t, carry)
        sre_s[...] = carry[0]
        sim_s[...] = carry[1]
        y = (mm(s5_s[:, 0:S5_W], cre_ref[...])
             - mm(s5_s[:, S5_W:2 * S5_W], cim_ref[...])
             + d_ref[...] * xc)
        y = _gelu(y)
        y = y * _sigmoid(mm(y, wglu_ref[...]) + bglu_ref[...])
        yc_s[rows, :] = y.astype(ydt)

    wlane = jnp.where(lane256 < GROUP_D, POOL_WINDOWS[0],
                      jnp.where(lane256 < 2 * GROUP_D, POOL_WINDOWS[1],
                                jnp.where(lane256 < 3 * GROUP_D, POOL_WINDOWS[2], POOL_WINDOWS[3])))

    def stage_pool(r0):
        base = r0 + POOL_BUF * B
        tok = pext_s[pl.ds(base, SB), :]
        acc = tok
        sums = {}
        for j in range(1, max(POOL_WINDOWS)):
            acc = acc + pext_s[pl.ds(base - j * B, SB), :]
            if j + 1 in POOL_WINDOWS:
                sums[j + 1] = acc
        sel = jnp.where(lane256 < GROUP_D, sums[2],
                        jnp.where(lane256 < 2 * GROUP_D, sums[4],
                                  jnp.where(lane256 < 3 * GROUP_D, sums[8], sums[16])))
        if start_pos >= POOL_BUF:
            cnt = wlane.astype(F32)
        else:
            row = lax.broadcasted_iota(I32, (SB, D_BR), 0) + r0
            tpos = start_pos + i * Tt + lax.shift_right_logical(row, jnp.full_like(row, int(math.log2(B))))
            cnt = jnp.minimum(wlane, tpos + 1).astype(F32)
        diff = sel / cnt - tok
        yd_s[pl.ds(r0, SB), :] = (mm(diff, pw_ref[...]) * ps_ref[...]).astype(ydt)

    def pool_tail():
        tail = pext_s[R:R + POOL_BUF * B, :]
        poolo_ref[...] = tail
        pext_s[0:POOL_BUF * B, :] = tail

    SBM = min(R, 512)

    def stage_merge(r0):
        rows = pl.ds(r0, SBM)
        xn = xn_s[rows, :]
        ya = u_s[rows, :] * jnp.concatenate([mix_s[0, rows, :], mix_s[1, rows, :]], axis=1)
        ys = (ya, b_s[rows, :], yc_s[rows, :], yd_s[rows, :])
        merged = None
        for n in range(N_BRANCH):
            c0 = COLS_A + n * D_MODEL
            logits = _dot(xn, win_ref[:, c0:c0 + D_MODEL]) + bin_ref[:, c0:c0 + D_MODEL]
            term = _sigmoid(logits) * mm(ys[n], wb_ref[n * D_BR:(n + 1) * D_BR, :])
            merged = term if merged is None else merged + term
        out = h_ref[rows, :] + mm(merged, wo_ref[...])
        out_ref[rows, :] = out
        if route:
            ohi = out.astype(BF16)
            olo = (out - ohi.astype(F32)).astype(BF16)
            z = _dot(ohi, rwh_ref[...]) + (_dot(olo, rwh_ref[...]) + _dot(ohi, rwl_ref[...]))
            logits = z * lax.rsqrt(jnp.mean(out * out, axis=-1, keepdims=True) + EPS)
            idx, gate = _top2(logits)
            idx_ref[rows, :] = idx
            gate_ref[rows, :] = gate

    def stage_front(r0):
        stage_in(r0)
        stage_lru(r0)
        stage_pool(r0)

    def sequence_stages():
        blocks(R // SB, SB, stage_front)
        stage_gmlp()
        lru_scan()
        pool_tail()

    def merge_rows(r0):
        for j in range(SB5 // SBM):
            rj = r0 + j * SBM
            stage_merge(rj if isinstance(rj, int) else pl.multiple_of(rj, SBM))

    nb5 = R // SB5
    pl.when(i == 0)(init_carries)
    sequence_stages()
    stage_s5(0)
    for k in range(1, nb5):
        stage_s5(k * SB5)
        merge_rows((k - 1) * SB5)
    merge_rows((nb5 - 1) * SB5)
    sreo_ref[...] = sre_s[...]
    simo_ref[...] = sim_s[...]


def _mixer_call(h_all, blk0, nsteps, B, Tt, start_pos, mm_gmlp, emit_v, states, state_layer, lp, layer,
                router=None):
    R = Tt * B
    route = router is not None
    cfg = (B, Tt, start_pos, mm_gmlp, emit_v, route)
    small = list(states)
    params = list(lp)
    big = {1, 16, 24, 25}
    h_spec = pl.BlockSpec((R, D_MODEL), lambda i: (blk0 + i, 0))
    in_specs = [h_spec]
    if state_layer is None:
        in_specs += [_const_spec(a.shape) for a in small]
        st_shapes = [a.shape for a in small]
    else:
        in_specs += [_layer_spec(a.shape, state_layer) for a in small]
        st_shapes = [a.shape[1:] for a in small]
    in_specs += [_layer_spec(a.shape, layer, single=(k in big)) for k, a in enumerate(params)]
    out_shape = [jax.ShapeDtypeStruct(h_all.shape, F32)] + [jax.ShapeDtypeStruct(s, F32) for s in st_shapes]
    out_specs = [h_spec] + [_const_spec(s.shape) for s in out_shape[1:]]
    if emit_v:
        out_shape.append(jax.ShapeDtypeStruct((R, D_BR), F32))
        out_specs.append(_const_spec((R, D_BR)))
    extra = []
    if route:
        extra = list(router)
        in_specs += [_const_spec(a.shape) for a in extra]
        out_shape += [jax.ShapeDtypeStruct((nsteps * R, LANES), I32),
                      jax.ShapeDtypeStruct((nsteps * R, LANES), F32)]
        out_specs += [pl.BlockSpec((R, LANES), lambda i: (i, 0))] * 2
    SB5 = max(B, min(R, 512))
    scratch = [
        pltpu.VMEM((R, D_MODEL), BF16),
        pltpu.VMEM((R + (CONV_W - 1) * B, D_BR), F32),
        pltpu.VMEM((R + POOL_BUF * B, D_BR), F32),
        pltpu.VMEM((B, D_BR), F32),
        pltpu.VMEM((B, S5_W), F32), pltpu.VMEM((B, S5_W), F32),
        pltpu.VMEM((R, D_BR), F32),
        pltpu.VMEM((2, R, LANES), F32),
        pltpu.VMEM((2, R, LANES), F32),
        pltpu.VMEM((R, D_BR), F32),
        pltpu.VMEM((R, D_BR), F32), pltpu.VMEM((R, D_BR), F32),
        pltpu.VMEM((R, D_BR), BF16), pltpu.VMEM((R, D_BR), BF16),
        pltpu.VMEM((SB5, 2 * S5_W), F32),
    ]
    return pl.pallas_call(
        functools.partial(_mixer_body, cfg),
        grid=(nsteps,),
        in_specs=in_specs,
        out_specs=out_specs,
        out_shape=out_shape,
        scratch_shapes=scratch,
        input_output_aliases={0: 0},
        compiler_params=pltpu.CompilerParams(dimension_semantics=("arbitrary",),
                                             vmem_limit_bytes=60 * 1024 * 1024),
        name="mixer_b%d" % B,
    )(h_all, *small, *params, *extra)


def _s5prep_body(are_ref, aim_ref, ldt_ref, bre_ref, bim_ref, lbr_ref, lbi_ref, bbr_ref, bbi_ref):
    a_re, a_im = are_ref[...], aim_ref[...]
    dt = jnp.exp(ldt_ref[...])
    mag = jnp.exp(a_re * dt)
    lb_re = mag * jnp.cos(a_im * dt)
    lb_im = mag * jnp.sin(a_im * dt)
    den = a_re * a_re + a_im * a_im
    n_re = lb_re - 1.0
    q_re = (n_re * a_re + lb_im * a_im) / den
    q_im = (lb_im * a_re - n_re * a_im) / den
    lbr_ref[...] = lb_re
    lbi_ref[...] = lb_im
    bbr_ref[...] = q_re * bre_ref[...] - q_im * bim_ref[...]
    bbi_ref[...] = q_re * bim_ref[...] + q_im * bre_ref[...]


def _s5prep(a_re, a_im, log_dt, b_re, b_im):
    depth = a_re.shape[0]
    rows = depth * N_GROUPS_C * GROUP_C

    def rep(x):
        return jnp.broadcast_to(x[:, :, None, :], (depth, N_GROUPS_C, GROUP_C, P_STATE)).reshape(rows, P_STATE)

    ldt = jnp.broadcast_to(log_dt[:, :, None, None], (depth, N_GROUPS_C, GROUP_C, P_STATE)).reshape(rows, P_STATE)
    b_re_t = jnp.transpose(b_re, (0, 1, 3, 2)).reshape(rows, P_STATE)
    b_im_t = jnp.transpose(b_im, (0, 1, 3, 2)).reshape(rows, P_STATE)
    shp = jax.ShapeDtypeStruct((rows, P_STATE), F32)
    lbr, lbi, bbr, bbi = pl.pallas_call(_s5prep_body, out_shape=[shp] * 4, name="s5prep")(
        rep(a_re), rep(a_im), ldt, b_re_t, b_im_t)
    r4 = lambda x: x.reshape(depth, N_GROUPS_C, GROUP_C, P_STATE)
    return r4(lbr)[:, :, 0, :], r4(lbi)[:, :, 0, :], r4(bbr), r4(bbi)


def _ffn_body(chunks, x_ref, g_ref, w1_ref, w3_ref, w2_ref, o_ref):
    x = x_ref[...]
    xn = _rms(x, g_ref[...]).astype(BF16)
    acc = None
    for c0, cw in chunks:
        a = _dot(xn, w1_ref[:, c0:c0 + cw])
        b = _dot(xn, w3_ref[:, c0:c0 + cw])
        t = _dot((a * _sigmoid(a) * b).astype(BF16), w2_ref[c0:c0 + cw, :])
        acc = t if acc is None else acc + t
    o_ref[...] = x + acc


def _ffn_call(h, g, w1, w3, w2, tm, tf):
    dff = w1.shape[1]
    chunks = tuple((c0, min(tf, dff - c0)) for c0 in range(0, dff, tf))
    return pl.pallas_call(
        functools.partial(_ffn_body, chunks),
        grid=(h.shape[0] // tm,),
        in_specs=[pl.BlockSpec((tm, D_MODEL), lambda i: (i, 0)),
                  _const_spec((1, D_MODEL)),
                  _const_spec((D_MODEL, dff), single=True),
                  _const_spec((D_MODEL, dff), single=True),
                  _const_spec((dff, D_MODEL), single=True)],
        out_specs=pl.BlockSpec((tm, D_MODEL), lambda i: (i, 0)),
        out_shape=jax.ShapeDtypeStruct(h.shape, F32),
        input_output_aliases={0: 0},
        compiler_params=pltpu.CompilerParams(dimension_semantics=("arbitrary",),
                                             vmem_limit_bytes=56 * 1024 * 1024),
        name="ffn",
    )(h, g, w1, w3, w2)


def _row_copy_wait(hbm_ref, sem, nrows):
    pltpu.make_async_copy(hbm_ref.at[pl.ds(0, nrows)], hbm_ref.at[pl.ds(0, nrows)], sem).wait()


def _dispatch_body(tm, pad_rows, pos_ref, ends_ref, x_ref, g_ref, xs_ref, buf, zbuf, sem):
    @pl.when(pl.program_id(0) == 0)
    def _():
        zbuf[...] = jnp.zeros_like(zbuf)
        n_rows = xs_ref.shape[0]
        for e in range(N_EXPERTS):
            for start in (jnp.maximum(ends_ref[e] - pad_rows, 0), n_rows - (e + 1) * pad_rows):
                cp = pltpu.make_async_copy(zbuf, xs_ref.at[pl.ds(start, pad_rows)], sem.at[1])
                cp.start()
                cp.wait()

    xn = _rms(x_ref[...], g_ref[...])
    for k in range(ROW_TILE):
        buf[pl.ds(k, tm, stride=ROW_TILE), :] = xn[:, k * LANES:(k + 1) * LANES]

    def issue(r, c):
        src = buf.at[pl.ds(pl.multiple_of(r * ROW_TILE, ROW_TILE), ROW_TILE), :]
        pltpu.make_async_copy(src, xs_ref.at[pos_ref[0, 0, 2 * r]], sem.at[0]).start(priority=0)
        pltpu.make_async_copy(src, xs_ref.at[pos_ref[0, 0, 2 * r + 1]], sem.at[0]).start(priority=1)
        return c

    lax.fori_loop(0, tm, issue, 0)
    _row_copy_wait(xs_ref, sem.at[0], 2 * tm)


def _dispatch_call(h, g, pos, ends, n_rows, pad_rows, tm):
    n = h.shape[0]
    nt = n // tm
    return pl.pallas_call(
        functools.partial(_dispatch_body, tm, pad_rows),
        grid=(nt,),
        in_specs=[pl.BlockSpec((1, 1, 2 * tm), lambda i: (i, 0, 0), memory_space=pltpu.SMEM),
                  pl.BlockSpec(memory_space=pltpu.SMEM),
                  pl.BlockSpec((tm, D_MODEL), lambda i: (i, 0)),
                  _const_spec((1, D_MODEL))],
        out_specs=pl.BlockSpec(memory_space=pl.ANY),
        out_shape=jax.ShapeDtypeStruct((n_rows, ROW_TILE, LANES), F32),
        scratch_shapes=[pltpu.VMEM((tm * ROW_TILE, LANES), F32),
                        pltpu.VMEM((pad_rows, ROW_TILE, LANES), F32),
                        pltpu.SemaphoreType.DMA((2,))],
        compiler_params=pltpu.CompilerParams(dimension_semantics=("arbitrary",)),
        name="dispatch",
    )(pos.reshape(nt, 1, 2 * tm), ends, h, g)


def _experts_body(tm, nf, tf, te_ref, tv_ref, tfirst_ref, xs_ref, w1_hbm, w3_hbm, w2_hbm, y_ref,
                  x_s, c1_s, c3_s, c2_s, s1, s3, s2, sem):
    t = pl.program_id(0)
    e = te_ref[t]

    def chunk_copies(f, slot):
        cols = pl.ds(f * tf, tf)
        return (pltpu.make_async_copy(w1_hbm.at[e, :, cols], s1.at[slot], sem.at[slot, 0]),
                pltpu.make_async_copy(w3_hbm.at[e, :, cols], s3.at[slot], sem.at[slot, 1]),
                pltpu.make_async_copy(w2_hbm.at[e, cols, :], s2.at[slot], sem.at[slot, 2]))

    def load_rows():
        for k in range(ROW_TILE):
            x_s[:, k * LANES:(k + 1) * LANES] = xs_ref[pl.ds(k, tm, stride=ROW_TILE), :].astype(BF16)
        return x_s[...]

    def chunk_out(x, f):
        a = _dot(x, c1_s[f])
        b = _dot(x, c3_s[f])
        return _dot((a * _sigmoid(a) * b).astype(BF16), c2_s[f])

    def store_rows(acc):
        for k in range(ROW_TILE):
            y_ref[pl.ds(k, tm, stride=ROW_TILE), :] = acc[:, k * LANES:(k + 1) * LANES]

    @pl.when(tfirst_ref[t] > 0)
    def _():
        for cp in chunk_copies(0, 0):
            cp.start()
        x = load_rows()
        acc = None
        for f in range(nf):
            slot = f % 2
            if f + 1 < nf:
                for cp in chunk_copies(f + 1, 1 - slot):
                    cp.start()
            for cp in chunk_copies(f, slot):
                cp.wait()
            c1_s[f] = s1[slot].astype(BF16)
            c3_s[f] = s3[slot].astype(BF16)
            c2_s[f] = s2[slot].astype(BF16)
            term = chunk_out(x, f)
            acc = term if acc is None else acc + term
        store_rows(acc)

    @pl.when(jnp.logical_and(tv_ref[t] > 0, tfirst_ref[t] == 0))
    def _():
        x = load_rows()
        acc = None
        for f in range(nf):
            term = chunk_out(x, f)
            acc = term if acc is None else acc + term
        store_rows(acc)

    @pl.when(tv_ref[t] == 0)
    def _():
        y_ref[...] = jnp.zeros_like(y_ref)


def _experts_call(xs2d, tile_expert, tile_valid, tile_first, w1, w3, w2, tm, tf):
    rows = xs2d.shape[0] // ROW_TILE
    nt = rows // tm
    dff = w1.shape[2]
    nf = dff // tf
    grid_spec = pltpu.PrefetchScalarGridSpec(
        num_scalar_prefetch=3,
        grid=(nt,),
        in_specs=[pl.BlockSpec((tm * ROW_TILE, LANES), lambda t, te, tv, t1: (t, 0)),
                  pl.BlockSpec(memory_space=pl.ANY), pl.BlockSpec(memory_space=pl.ANY),
                  pl.BlockSpec(memory_space=pl.ANY)],
        out_specs=pl.BlockSpec((tm * ROW_TILE, LANES), lambda t, te, tv, t1: (t, 0)),
        scratch_shapes=[pltpu.VMEM((tm, D_MODEL), BF16),
                        pltpu.VMEM((nf, D_MODEL, tf), BF16), pltpu.VMEM((nf, D_MODEL, tf), BF16),
                        pltpu.VMEM((nf, tf, D_MODEL), BF16),
                        pltpu.VMEM((2, D_MODEL, tf), F32), pltpu.VMEM((2, D_MODEL, tf), F32),
                        pltpu.VMEM((2, tf, D_MODEL), F32),
                        pltpu.SemaphoreType.DMA((2, 3))])
    return pl.pallas_call(
        functools.partial(_experts_body, tm, nf, tf),
        grid_spec=grid_spec,
        out_shape=jax.ShapeDtypeStruct(xs2d.shape, F32),
        compiler_params=pltpu.CompilerParams(dimension_semantics=("arbitrary",),
                                             vmem_limit_bytes=56 * 1024 * 1024),
        name="experts",
    )(tile_expert, tile_valid, tile_first, xs2d, w1, w3, w2)


def _combine_body(tm, final, nb, pos_ref, h_ref, gate_ref, *rest):
    rest = list(rest)
    ng_ref = rest.pop(0) if final else None
    y_ref, o_ref, buf = rest[:3]
    slab = rest[3] if nb is not None else None
    sem = rest[-1]

    def issue(r, c):
        for j in range(2):
            dst = buf.at[j, pl.ds(pl.multiple_of(r * ROW_TILE, ROW_TILE), ROW_TILE), :]
            pltpu.make_async_copy(y_ref.at[pos_ref[0, 0, 2 * r + j]], dst, sem.at[0]).start(priority=j)
        return c

    lax.fori_loop(0, tm, issue, 0)
    _row_copy_wait(y_ref, sem.at[0], 2 * tm)
    g0 = gate_ref[:, 0:1]
    g1 = gate_ref[:, 1:2]

    def row_slab(k):
        cols = slice(k * LANES, (k + 1) * LANES)
        moe = (g0 * buf[0, pl.ds(k, tm, stride=ROW_TILE), :]
               + g1 * buf[1, pl.ds(k, tm, stride=ROW_TILE), :])
        return h_ref[:, cols] + moe

    if nb is None:
        for k in range(ROW_TILE):
            o_ref[:, k * LANES:(k + 1) * LANES] = row_slab(k)
        if final:
            o_ref[...] = _rms(o_ref[...], ng_ref[...])
    else:
        tt = tm // nb
        ssq = None
        for k in range(ROW_TILE):
            s = row_slab(k)
            slab[k] = s
            part = jnp.sum(s * s, axis=-1, keepdims=True)
            ssq = part if ssq is None else ssq + part
        scale = lax.rsqrt(ssq * (1.0 / D_MODEL) + EPS)
        for k in range(ROW_TILE):
            slab[k] = slab[k] * scale * ng_ref[:, k * LANES:(k + 1) * LANES]
        for b in range(nb):
            for k in range(ROW_TILE):
                o_ref[b, :, k * LANES:(k + 1) * LANES] = slab[k, pl.ds(b, tt, stride=nb), :]


def _combine_call(h, gates, pos, y3, tm, tile0, n_tiles, final_g=None, nb=None):
    final = final_g is not None
    n_pos = pos.shape[0] // (2 * tm)
    in_specs = [pl.BlockSpec((1, 1, 2 * tm), lambda i: (tile0 + i, 0, 0), memory_space=pltpu.SMEM),
                pl.BlockSpec((tm, D_MODEL), lambda i: (tile0 + i, 0)),
                pl.BlockSpec((tm, LANES), lambda i: (tile0 + i, 0))]
    args = [pos.reshape(n_pos, 1, 2 * tm), h, gates]
    if final:
        in_specs.append(_const_spec((1, D_MODEL)))
        args.append(final_g)
    in_specs.append(pl.BlockSpec(memory_space=pl.ANY))
    args.append(y3)
    scratch = [pltpu.VMEM((2, tm * ROW_TILE, LANES), F32), pltpu.SemaphoreType.DMA((1,))]
    if nb is None:
        out_spec = pl.BlockSpec((tm, D_MODEL), lambda i: (i, 0))
        out_shape = jax.ShapeDtypeStruct((n_tiles * tm, D_MODEL), F32)
    else:
        tt = tm // nb
        out_spec = pl.BlockSpec((nb, tt, D_MODEL), lambda i: (0, i, 0))
        out_shape = jax.ShapeDtypeStruct((nb, n_tiles * tt, D_MODEL), F32)
        scratch.insert(1, pltpu.VMEM((ROW_TILE, tm, LANES), F32))
    return pl.pallas_call(
        functools.partial(_combine_body, tm, final, nb),
        grid=(n_tiles,),
        in_specs=in_specs,
        out_specs=out_spec,
        out_shape=out_shape,
        scratch_shapes=scratch,
        compiler_params=pltpu.CompilerParams(dimension_semantics=("arbitrary",)),
        name="combine",
    )(*args)


def _to_tm_body(nb, tt, x_ref, tail_ref, o_ref, slab):
    last = pl.program_id(0) == pl.num_programs(0) - 1

    @pl.when(jnp.logical_not(last))
    def _():
        for b in range(nb):
            for k in range(ROW_TILE):
                slab[k, pl.ds(b, tt, stride=nb), :] = x_ref[b, :, k * LANES:(k + 1) * LANES]
        for k in range(ROW_TILE):
            o_ref[:, k * LANES:(k + 1) * LANES] = slab[k]

    @pl.when(last)
    def _():
        o_ref[0:tail_ref.shape[0], :] = tail_ref[...]


def _from_tm_body(nb, tt, x_ref, o_ref, slab):
    for k in range(ROW_TILE):
        slab[k] = x_ref[:, k * LANES:(k + 1) * LANES]
    for b in range(nb):
        for k in range(ROW_TILE):
            o_ref[b, :, k * LANES:(k + 1) * LANES] = slab[k, pl.ds(b, tt, stride=nb), :]


def _to_tm_call(x, tail, tt):
    nb, t, d = x.shape
    ns = tail.shape[0]
    assert ns <= tt * nb
    nsteps = t // tt
    return pl.pallas_call(
        functools.partial(_to_tm_body, nb, tt),
        grid=(nsteps + 1,),
        in_specs=[pl.BlockSpec((nb, tt, d), lambda i: (0, jnp.minimum(i, nsteps - 1), 0)),
                  _const_spec((ns, d))],
        out_specs=pl.BlockSpec((tt * nb, d), lambda i: (i, 0)),
        out_shape=jax.ShapeDtypeStruct((nb * t + ns, d), F32),
        scratch_shapes=[pltpu.VMEM((ROW_TILE, tt * nb, LANES), F32)],
        compiler_params=pltpu.CompilerParams(dimension_semantics=("arbitrary",)),
        name="to_tm",
    )(x, tail)


def _from_tm_call(h, nb, t, tt):
    d = h.shape[1]
    return pl.pallas_call(
        functools.partial(_from_tm_body, nb, tt),
        grid=(t // tt,),
        in_specs=[pl.BlockSpec((tt * nb, d), lambda i: (i, 0))],
        out_specs=pl.BlockSpec((nb, tt, d), lambda i: (0, i, 0)),
        out_shape=jax.ShapeDtypeStruct((nb, t, d), F32),
        scratch_shapes=[pltpu.VMEM((ROW_TILE, tt * nb, LANES), F32)],
        compiler_params=pltpu.CompilerParams(dimension_semantics=("arbitrary",)),
        name="from_tm",
    )(h)


def _rms_body(x_ref, g_ref, o_ref):
    o_ref[...] = _rms(x_ref[...], g_ref[...])


def _rms_call(h, g, tm):
    n = h.shape[0]
    return pl.pallas_call(
        _rms_body, grid=(n // tm,),
        in_specs=[pl.BlockSpec((tm, D_MODEL), lambda i: (i, 0)), _const_spec((1, D_MODEL))],
        out_specs=pl.BlockSpec((tm, D_MODEL), lambda i: (i, 0)),
        out_shape=jax.ShapeDtypeStruct((n, D_MODEL), F32),
        name="rms",
    )(h, g)


MOE_TM = 512
MOE_TF = 512
TOK_TM = 768
FFN_TF = 512


def _moe_layer(h, norm_g, idx, gates, w1, w3, w2, e0, final_g=None, final_bm=None):
    n = h.shape[0]
    e_flat = idx[:, 0:2].reshape(2 * n)
    onehot = (e_flat[:, None] == jnp.arange(N_EXPERTS, dtype=I32)[None, :]).astype(I32)
    csum = jnp.cumsum(onehot, axis=0)
    rank = jnp.sum((csum - onehot) * onehot, axis=1)
    counts = csum[-1]
    padded = ((counts + MOE_TM - 1) // MOE_TM) * MOE_TM
    ends = jnp.cumsum(padded)
    starts = ends - padded
    pos = (jnp.sum(onehot * starts[None, :], axis=1) + rank).astype(I32)
    n_tiles = (2 * n) // MOE_TM + N_EXPERTS
    tile_start = jnp.arange(n_tiles, dtype=I32) * MOE_TM
    tile_valid = (tile_start < ends[-1]).astype(I32)
    tile_expert = jnp.minimum(jnp.sum((tile_start[:, None] >= ends[None, :]).astype(I32), axis=1),
                              N_EXPERTS - 1).astype(I32)
    last_valid = jnp.max(jnp.where(tile_valid > 0, tile_expert, 0))
    tile_expert = jnp.where(tile_valid > 0, tile_expert, last_valid)
    xs = _dispatch_call(h, norm_g, pos, ends.astype(I32), n_tiles * MOE_TM, MOE_TM, TOK_TM)
    prev_expert = jnp.concatenate([jnp.full((1,), -1, I32), tile_expert[:-1]])
    tile_first = jnp.logical_and(tile_expert != prev_expert, tile_valid > 0).astype(I32)
    y2d = _experts_call(xs.reshape(n_tiles * MOE_TM * ROW_TILE, LANES), tile_expert + e0, tile_valid, tile_first,
                        w1, w3, w2, MOE_TM, MOE_TF)
    y3 = y2d.reshape(n_tiles * MOE_TM, ROW_TILE, LANES)
    if final_g is None:
        return _combine_call(h, gates, pos, y3, TOK_TM, 0, n // TOK_TM)
    nb, n_first = final_bm
    tm = n - n_first
    assert n_first % tm == 0 and tm % nb == 0
    y_first = _combine_call(h, gates, pos, y3, tm, 0, n_first // tm, final_g, nb)
    y_rest = _combine_call(h, gates, pos, y3, tm, n_first // tm, 1, final_g)
    return y_first, y_rest


def _block_diag(w):
    n, k, a, b = w.shape
    eye = jnp.eye(k, dtype=w.dtype)
    return jnp.einsum('lkab,kj->lkajb', w, eye).reshape(n, k * a, k * b)


def kernel(x_prompt, x_sample, state_conv, state_lru, state_s5_re, state_s5_im, state_pool, norm_mix_g, w_in, b_in, gmlp_ln_g, gmlp_ln_b, gmlp_w_s, gmlp_b_s, conv_w, conv_b, lru_w_a, lru_b_a, lru_w_x, lru_b_x, lru_lam, s5_a_re, s5_a_im, s5_log_dt, s5_b_re, s5_b_im, s5_c_re, s5_c_im, s5_d, s5_w_glu, s5_b_glu, pool_w, pool_scale, w_branch, w_out, norm_ffn_g, ffn_w1, ffn_w3, ffn_w2, router_w, moe_w1, moe_w3, moe_w2, norm_final_g):
    depth = w_in.shape[0]
    bp, tp, _ = x_prompt.shape
    bs, ts, _ = x_sample.shape
    n_p, n_s = bp * tp, bs * ts
    assert n_p % (CHUNK * bp) == 0 and n_p % n_s == 0

    h = _to_tm_call(x_prompt, jnp.transpose(x_sample, (1, 0, 2)).reshape(n_s, D_MODEL), CHUNK)

    lbr, lbi, bbr, bbi = _s5prep(s5_a_re, s5_a_im, s5_log_dt, s5_b_re, s5_b_im)
    lane_head = jnp.arange(D_BR) // HEAD_DIM

    rows = lambda x: x.reshape(depth, 1, -1)
    bf = lambda x: x.astype(BF16)
    common_tail = (
        conv_w, rows(conv_b),
        bf(_block_diag(lru_w_a)), rows(lru_b_a), bf(_block_diag(lru_w_x)), rows(lru_b_x), rows(lru_lam),
        rows(lbr), rows(lbi),
        bf(jnp.concatenate([_block_diag(bbr), _block_diag(bbi)], axis=2)),
        bf(_block_diag(jnp.transpose(s5_c_re, (0, 1, 3, 2)))),
        bf(_block_diag(jnp.transpose(s5_c_im, (0, 1, 3, 2)))), rows(s5_d),
        bf(s5_w_glu), rows(s5_b_glu),
        bf(_block_diag(pool_w)), rows(pool_scale),
        bf(w_branch.reshape(depth, N_BRANCH * D_BR, D_MODEL)), bf(w_out))
    common_head = (rows(norm_mix_g), bf(w_in), rows(b_in), rows(gmlp_ln_g), rows(gmlp_ln_b))
    bsm_p = jnp.transpose(gmlp_b_s[:, lane_head, :], (0, 2, 1))
    lp_p = common_head + (gmlp_w_s, bsm_p) + common_tail
    ws_small = jnp.transpose(gmlp_w_s[:, :, :ts, :ts], (0, 2, 3, 1))[..., lane_head].reshape(depth, ts * ts, D_BR)
    bsm_s = jnp.transpose(gmlp_b_s[:, lane_head, :ts], (0, 2, 1))
    lp_s = common_head + (ws_small, bsm_s) + common_tail
    st_s = (jnp.transpose(state_conv, (0, 2, 1, 3)).reshape(depth, (CONV_W - 1) * bs, D_BR),
            state_lru, state_s5_re.reshape(depth, bs, S5_W), state_s5_im.reshape(depth, bs, S5_W),
            jnp.transpose(state_pool, (0, 2, 1, 3)).reshape(depth, POOL_BUF * bs, D_BR))
    row = lambda x: x.reshape(1, -1)

    conv_p, lru_p, sre_p, sim_p, pool_p = [], [], [], [], []
    conv_s, lru_s, sre_s, sim_s, pool_s, v_s = [], [], [], [], [], []
    zeros_p = (jnp.zeros(((CONV_W - 1) * bp, D_BR), F32), jnp.zeros((bp, D_BR), F32),
               jnp.zeros((bp, S5_W), F32), jnp.zeros((bp, S5_W), F32),
               jnp.zeros((POOL_BUF * bp, D_BR), F32))
    for l in range(depth):
        router = None
        if l % 2 == 1:
            rw = jnp.pad(norm_ffn_g[l][:, None] * router_w[l // 2], ((0, 0), (0, LANES - N_EXPERTS)))
            rw_hi = rw.astype(BF16)
            router = (rw_hi, (rw - rw_hi.astype(F32)).astype(BF16))
        outs_p = _mixer_call(h, 0, tp // CHUNK, bp, CHUNK, 0, True, False, zeros_p, None, lp_p, l, router)
        outs_s = _mixer_call(outs_p[0], n_p // n_s, 1, bs, ts, PAST_LEN, False, True, st_s, l, lp_s, l, router)
        h = outs_s[0]
        for lst, o in zip((conv_p, lru_p, sre_p, sim_p, pool_p), outs_p[1:6]):
            lst.append(o)
        for lst, o in zip((conv_s, lru_s, sre_s, sim_s, pool_s, v_s), outs_s[1:7]):
            lst.append(o)

        if l % 2 == 0:
            k = l // 2
            h = _ffn_call(h, row(norm_ffn_g[l]), ffn_w1[k].astype(BF16), ffn_w3[k].astype(BF16),
                          ffn_w2[k].astype(BF16), TOK_TM, FFN_TF)
            if l == depth - 1:
                h = _rms_call(h, row(norm_final_g), TOK_TM)
        else:
            k = l // 2
            idx = jnp.concatenate([outs_p[-2], outs_s[-2]], axis=0)
            gates = jnp.concatenate([outs_p[-1], outs_s[-1]], axis=0)
            stk = lambda w: w.reshape((-1,) + w.shape[2:])
            if l == depth - 1:
                y_prompt, y_sample_tm = _moe_layer(h, row(norm_ffn_g[l]), idx, gates, stk(moe_w1), stk(moe_w3),
                                                   stk(moe_w2), k * N_EXPERTS, row(norm_final_g), (bp, n_p))
            else:
                h = _moe_layer(h, row(norm_ffn_g[l]), idx, gates, stk(moe_w1), stk(moe_w3), stk(moe_w2),
                               k * N_EXPERTS)

    def tm_to_bm(x, t, b):
        return jnp.transpose(x.reshape(t, b, x.shape[-1]), (1, 0, 2))

    if depth % 2 == 1:
        y_prompt = _from_tm_call(h, bp, tp, CHUNK)
        y_sample_tm = h[n_p:]
    y_sample = tm_to_bm(y_sample_tm, ts, bs)
    stack = lambda lst, f: jnp.stack([f(o) for o in lst])
    return (
        y_prompt, y_sample,
        stack(conv_p, lambda o: tm_to_bm(o, CONV_W - 1, bp)),
        stack(lru_p, lambda o: o),
        stack(sre_p, lambda o: o.reshape(bp, N_GROUPS_C, P_STATE)),
        stack(sim_p, lambda o: o.reshape(bp, N_GROUPS_C, P_STATE)),
        stack(pool_p, lambda o: tm_to_bm(o, POOL_BUF, bp)),
        stack(conv_s, lambda o: tm_to_bm(o, CONV_W - 1, bs)),
        stack(lru_s, lambda o: o),
        stack(sre_s, lambda o: o.reshape(bs, N_GROUPS_C, P_STATE)),
        stack(sim_s, lambda o: o.reshape(bs, N_GROUPS_C, P_STATE)),
        stack(pool_s, lambda o: tm_to_bm(o, POOL_BUF, bs)),
        stack(v_s, lambda o: tm_to_bm(o, ts, bs)),
    )
```

```python
import functools
import math

import jax
import jax.numpy as jnp
from jax import lax
from jax.experimental import pallas as pl
from jax.experimental.pallas import tpu as pltpu

F32 = jnp.float32
BF16 = jnp.bfloat16
I32 = jnp.int32

SUBLANES = 8
LANES = 128

D_MODEL = 1024
D_BR = 256
N_BRANCH = 4
N_HEADS = 4
HEAD_DIM = D_BR // N_HEADS
CHUNK = 128
CONV_W = 4
LRU_C = 8.0
N_GROUPS_C = 16
GROUP_C = 16
P_STATE = 64
S5_W = N_GROUPS_C * P_STATE
POOL_WINDOWS = (2, 4, 8, 16)
POOL_BUF = max(POOL_WINDOWS) - 1
GROUP_D = D_BR // len(POOL_WINDOWS)
COLS_A = 5 * D_BR
IN_COLS = COLS_A + N_BRANCH * D_MODEL
N_EXPERTS = 8
EPS = 1e-6
PAST_LEN = 16384
SQRT_2_OVER_PI = math.sqrt(2.0 / math.pi)
ROW_TILE = D_MODEL // LANES


def _gelu(x):
    return x * (0.5 * (1.0 + jnp.tanh(SQRT_2_OVER_PI * (x + 0.044715 * (x * x * x)))))


def _sigmoid(x):
    return 1.0 / (1.0 + jnp.exp(-x))


def _rms(x, g):
    return x * lax.rsqrt(jnp.mean(x * x, axis=-1, keepdims=True) + EPS) * g


def _dot(a, b):
    return jnp.dot(a, b, preferred_element_type=F32)


def _top2(logits):
    lane = lax.broadcasted_iota(I32, logits.shape, 1)
    neg = jnp.float32(-jnp.inf)
    logits = jnp.where(lane < N_EXPERTS, logits, neg)
    m1 = jnp.max(logits, axis=-1, keepdims=True)
    i1 = jnp.min(jnp.where(logits == m1, lane, LANES), axis=-1, keepdims=True)
    rest = jnp.where(lane == i1, neg, logits)
    m2 = jnp.max(rest, axis=-1, keepdims=True)
    i2 = jnp.min(jnp.where(rest == m2, lane, LANES), axis=-1, keepdims=True)
    ex = jnp.exp(m2 - m1)
    g1 = 1.0 / (1.0 + ex)
    g2 = ex / (1.0 + ex)
    idx = jnp.where(lane == 0, i1, jnp.where(lane == 1, i2, 0))
    gate = jnp.where(lane == 0, g1, jnp.where(lane == 1, g2, 0.0))
    return idx, gate


def _const_spec(shape, single=False):
    nd = len(shape)
    if single:
        return pl.BlockSpec(shape, lambda *_: (0,) * nd, pipeline_mode=pl.Buffered(1))
    return pl.BlockSpec(shape, lambda *_: (0,) * nd)


def _layer_spec(shape, layer, single=False):
    idx = (layer,) + (0,) * (len(shape) - 1)
    blk = (None,) + tuple(shape[1:])
    if single:
        return pl.BlockSpec(blk, lambda *_: idx, pipeline_mode=pl.Buffered(1))
    return pl.BlockSpec(blk, lambda *_: idx)


def _mixer_body(cfg, *refs):
    B, Tt, start_pos, mm_gmlp, emit_v, route = cfg
    R = Tt * B
    refs = list(refs)
    h_ref = refs.pop(0)
    (conv0_ref, lru0_ref, sre0_ref, sim0_ref, pool0_ref,
     ng_ref, win_ref, bin_ref, lng_ref, lnb_ref, ws_ref, bsm_ref,
     cw_ref, cb_ref, wa_ref, ba_ref, wx_ref, bx_ref, lam_ref,
     lbr_ref, lbi_ref, bblk_ref, cre_ref, cim_ref, d_ref, wglu_ref, bglu_ref,
     pw_ref, ps_ref, wb_ref, wo_ref) = refs[:31]
    n_in = 33 if route else 31
    if route:
        rwh_ref, rwl_ref = refs[31:33]
    n_out = 6 + int(emit_v) + 2 * int(route)
    outs = refs[n_in:n_in + n_out]
    out_ref, convo_ref, lruo_ref, sreo_ref, simo_ref, poolo_ref = outs[:6]
    if route:
        idx_ref, gate_ref = outs[-2:]
    (xn_s, cext_s, pext_s, lruh_s, sre_s, sim_s, u_s, v_s, mix_s, xc_s,
     a_s, b_s, yc_s, yd_s, s5_s) = refs[n_in + n_out:]

    i = pl.program_id(0)
    SB = min(R, 256)
    SB5 = max(B, min(R, 256))
    lane256 = lax.broadcasted_iota(I32, (1, D_BR), 1)
    ydt = yc_s.dtype

    def mm(a, w):
        return _dot(a.astype(BF16), w)

    def init_carries():
        cext_s[0:(CONV_W - 1) * B, :] = conv0_ref[...]
        pext_s[0:POOL_BUF * B, :] = pool0_ref[...]
        lruh_s[...] = lru0_ref[...]
        sre_s[...] = sre0_ref[...]
        sim_s[...] = sim0_ref[...]

    def blocks(n, size, fn):
        for k in range(n):
            fn(k * size)

    def stage_in(r0):
        rows = pl.ds(r0, SB)
        xn = _rms(h_ref[rows, :], ng_ref[...]).astype(BF16)
        xn_s[rows, :] = xn
        pa = _dot(xn, win_ref[:, 0:COLS_A]) + bin_ref[:, 0:COLS_A]
        u_s[rows, :] = _gelu(pa[:, 0:D_BR])
        gv = _gelu(pa[:, D_BR:2 * D_BR])
        mu = jnp.mean(gv, axis=-1, keepdims=True)
        var = jnp.mean(jnp.square(gv - mu), axis=-1, keepdims=True)
        v = (gv - mu) * lax.rsqrt(var + EPS) * lng_ref[...] + lnb_ref[...]
        v_s[0, rows, :] = v[:, 0:LANES]
        v_s[1, rows, :] = v[:, LANES:2 * LANES]
        cext_s[pl.ds(r0 + (CONV_W - 1) * B, SB), :] = pa[:, 2 * D_BR:3 * D_BR]
        xc_s[rows, :] = pa[:, 3 * D_BR:4 * D_BR]
        pext_s[pl.ds(r0 + POOL_BUF * B, SB), :] = pa[:, 4 * D_BR:5 * D_BR]

    def stage_gmlp():
        if mm_gmlp:
            tril = (lax.broadcasted_iota(I32, (CHUNK, CHUNK), 0)
                    >= lax.broadcasted_iota(I32, (CHUNK, CHUNK), 1))
            wm = [jnp.where(tril, ws_ref[hd], 0.0).astype(BF16) for hd in range(N_HEADS)]
            head = lane256 // HEAD_DIM
            for b in range(B):
                vb = jnp.concatenate([v_s[0, pl.ds(b, Tt, stride=B), :],
                                      v_s[1, pl.ds(b, Tt, stride=B), :]], axis=1).astype(BF16)
                mixed = bsm_ref[...]
                for hd in range(N_HEADS):
                    mixed = mixed + jnp.where(head == hd, mm(wm[hd], vb), 0.0)
                mix_s[0, pl.ds(b, Tt, stride=B), :] = mixed[:, 0:LANES]
                mix_s[1, pl.ds(b, Tt, stride=B), :] = mixed[:, LANES:2 * LANES]
        else:
            for t in range(Tt):
                for half in range(2):
                    lo = half * LANES
                    acc = jnp.broadcast_to(bsm_ref[t:t + 1, lo:lo + LANES], (B, LANES))
                    for s in range(t + 1):
                        w = ws_ref[t * Tt + s:t * Tt + s + 1, lo:lo + LANES]
                        acc = acc + w * v_s[half, s * B:(s + 1) * B, :]
                    mix_s[half, t * B:(t + 1) * B, :] = acc
        if emit_v:
            vo_ref = outs[6]
            vo_ref[:, 0:LANES] = v_s[0]
            vo_ref[:, LANES:2 * LANES] = v_s[1]

    log_sig_lam = (jnp.minimum(lam_ref[...], 0.0)
                   - jnp.log1p(jnp.exp(-jnp.abs(lam_ref[...]))))

    def stage_lru(r0):
        rows = pl.ds(r0, SB)
        conv = cb_ref[...] + cext_s[pl.ds(r0, SB), :] * cw_ref[0:1, :]
        for k in range(1, CONV_W):
            conv = conv + cext_s[pl.ds(r0 + k * B, SB), :] * cw_ref[k:k + 1, :]
        cbf = conv.astype(BF16)
        r = _sigmoid(mm(cbf, wa_ref[...]) + ba_ref[...])
        ig = _sigmoid(mm(cbf, wx_ref[...]) + bx_ref[...])
        a = jnp.exp(LRU_C * r * log_sig_lam)
        mult = jnp.sqrt(1.0 - a * a)
        if start_pos == 0:
            row = lax.broadcasted_iota(I32, (SB, D_BR), 0) + r0
            mult = jnp.where(jnp.logical_and(i == 0, row < B), 1.0, mult)
        a_s[rows, :] = a
        b_s[rows, :] = mult * ig * conv

    def lru_step(t, hprev):
        rows = pl.ds(t * B, B) if isinstance(t, int) else pl.ds(pl.multiple_of(t * B, B), B)
        hnew = a_s[rows, :] * hprev + b_s[rows, :]
        b_s[rows, :] = hnew
        return hnew

    def lru_scan():
        hl = lruh_s[...]
        for t in range(Tt):
            hl = lru_step(t, hl)
        lruh_s[...] = hl
        lruo_ref[...] = hl
        conv_tail = cext_s[R:R + (CONV_W - 1) * B, :]
        convo_ref[...] = conv_tail
        cext_s[0:(CONV_W - 1) * B, :] = conv_tail

    lbr = jnp.broadcast_to(lbr_ref[...], (B, S5_W))
    lbi = jnp.broadcast_to(lbi_ref[...], (B, S5_W))
    steps5 = SB5 // B

    def stage_s5(r0):
        rows = pl.ds(r0, SB5)
        xc = xc_s[rows, :]
        s5_s[...] = mm(xc, bblk_ref[...])

        def s5_step(t, carry):
            sre, sim = carry
            rr = pl.ds(t * B, B) if isinstance(t, int) else pl.ds(pl.multiple_of(t * B, B), B)
            nre = lbr * sre - lbi * sim + s5_s[rr, 0:S5_W]
            nim = lbr * sim + lbi * sre + s5_s[rr, S5_W:2 * S5_W]
            s5_s[rr, 0:S5_W] = nre
            s5_s[rr, S5_W:2 * S5_W] = nim
            return nre, nim

        carry = (sre_s[...], sim_s[...])
        for t in range(steps5):
            carry = s5_step(t, carry)
        sre_s[...] = carry[0]
        sim_s[...] = carry[1]
        y = (mm(s5_s[:, 0:S5_W], cre_ref[...])
             - mm(s5_s[:, S5_W:2 * S5_W], cim_ref[...])
             + d_ref[...] * xc)
        y = _gelu(y)
        y = y * _sigmoid(mm(y, wglu_ref[...]) + bglu_ref[...])
        yc_s[rows, :] = y.astype(ydt)

    wlane = jnp.where(lane256 < GROUP_D, POOL_WINDOWS[0],
                      jnp.where(lane256 < 2 * GROUP_D, POOL_WINDOWS[1],
                                jnp.where(lane256 < 3 * GROUP_D, POOL_WINDOWS[2], POOL_WINDOWS[3])))

    def stage_pool(r0):
        base = r0 + POOL_BUF * B
        tok = pext_s[pl.ds(base, SB), :]
        acc = tok
        sums = {}
        for j in range(1, max(POOL_WINDOWS)):
            acc = acc + pext_s[pl.ds(base - j * B, SB), :]
            if j + 1 in POOL_WINDOWS:
                sums[j + 1] = acc
        sel = jnp.where(lane256 < GROUP_D, sums[2],
                        jnp.where(lane256 < 2 * GROUP_D, sums[4],
                                  jnp.where(lane256 < 3 * GROUP_D, sums[8], sums[16])))
        if start_pos >= POOL_BUF:
            cnt = wlane.astype(F32)
        else:
            row = lax.broadcasted_iota(I32, (SB, D_BR), 0) + r0
            tpos = start_pos + i * Tt + lax.shift_right_logical(row, jnp.full_like(row, int(math.log2(B))))
            cnt = jnp.minimum(wlane, tpos + 1).astype(F32)
        diff = sel / cnt - tok
        yd_s[pl.ds(r0, SB), :] = (mm(diff, pw_ref[...]) * ps_ref[...]).astype(ydt)

    def pool_tail():
        tail = pext_s[R:R + POOL_BUF * B, :]
        poolo_ref[...] = tail
        pext_s[0:POOL_BUF * B, :] = tail

    SBM = min(R, 256)

    def stage_merge(r0):
        rows = pl.ds(r0, SBM)
        xn = xn_s[rows, :]
        ya = u_s[rows, :] * jnp.concatenate([mix_s[0, rows, :], mix_s[1, rows, :]], axis=1)
        ys = (ya, b_s[rows, :], yc_s[rows, :], yd_s[rows, :])
        merged = None
        for n in range(N_BRANCH):
            c0 = COLS_A + n * D_MODEL
            logits = _dot(xn, win_ref[:, c0:c0 + D_MODEL]) + bin_ref[:, c0:c0 + D_MODEL]
            term = _sigmoid(logits) * mm(ys[n], wb_ref[n * D_BR:(n + 1) * D_BR, :])
            merged = term if merged is None else merged + term
        out = h_ref[rows, :] + mm(merged, wo_ref[...])
        out_ref[rows, :] = out
        if route:
            ohi = out.astype(BF16)
            olo = (out - ohi.astype(F32)).astype(BF16)
            z = _dot(ohi, rwh_ref[...]) + (_dot(olo, rwh_ref[...]) + _dot(ohi, rwl_ref[...]))
            logits = z * lax.rsqrt(jnp.mean(out * out, axis=-1, keepdims=True) + EPS)
            idx, gate = _top2(logits)
            idx_ref[rows, :] = idx
            gate_ref[rows, :] = gate

    def stage_front(r0):
        stage_in(r0)
        stage_lru(r0)
        stage_pool(r0)

    def sequence_stages():
        blocks(R // SB, SB, stage_front)
        stage_gmlp()
        lru_scan()
        pool_tail()

    def merge_rows(r0):
        for j in range(SB5 // SBM):
            rj = r0 + j * SBM
            stage_merge(rj if isinstance(rj, int) else pl.multiple_of(rj, SBM))

    nb5 = R // SB5
    pl.when(i == 0)(init_carries)
    sequence_stages()
    stage_s5(0)
    for k in range(1, nb5):
        stage_s5(k * SB5)
        merge_rows((k - 1) * SB5)
    merge_rows((nb5 - 1) * SB5)
    sreo_ref[...] = sre_s[...]
    simo_ref[...] = sim_s[...]


def _mixer_call(h_all, blk0, nsteps, B, Tt, start_pos, mm_gmlp, emit_v, states, state_layer, lp, layer,
                router=None):
    R = Tt * B
    route = router is not None
    cfg = (B, Tt, start_pos, mm_gmlp, emit_v, route)
    small = list(states)
    params = list(lp)
    big = {1, 16, 24, 25}
    h_spec = pl.BlockSpec((R, D_MODEL), lambda i: (blk0 + i, 0))
    in_specs = [h_spec]
    if state_layer is None:
        in_specs += [_const_spec(a.shape) for a in small]
        st_shapes = [a.shape for a in small]
    else:
        in_specs += [_layer_spec(a.shape, state_layer) for a in small]
        st_shapes = [a.shape[1:] for a in small]
    in_specs += [_layer_spec(a.shape, layer, single=(k in big)) for k, a in enumerate(params)]
    out_shape = [jax.ShapeDtypeStruct(h_all.shape, F32)] + [jax.ShapeDtypeStruct(s, F32) for s in st_shapes]
    out_specs = [h_spec] + [_const_spec(s.shape) for s in out_shape[1:]]
    if emit_v:
        out_shape.append(jax.ShapeDtypeStruct((R, D_BR), F32))
        out_specs.append(_const_spec((R, D_BR)))
    extra = []
    if route:
        extra = list(router)
        in_specs += [_const_spec(a.shape) for a in extra]
        out_shape += [jax.ShapeDtypeStruct((nsteps * R, LANES), I32),
                      jax.ShapeDtypeStruct((nsteps * R, LANES), F32)]
        out_specs += [pl.BlockSpec((R, LANES), lambda i: (i, 0))] * 2
    SB5 = max(B, min(R, 256))
    scratch = [
        pltpu.VMEM((R, D_MODEL), BF16),
        pltpu.VMEM((R + (CONV_W - 1) * B, D_BR), F32),
        pltpu.VMEM((R + POOL_BUF * B, D_BR), F32),
        pltpu.VMEM((B, D_BR), F32),
        pltpu.VMEM((B, S5_W), F32), pltpu.VMEM((B, S5_W), F32),
        pltpu.VMEM((R, D_BR), F32),
        pltpu.VMEM((2, R, LANES), F32),
        pltpu.VMEM((2, R, LANES), F32),
        pltpu.VMEM((R, D_BR), F32),
        pltpu.VMEM((R, D_BR), F32), pltpu.VMEM((R, D_BR), F32),
        pltpu.VMEM((R, D_BR), BF16), pltpu.VMEM((R, D_BR), BF16),
        pltpu.VMEM((SB5, 2 * S5_W), F32),
    ]
    return pl.pallas_call(
        functools.partial(_mixer_body, cfg),
        grid=(nsteps,),
        in_specs=in_specs,
        out_specs=out_specs,
        out_shape=out_shape,
        scratch_shapes=scratch,
        input_output_aliases={0: 0},
        compiler_params=pltpu.CompilerParams(dimension_semantics=("arbitrary",),
                                             vmem_limit_bytes=60 * 1024 * 1024),
        name="mixer_b%d" % B,
    )(h_all, *small, *params, *extra)


def _s5prep_body(are_ref, aim_ref, ldt_ref, bre_ref, bim_ref, lbr_ref, lbi_ref, bbr_ref, bbi_ref):
    a_re, a_im = are_ref[...], aim_ref[...]
    dt = jnp.exp(ldt_ref[...])
    mag = jnp.exp(a_re * dt)
    lb_re = mag * jnp.cos(a_im * dt)
    lb_im = mag * jnp.sin(a_im * dt)
    den = a_re * a_re + a_im * a_im
    n_re = lb_re - 1.0
    q_re = (n_re * a_re + lb_im * a_im) / den
    q_im = (lb_im * a_re - n_re * a_im) / den
    lbr_ref[...] = lb_re
    lbi_ref[...] = lb_im
    bbr_ref[...] = q_re * bre_ref[...] - q_im * bim_ref[...]
    bbi_ref[...] = q_re * bim_ref[...] + q_im * bre_ref[...]


def _s5prep(a_re, a_im, log_dt, b_re, b_im):
    depth = a_re.shape[0]
    rows = depth * N_GROUPS_C * GROUP_C

    def rep(x):
        return jnp.broadcast_to(x[:, :, None, :], (depth, N_GROUPS_C, GROUP_C, P_STATE)).reshape(rows, P_STATE)

    ldt = jnp.broadcast_to(log_dt[:, :, None, None], (depth, N_GROUPS_C, GROUP_C, P_STATE)).reshape(rows, P_STATE)
    b_re_t = jnp.transpose(b_re, (0, 1, 3, 2)).reshape(rows, P_STATE)
    b_im_t = jnp.transpose(b_im, (0, 1, 3, 2)).reshape(rows, P_STATE)
    shp = jax.ShapeDtypeStruct((rows, P_STATE), F32)
    lbr, lbi, bbr, bbi = pl.pallas_call(_s5prep_body, out_shape=[shp] * 4, name="s5prep")(
        rep(a_re), rep(a_im), ldt, b_re_t, b_im_t)
    r4 = lambda x: x.reshape(depth, N_GROUPS_C, GROUP_C, P_STATE)
    return r4(lbr)[:, :, 0, :], r4(lbi)[:, :, 0, :], r4(bbr), r4(bbi)


def _ffn_body(chunks, x_ref, g_ref, w1_ref, w3_ref, w2_ref, o_ref):
    x = x_ref[...]
    xn = _rms(x, g_ref[...]).astype(BF16)
    acc = None
    for c0, cw in chunks:
        a = _dot(xn, w1_ref[:, c0:c0 + cw])
        b = _dot(xn, w3_ref[:, c0:c0 + cw])
        t = _dot((a * _sigmoid(a) * b).astype(BF16), w2_ref[c0:c0 + cw, :])
        acc = t if acc is None else acc + t
    o_ref[...] = x + acc


def _ffn_call(h, g, w1, w3, w2, tm, tf):
    dff = w1.shape[1]
    chunks = tuple((c0, min(tf, dff - c0)) for c0 in range(0, dff, tf))
    return pl.pallas_call(
        functools.partial(_ffn_body, chunks),
        grid=(h.shape[0] // tm,),
        in_specs=[pl.BlockSpec((tm, D_MODEL), lambda i: (i, 0)),
                  _const_spec((1, D_MODEL)),
                  _const_spec((D_MODEL, dff), single=True),
                  _const_spec((D_MODEL, dff), single=True),
                  _const_spec((dff, D_MODEL), single=True)],
        out_specs=pl.BlockSpec((tm, D_MODEL), lambda i: (i, 0)),
        out_shape=jax.ShapeDtypeStruct(h.shape, F32),
        input_output_aliases={0: 0},
        compiler_params=pltpu.CompilerParams(dimension_semantics=("arbitrary",),
                                             vmem_limit_bytes=56 * 1024 * 1024),
        name="ffn",
    )(h, g, w1, w3, w2)


def _row_copy_wait(hbm_ref, sem, nrows):
    pltpu.make_async_copy(hbm_ref.at[pl.ds(0, nrows)], hbm_ref.at[pl.ds(0, nrows)], sem).wait()


def _dispatch_body(tm, pad_rows, pos_ref, ends_ref, x_ref, g_ref, xs_ref, buf, zbuf, sem):
    @pl.when(pl.program_id(0) == 0)
    def _():
        zbuf[...] = jnp.zeros_like(zbuf)
        n_rows = xs_ref.shape[0]
        for e in range(N_EXPERTS):
            for start in (jnp.maximum(ends_ref[e] - pad_rows, 0), n_rows - (e + 1) * pad_rows):
                cp = pltpu.make_async_copy(zbuf, xs_ref.at[pl.ds(start, pad_rows)], sem.at[1])
                cp.start()
                cp.wait()

    xn = _rms(x_ref[...], g_ref[...])
    for k in range(ROW_TILE):
        buf[pl.ds(k, tm, stride=ROW_TILE), :] = xn[:, k * LANES:(k + 1) * LANES]

    def issue(r, c):
        src = buf.at[pl.ds(pl.multiple_of(r * ROW_TILE, ROW_TILE), ROW_TILE), :]
        pltpu.make_async_copy(src, xs_ref.at[pos_ref[0, 0, 2 * r]], sem.at[0]).start(priority=0)
        pltpu.make_async_copy(src, xs_ref.at[pos_ref[0, 0, 2 * r + 1]], sem.at[0]).start(priority=1)
        return c

    lax.fori_loop(0, tm, issue, 0)
    _row_copy_wait(xs_ref, sem.at[0], 2 * tm)


def _dispatch_call(h, g, pos, ends, n_rows, pad_rows, tm):
    n = h.shape[0]
    nt = n // tm
    return pl.pallas_call(
        functools.partial(_dispatch_body, tm, pad_rows),
        grid=(nt,),
        in_specs=[pl.BlockSpec((1, 1, 2 * tm), lambda i: (i, 0, 0), memory_space=pltpu.SMEM),
                  pl.BlockSpec(memory_space=pltpu.SMEM),
                  pl.BlockSpec((tm, D_MODEL), lambda i: (i, 0)),
                  _const_spec((1, D_MODEL))],
        out_specs=pl.BlockSpec(memory_space=pl.ANY),
        out_shape=jax.ShapeDtypeStruct((n_rows, ROW_TILE, LANES), F32),
        scratch_shapes=[pltpu.VMEM((tm * ROW_TILE, LANES), F32),
                        pltpu.VMEM((pad_rows, ROW_TILE, LANES), F32),
                        pltpu.SemaphoreType.DMA((2,))],
        compiler_params=pltpu.CompilerParams(dimension_semantics=("arbitrary",)),
        name="dispatch",
    )(pos.reshape(nt, 1, 2 * tm), ends, h, g)


def _experts_body(tm, nf, tf, te_ref, tv_ref, tfirst_ref, xs_ref, w1_hbm, w3_hbm, w2_hbm, y_ref,
                  x_s, c1_s, c3_s, c2_s, s1, s3, s2, sem):
    t = pl.program_id(0)
    e = te_ref[t]

    def chunk_copies(f, slot):
        cols = pl.ds(f * tf, tf)
        return (pltpu.make_async_copy(w1_hbm.at[e, :, cols], s1.at[slot], sem.at[slot, 0]),
                pltpu.make_async_copy(w3_hbm.at[e, :, cols], s3.at[slot], sem.at[slot, 1]),
                pltpu.make_async_copy(w2_hbm.at[e, cols, :], s2.at[slot], sem.at[slot, 2]))

    def load_rows():
        for k in range(ROW_TILE):
            x_s[:, k * LANES:(k + 1) * LANES] = xs_ref[pl.ds(k, tm, stride=ROW_TILE), :].astype(BF16)
        return x_s[...]

    def chunk_out(x, f):
        a = _dot(x, c1_s[f])
        b = _dot(x, c3_s[f])
        return _dot((a * _sigmoid(a) * b).astype(BF16), c2_s[f])

    def store_rows(acc):
        for k in range(ROW_TILE):
            y_ref[pl.ds(k, tm, stride=ROW_TILE), :] = acc[:, k * LANES:(k + 1) * LANES]

    @pl.when(tfirst_ref[t] > 0)
    def _():
        for cp in chunk_copies(0, 0):
            cp.start()
        x = load_rows()
        acc = None
        for f in range(nf):
            slot = f % 2
            if f + 1 < nf:
                for cp in chunk_copies(f + 1, 1 - slot):
                    cp.start()
            for cp in chunk_copies(f, slot):
                cp.wait()
            c1_s[f] = s1[slot].astype(BF16)
            c3_s[f] = s3[slot].astype(BF16)
            c2_s[f] = s2[slot].astype(BF16)
            term = chunk_out(x, f)
            acc = term if acc is None else acc + term
        store_rows(acc)

    @pl.when(jnp.logical_and(tv_ref[t] > 0, tfirst_ref[t] == 0))
    def _():
        x = load_rows()
        acc = None
        for f in range(nf):
            term = chunk_out(x, f)
            acc = term if acc is None else acc + term
        store_rows(acc)

    @pl.when(tv_ref[t] == 0)
    def _():
        y_ref[...] = jnp.zeros_like(y_ref)


def _experts_call(xs2d, tile_expert, tile_valid, tile_first, w1, w3, w2, tm, tf):
    rows = xs2d.shape[0] // ROW_TILE
    nt = rows // tm
    dff = w1.shape[2]
    nf = dff // tf
    grid_spec = pltpu.PrefetchScalarGridSpec(
        num_scalar_prefetch=3,
        grid=(nt,),
        in_specs=[pl.BlockSpec((tm * ROW_TILE, LANES), lambda t, te, tv, t1: (t, 0)),
                  pl.BlockSpec(memory_space=pl.ANY), pl.BlockSpec(memory_space=pl.ANY),
                  pl.BlockSpec(memory_space=pl.ANY)],
        out_specs=pl.BlockSpec((tm * ROW_TILE, LANES), lambda t, te, tv, t1: (t, 0)),
        scratch_shapes=[pltpu.VMEM((tm, D_MODEL), BF16),
                        pltpu.VMEM((nf, D_MODEL, tf), BF16), pltpu.VMEM((nf, D_MODEL, tf), BF16),
                        pltpu.VMEM((nf, tf, D_MODEL), BF16),
                        pltpu.VMEM((2, D_MODEL, tf), F32), pltpu.VMEM((2, D_MODEL, tf), F32),
                        pltpu.VMEM((2, tf, D_MODEL), F32),
                        pltpu.SemaphoreType.DMA((2, 3))])
    return pl.pallas_call(
        functools.partial(_experts_body, tm, nf, tf),
        grid_spec=grid_spec,
        out_shape=jax.ShapeDtypeStruct(xs2d.shape, F32),
        compiler_params=pltpu.CompilerParams(dimension_semantics=("arbitrary",),
                                             vmem_limit_bytes=56 * 1024 * 1024),
        name="experts",
    )(tile_expert, tile_valid, tile_first, xs2d, w1, w3, w2)


def _combine_body(tm, final, nb, pos_ref, posn_ref, h_ref, gate_ref, *rest):
    rest = list(rest)
    ng_ref = rest.pop(0) if final else None
    y_ref, o_ref, ring = rest[:3]
    slab = rest[3] if nb is not None else None
    sem = rest[-1]
    i = pl.program_id(0)

    def gather(p_ref, slot):
        def issue(r, c):
            for j in range(2):
                dst = ring.at[slot, j, pl.ds(pl.multiple_of(r * ROW_TILE, ROW_TILE), ROW_TILE), :]
                pltpu.make_async_copy(y_ref.at[p_ref[0, 0, 2 * r + j]], dst, sem.at[slot]).start(priority=j)
            return c
        lax.fori_loop(0, tm, issue, 0)

    slot = lax.rem(i, 2)

    @pl.when(i == 0)
    def _():
        gather(pos_ref, 0)

    @pl.when(i + 1 < pl.num_programs(0))
    def _():
        gather(posn_ref, 1 - slot)

    _row_copy_wait(y_ref, sem.at[slot], 2 * tm)
    buf = ring.at[slot]
    g0 = gate_ref[:, 0:1]
    g1 = gate_ref[:, 1:2]

    def row_slab(k):
        cols = slice(k * LANES, (k + 1) * LANES)
        moe = (g0 * buf[0, pl.ds(k, tm, stride=ROW_TILE), :]
               + g1 * buf[1, pl.ds(k, tm, stride=ROW_TILE), :])
        return h_ref[:, cols] + moe

    if nb is None:
        for k in range(ROW_TILE):
            o_ref[:, k * LANES:(k + 1) * LANES] = row_slab(k)
        if final:
            o_ref[...] = _rms(o_ref[...], ng_ref[...])
    else:
        tt = tm // nb
        ssq = None
        for k in range(ROW_TILE):
            s = row_slab(k)
            slab[k] = s
            part = jnp.sum(s * s, axis=-1, keepdims=True)
            ssq = part if ssq is None else ssq + part
        scale = lax.rsqrt(ssq * (1.0 / D_MODEL) + EPS)
        for k in range(ROW_TILE):
            slab[k] = slab[k] * scale * ng_ref[:, k * LANES:(k + 1) * LANES]
        for b in range(nb):
            for k in range(ROW_TILE):
                o_ref[b, :, k * LANES:(k + 1) * LANES] = slab[k, pl.ds(b, tt, stride=nb), :]


def _combine_call(h, gates, pos, y3, tm, tile0, n_tiles, final_g=None, nb=None):
    final = final_g is not None
    n_pos = pos.shape[0] // (2 * tm)
    last = tile0 + n_tiles - 1
    pos3 = pos.reshape(n_pos, 1, 2 * tm)
    in_specs = [pl.BlockSpec((1, 1, 2 * tm), lambda i: (tile0 + i, 0, 0), memory_space=pltpu.SMEM),
                pl.BlockSpec((1, 1, 2 * tm), lambda i: (jnp.minimum(tile0 + i + 1, last), 0, 0),
                             memory_space=pltpu.SMEM),
                pl.BlockSpec((tm, D_MODEL), lambda i: (tile0 + i, 0)),
                pl.BlockSpec((tm, LANES), lambda i: (tile0 + i, 0))]
    args = [pos3, pos3, h, gates]
    if final:
        in_specs.append(_const_spec((1, D_MODEL)))
        args.append(final_g)
    in_specs.append(pl.BlockSpec(memory_space=pl.ANY))
    args.append(y3)
    scratch = [pltpu.VMEM((2, 2, tm * ROW_TILE, LANES), F32), pltpu.SemaphoreType.DMA((2,))]
    if nb is None:
        out_spec = pl.BlockSpec((tm, D_MODEL), lambda i: (i, 0))
        out_shape = jax.ShapeDtypeStruct((n_tiles * tm, D_MODEL), F32)
    else:
        tt = tm // nb
        out_spec = pl.BlockSpec((nb, tt, D_MODEL), lambda i: (0, i, 0))
        out_shape = jax.ShapeDtypeStruct((nb, n_tiles * tt, D_MODEL), F32)
        scratch.insert(1, pltpu.VMEM((ROW_TILE, tm, LANES), F32))
    return pl.pallas_call(
        functools.partial(_combine_body, tm, final, nb),
        grid=(n_tiles,),
        in_specs=in_specs,
        out_specs=out_spec,
        out_shape=out_shape,
        scratch_shapes=scratch,
        compiler_params=pltpu.CompilerParams(dimension_semantics=("arbitrary",)),
        name="combine",
    )(*args)


def _to_tm_body(nb, tt, x_ref, tail_ref, o_ref, slab):
    last = pl.program_id(0) == pl.num_programs(0) - 1

    @pl.when(jnp.logical_not(last))
    def _():
        for b in range(nb):
            for k in range(ROW_TILE):
                slab[k, pl.ds(b, tt, stride=nb), :] = x_ref[b, :, k * LANES:(k + 1) * LANES]
        for k in range(ROW_TILE):
            o_ref[:, k * LANES:(k + 1) * LANES] = slab[k]

    @pl.when(last)
    def _():
        o_ref[0:tail_ref.shape[0], :] = tail_ref[...]


def _from_tm_body(nb, tt, x_ref, o_ref, slab):
    for k in range(ROW_TILE):
        slab[k] = x_ref[:, k * LANES:(k + 1) * LANES]
    for b in range(nb):
        for k in range(ROW_TILE):
            o_ref[b, :, k * LANES:(k + 1) * LANES] = slab[k, pl.ds(b, tt, stride=nb), :]


def _to_tm_call(x, tail, tt):
    nb, t, d = x.shape
    ns = tail.shape[0]
    assert ns <= tt * nb
    nsteps = t // tt
    return pl.pallas_call(
        functools.partial(_to_tm_body, nb, tt),
        grid=(nsteps + 1,),
        in_specs=[pl.BlockSpec((nb, tt, d), lambda i: (0, jnp.minimum(i, nsteps - 1), 0)),
                  _const_spec((ns, d))],
        out_specs=pl.BlockSpec((tt * nb, d), lambda i: (i, 0)),
        out_shape=jax.ShapeDtypeStruct((nb * t + ns, d), F32),
        scratch_shapes=[pltpu.VMEM((ROW_TILE, tt * nb, LANES), F32)],
        compiler_params=pltpu.CompilerParams(dimension_semantics=("arbitrary",)),
        name="to_tm",
    )(x, tail)


def _from_tm_call(h, nb, t, tt):
    d = h.shape[1]
    return pl.pallas_call(
        functools.partial(_from_tm_body, nb, tt),
        grid=(t // tt,),
        in_specs=[pl.BlockSpec((tt * nb, d), lambda i: (i, 0))],
        out_specs=pl.BlockSpec((nb, tt, d), lambda i: (0, i, 0)),
        out_shape=jax.ShapeDtypeStruct((nb, t, d), F32),
        scratch_shapes=[pltpu.VMEM((ROW_TILE, tt * nb, LANES), F32)],
        compiler_params=pltpu.CompilerParams(dimension_semantics=("arbitrary",)),
        name="from_tm",
    )(h)


def _rms_body(x_ref, g_ref, o_ref):
    o_ref[...] = _rms(x_ref[...], g_ref[...])


def _rms_call(h, g, tm):
    n = h.shape[0]
    return pl.pallas_call(
        _rms_body, grid=(n // tm,),
        in_specs=[pl.BlockSpec((tm, D_MODEL), lambda i: (i, 0)), _const_spec((1, D_MODEL))],
        out_specs=pl.BlockSpec((tm, D_MODEL), lambda i: (i, 0)),
        out_shape=jax.ShapeDtypeStruct((n, D_MODEL), F32),
        name="rms",
    )(h, g)


MOE_TM = 512
MOE_TF = 512
TOK_TM = 768
FFN_TF = 512


def _moe_layer(h, norm_g, idx, gates, w1, w3, w2, e0, final_g=None, final_bm=None):
    n = h.shape[0]
    e_flat = idx[:, 0:2].reshape(2 * n)
    onehot = (e_flat[:, None] == jnp.arange(N_EXPERTS, dtype=I32)[None, :]).astype(I32)
    csum = jnp.cumsum(onehot, axis=0)
    rank = jnp.sum((csum - onehot) * onehot, axis=1)
    counts = csum[-1]
    padded = ((counts + MOE_TM - 1) // MOE_TM) * MOE_TM
    ends = jnp.cumsum(padded)
    starts = ends - padded
    pos = (jnp.sum(onehot * starts[None, :], axis=1) + rank).astype(I32)
    n_tiles = (2 * n) // MOE_TM + N_EXPERTS
    tile_start = jnp.arange(n_tiles, dtype=I32) * MOE_TM
    tile_valid = (tile_start < ends[-1]).astype(I32)
    tile_expert = jnp.minimum(jnp.sum((tile_start[:, None] >= ends[None, :]).astype(I32), axis=1),
                              N_EXPERTS - 1).astype(I32)
    last_valid = jnp.max(jnp.where(tile_valid > 0, tile_expert, 0))
    tile_expert = jnp.where(tile_valid > 0, tile_expert, last_valid)
    xs = _dispatch_call(h, norm_g, pos, ends.astype(I32), n_tiles * MOE_TM, MOE_TM, TOK_TM)
    prev_expert = jnp.concatenate([jnp.full((1,), -1, I32), tile_expert[:-1]])
    tile_first = jnp.logical_and(tile_expert != prev_expert, tile_valid > 0).astype(I32)
    y2d = _experts_call(xs.reshape(n_tiles * MOE_TM * ROW_TILE, LANES), tile_expert + e0, tile_valid, tile_first,
                        w1, w3, w2, MOE_TM, MOE_TF)
    y3 = y2d.reshape(n_tiles * MOE_TM, ROW_TILE, LANES)
    if final_g is None:
        return _combine_call(h, gates, pos, y3, TOK_TM, 0, n // TOK_TM)
    nb, n_first = final_bm
    tm = n - n_first
    assert n_first % tm == 0 and tm % nb == 0
    y_first = _combine_call(h, gates, pos, y3, tm, 0, n_first // tm, final_g, nb)
    y_rest = _combine_call(h, gates, pos, y3, tm, n_first // tm, 1, final_g)
    return y_first, y_rest


def _block_diag(w):
    n, k, a, b = w.shape
    eye = jnp.eye(k, dtype=w.dtype)
    return jnp.einsum('lkab,kj->lkajb', w, eye).reshape(n, k * a, k * b)


def kernel(x_prompt, x_sample, state_conv, state_lru, state_s5_re, state_s5_im, state_pool, norm_mix_g, w_in, b_in, gmlp_ln_g, gmlp_ln_b, gmlp_w_s, gmlp_b_s, conv_w, conv_b, lru_w_a, lru_b_a, lru_w_x, lru_b_x, lru_lam, s5_a_re, s5_a_im, s5_log_dt, s5_b_re, s5_b_im, s5_c_re, s5_c_im, s5_d, s5_w_glu, s5_b_glu, pool_w, pool_scale, w_branch, w_out, norm_ffn_g, ffn_w1, ffn_w3, ffn_w2, router_w, moe_w1, moe_w3, moe_w2, norm_final_g):
    depth = w_in.shape[0]
    bp, tp, _ = x_prompt.shape
    bs, ts, _ = x_sample.shape
    n_p, n_s = bp * tp, bs * ts
    assert n_p % (CHUNK * bp) == 0 and n_p % n_s == 0

    h = _to_tm_call(x_prompt, jnp.transpose(x_sample, (1, 0, 2)).reshape(n_s, D_MODEL), CHUNK)

    lbr, lbi, bbr, bbi = _s5prep(s5_a_re, s5_a_im, s5_log_dt, s5_b_re, s5_b_im)
    lane_head = jnp.arange(D_BR) // HEAD_DIM

    rows = lambda x: x.reshape(depth, 1, -1)
    bf = lambda x: x.astype(BF16)
    common_tail = (
        conv_w, rows(conv_b),
        bf(_block_diag(lru_w_a)), rows(lru_b_a), bf(_block_diag(lru_w_x)), rows(lru_b_x), rows(lru_lam),
        rows(lbr), rows(lbi),
        bf(jnp.concatenate([_block_diag(bbr), _block_diag(bbi)], axis=2)),
        bf(_block_diag(jnp.transpose(s5_c_re, (0, 1, 3, 2)))),
        bf(_block_diag(jnp.transpose(s5_c_im, (0, 1, 3, 2)))), rows(s5_d),
        bf(s5_w_glu), rows(s5_b_glu),
        bf(_block_diag(pool_w)), rows(pool_scale),
        bf(w_branch.reshape(depth, N_BRANCH * D_BR, D_MODEL)), bf(w_out))
    common_head = (rows(norm_mix_g), bf(w_in), rows(b_in), rows(gmlp_ln_g), rows(gmlp_ln_b))
    bsm_p = jnp.transpose(gmlp_b_s[:, lane_head, :], (0, 2, 1))
    lp_p = common_head + (gmlp_w_s, bsm_p) + common_tail
    ws_small = jnp.transpose(gmlp_w_s[:, :, :ts, :ts], (0, 2, 3, 1))[..., lane_head].reshape(depth, ts * ts, D_BR)
    bsm_s = jnp.transpose(gmlp_b_s[:, lane_head, :ts], (0, 2, 1))
    lp_s = common_head + (ws_small, bsm_s) + common_tail
    st_s = (jnp.transpose(state_conv, (0, 2, 1, 3)).reshape(depth, (CONV_W - 1) * bs, D_BR),
            state_lru, state_s5_re.reshape(depth, bs, S5_W), state_s5_im.reshape(depth, bs, S5_W),
            jnp.transpose(state_pool, (0, 2, 1, 3)).reshape(depth, POOL_BUF * bs, D_BR))
    row = lambda x: x.reshape(1, -1)

    conv_p, lru_p, sre_p, sim_p, pool_p = [], [], [], [], []
    conv_s, lru_s, sre_s, sim_s, pool_s, v_s = [], [], [], [], [], []
    zeros_p = (jnp.zeros(((CONV_W - 1) * bp, D_BR), F32), jnp.zeros((bp, D_BR), F32),
               jnp.zeros((bp, S5_W), F32), jnp.zeros((bp, S5_W), F32),
               jnp.zeros((POOL_BUF * bp, D_BR), F32))
    for l in range(depth):
        router = None
        if l % 2 == 1:
            rw = jnp.pad(norm_ffn_g[l][:, None] * router_w[l // 2], ((0, 0), (0, LANES - N_EXPERTS)))
            rw_hi = rw.astype(BF16)
            router = (rw_hi, (rw - rw_hi.astype(F32)).astype(BF16))
        outs_p = _mixer_call(h, 0, tp // CHUNK, bp, CHUNK, 0, True, False, zeros_p, None, lp_p, l, router)
        outs_s = _mixer_call(outs_p[0], n_p // n_s, 1, bs, ts, PAST_LEN, False, True, st_s, l, lp_s, l, router)
        h = outs_s[0]
        for lst, o in zip((conv_p, lru_p, sre_p, sim_p, pool_p), outs_p[1:6]):
            lst.append(o)
        for lst, o in zip((conv_s, lru_s, sre_s, sim_s, pool_s, v_s), outs_s[1:7]):
            lst.append(o)

        if l % 2 == 0:
            k = l // 2
            h = _ffn_call(h, row(norm_ffn_g[l]), ffn_w1[k].astype(BF16), ffn_w3[k].astype(BF16),
                          ffn_w2[k].astype(BF16), TOK_TM, FFN_TF)
            if l == depth - 1:
                h = _rms_call(h, row(norm_final_g), TOK_TM)
        else:
            k = l // 2
            idx = jnp.concatenate([outs_p[-2], outs_s[-2]], axis=0)
            gates = jnp.concatenate([outs_p[-1], outs_s[-1]], axis=0)
            stk = lambda w: w.reshape((-1,) + w.shape[2:])
            if l == depth - 1:
                y_prompt, y_sample_tm = _moe_layer(h, row(norm_ffn_g[l]), idx, gates, stk(moe_w1), stk(moe_w3),
                                                   stk(moe_w2), k * N_EXPERTS, row(norm_final_g), (bp, n_p))
            else:
                h = _moe_layer(h, row(norm_ffn_g[l]), idx, gates, stk(moe_w1), stk(moe_w3), stk(moe_w2),
                               k * N_EXPERTS)

    def tm_to_bm(x, t, b):
        return jnp.transpose(x.reshape(t, b, x.shape[-1]), (1, 0, 2))

    if depth % 2 == 1:
        y_prompt = _from_tm_call(h, bp, tp, CHUNK)
        y_sample_tm = h[n_p:]
    y_sample = tm_to_bm(y_sample_tm, ts, bs)
    stack = lambda lst, f: jnp.stack([f(o) for o in lst])
    return (
        y_prompt, y_sample,
        stack(conv_p, lambda o: tm_to_bm(o, CONV_W - 1, bp)),
        stack(lru_p, lambda o: o),
        stack(sre_p, lambda o: o.reshape(bp, N_GROUPS_C, P_STATE)),
        stack(sim_p, lambda o: o.reshape(bp, N_GROUPS_C, P_STATE)),
        stack(pool_p, lambda o: tm_to_bm(o, POOL_BUF, bp)),
        stack(conv_s, lambda o: tm_to_bm(o, CONV_W - 1, bs)),
        stack(lru_s, lambda o: o),
        stack(sre_s, lambda o: o.reshape(bs, N_GROUPS_C, P_STATE)),
        stack(sim_s, lambda o: o.reshape(bs, N_GROUPS_C, P_STATE)),
        stack(pool_s, lambda o: tm_to_bm(o, POOL_BUF, bs)),
        stack(v_s, lambda o: tm_to_bm(o, ts, bs)),
    )
```
